```python
import jax
import jax.numpy as jnp
from jax import lax
import numpy as np

D_MODEL = 1024
BATCH = 16
SEQ = 2048
DEPTH = 1
DEC_BATCH = 32
DEC_SEQ = 1
PAST_LEN = 16384
PAGE_SIZE = 128

N_HEADS = 8
KV_HEADS = 2
Q_PER_KV = N_HEADS // KV_HEADS
HEAD_DIM = 64
CMP_BLOCK = 32
CMP_STRIDE = 16
CMP_HID = 2 * HEAD_DIM
SEL_BLOCK = 64
TOP_N = 16
WINDOW = 512
Q_BLOCK = 32
C_CONV = D_MODEL // 2
CONV_W = 31
N_EXPERTS = 256
N_GROUPS = 8
TOPK_GROUPS = 4
TOP_K = 8
D_EXPERT = D_MODEL // 4
ROUTED_SCALE = 2.5
MOE_BLOCK = 128
EPS = 1e-6
NEG_INF = -1e30
FORCE_SCORE = 1e4
N_Q = N_HEADS * HEAD_DIM
N_KV = 6 * KV_HEADS * HEAD_DIM
N_GATE = 3 * N_HEADS
N_GLU = 2 * C_CONV
N_MERGE = 2 * D_MODEL
N_IN = N_Q + N_KV + N_GATE + N_GLU + N_MERGE
SPLITS = (N_Q, N_Q + N_KV, N_Q + N_KV + N_GATE, N_Q + N_KV + N_GATE + N_GLU)

kernel_name = 'nsa_conformer_moe_adaln_decode_step'


def rms_norm(x, g):
    xf = x.astype(jnp.float32)
    y = xf * lax.rsqrt(jnp.mean(xf * xf, axis=-1, keepdims=True) + EPS)
    return y.astype(x.dtype) * g


def layer_norm(x, g, b):
    xf = x.astype(jnp.float32)
    mu = jnp.mean(xf, axis=-1, keepdims=True)
    var = jnp.mean(jnp.square(xf - mu), axis=-1, keepdims=True)
    return ((xf - mu) * lax.rsqrt(var + EPS)).astype(x.dtype) * g + b


def swiglu(x, w1, w3, w2):
    return (jax.nn.silu(x @ w1) * (x @ w3)) @ w2


def masked_softmax(s, mask):
    p = jax.nn.softmax(jnp.where(mask, s, NEG_INF), axis=-1)
    return jnp.where(mask, p, 0.0)


def alibi_slopes():
    h = jnp.arange(1, N_HEADS + 1, dtype=jnp.float32)
    return jnp.exp2(-8.0 * h / N_HEADS).reshape(KV_HEADS, Q_PER_KV)


def compress(rows, pe, w1, w2):
    b, tk = rows.shape[:2]
    ch = rows.reshape(b, tk // CMP_STRIDE, CMP_STRIDE, KV_HEADS, HEAD_DIM)
    first = jnp.einsum('bclgd,ldh->bcgh', ch, w1[:CMP_STRIDE])
    second = jnp.einsum('bclgd,ldh->bcgh', ch, w1[CMP_STRIDE:])
    pre = first[:, :-1] + second[:, 1:] + jnp.einsum('ld,ldh->h', pe, w1)
    return jnp.einsum('bngh,hd->bngd', jax.nn.silu(pre), w2)


def nsa_attention(q, gates, q_pos0, k_cmp, v_cmp, k_sel, v_sel, kv_win):
    b, t_len = q.shape[:2]
    n_cmp = k_cmp.shape[1]
    n_blk = k_sel.shape[1]
    top_n = min(TOP_N, n_blk)
    scale = HEAD_DIM ** -0.5
    slopes = alibi_slopes()[None, :, :, None, None]
    c_pos = jnp.arange(n_cmp) * CMP_STRIDE + CMP_BLOCK - 1
    blk_ids = jnp.arange(n_blk)
    bi = jnp.arange(b)[:, None, None, None]
    gi = jnp.arange(KV_HEADS)[None, :, None, None]

    def attend(q_b, g_b, q0):
        nq = q_b.shape[1]
        t = q0 + jnp.arange(nq)
        dist_c = t[:, None] - c_pos[None, :]
        s = jnp.einsum('btgrd,bngd->bgrtn', q_b, k_cmp).astype(jnp.float32) * scale - slopes * dist_c
        p_cmp = masked_softmax(s, dist_c >= 0)
        o_cmp = jnp.einsum('bgrtn,bngd->btgrd', p_cmp.astype(v_cmp.dtype), v_cmp)
        imp = p_cmp.sum(axis=2)
        pad_hi = ((0, 0), (0, 0), (0, 0), (0, 1))
        pad_lo = ((0, 0), (0, 0), (0, 0), (1, 0))
        imp = 0.5 * (jnp.pad(imp, pad_hi) + jnp.pad(imp, pad_lo))
        imp = imp.reshape(b, KV_HEADS, nq, n_blk, SEL_BLOCK // CMP_STRIDE).sum(-1)
        cur = (t // SEL_BLOCK)[:, None]
        forced = (blk_ids == 0) | (blk_ids == cur) | (blk_ids == cur - 1)
        imp = jnp.where(forced, FORCE_SCORE, imp)
        imp = jnp.where(blk_ids * SEL_BLOCK <= t[:, None], imp, -jnp.inf)
        _, idx = lax.top_k(imp, top_n)
        k_g = k_sel[bi, idx, :, gi, :].reshape(b, KV_HEADS, nq, top_n * SEL_BLOCK, HEAD_DIM)
        v_g = v_sel[bi, idx, :, gi, :].reshape(b, KV_HEADS, nq, top_n * SEL_BLOCK, HEAD_DIM)
        k_pos = (idx[..., None] * SEL_BLOCK + jnp.arange(SEL_BLOCK)).reshape(b, KV_HEADS, nq, top_n * SEL_BLOCK)
        dist_s = (t[:, None] - k_pos)[:, :, None]
        s = jnp.einsum('btgrd,bgtkd->bgrtk', q_b, k_g).astype(jnp.float32) * scale - slopes * dist_s
        p = masked_softmax(s, dist_s >= 0)
        o_sel = jnp.einsum('bgrtk,bgtkd->btgrd', p.astype(v_g.dtype), v_g)
        kw = lax.dynamic_slice_in_dim(kv_win, q0 - q_pos0, WINDOW + nq, axis=1)
        w_pos = q0 - WINDOW + jnp.arange(WINDOW + nq)
        dist_w = t[:, None] - w_pos[None, :]
        s = jnp.einsum('btgrd,bkgd->bgrtk', q_b, kw[:, :, 0]).astype(jnp.float32) * scale - slopes * dist_w
        p = masked_softmax(s, (dist_w >= 0) & (dist_w < WINDOW) & (w_pos >= 0)[None, :])
        o_win = jnp.einsum('bgrtk,bkgd->btgrd', p.astype(kw.dtype), kw[:, :, 1])
        return g_b[..., 0:1] * o_cmp + g_b[..., 1:2] * o_sel + g_b[..., 2:3] * o_win

    if t_len <= Q_BLOCK or t_len % Q_BLOCK:
        out = attend(q, gates, q_pos0)
    else:
        nb = t_len // Q_BLOCK
        q_blocks = q.reshape((b, nb, Q_BLOCK) + q.shape[2:]).swapaxes(0, 1)
        g_blocks = gates.reshape((b, nb, Q_BLOCK) + gates.shape[2:]).swapaxes(0, 1)
        starts = q_pos0 + Q_BLOCK * jnp.arange(nb)
        out = lax.map(lambda xs: attend(xs[0], xs[1], xs[2]), (q_blocks, g_blocks, starts))
        out = out.swapaxes(0, 1)
    return out.reshape(b, t_len, N_Q)


def causal_depthwise_conv(u_full, w):
    return lax.conv_general_dilated(u_full, w[:, None, :].astype(u_full.dtype), (1,), 'VALID',
                                    dimension_numbers=('NWC', 'WIO', 'NWC'),
                                    feature_group_count=u_full.shape[-1])


def routed_experts(xf, e_idx, e_w, w1, w3, w2):
    n_tok, d = xf.shape
    n_asg = n_tok * TOP_K
    blk = max(1, min(MOE_BLOCK, n_asg // N_EXPERTS))
    flat_e = e_idx.reshape(-1)
    order = jnp.argsort(flat_e)
    sorted_e = flat_e[order]
    counts = jnp.bincount(flat_e, length=N_EXPERTS)
    padded = (counts + blk - 1) // blk * blk
    pad_end = jnp.cumsum(padded)
    start = jnp.cumsum(counts) - counts
    dest = (pad_end - padded)[sorted_e] + jnp.arange(n_asg) - start[sorted_e]
    n_rows = -(-(n_asg + N_EXPERTS * (blk - 1)) // blk) * blk
    n_blk = n_rows // blk
    row_tok = jnp.full((n_rows,), n_tok, jnp.int32).at[dest].set((order // TOP_K).astype(jnp.int32))
    row_w = jnp.zeros((n_rows,), e_w.dtype).at[dest].set(e_w.reshape(-1)[order])
    blk_e = jnp.minimum(jnp.searchsorted(pad_end, jnp.arange(n_blk) * blk, side='right'), N_EXPERTS - 1)
    x_pad = jnp.concatenate([xf, jnp.zeros((1, d), xf.dtype)], axis=0)

    def expert_rows(args):
        tok, wgt, e = args
        return swiglu(x_pad[tok], w1[e], w3[e], w2[e]) * wgt[:, None]

    y_rows = lax.map(expert_rows, (row_tok.reshape(n_blk, blk), row_w.reshape(n_blk, blk), blk_e))
    y_rows = y_rows.reshape(n_rows, d)
    return jnp.zeros((n_tok + 1, d), y_rows.dtype).at[row_tok].add(y_rows)[:n_tok]


def moe_ffn(h, router_w, router_b, exp_w1, exp_w3, exp_w2, sh_w1, sh_w3, sh_w2):
    b, t, d = h.shape
    hf = h.reshape(b * t, d)
    n = hf.shape[0]
    aff = jax.nn.sigmoid((hf @ router_w).astype(jnp.float32))
    biased = aff + router_b.astype(jnp.float32)
    grp = biased.reshape(n, N_GROUPS, N_EXPERTS // N_GROUPS)
    g_score = lax.top_k(grp, 2)[0].sum(-1)
    _, g_idx = lax.top_k(g_score, TOPK_GROUPS)
    g_keep = jax.nn.one_hot(g_idx, N_GROUPS, dtype=jnp.float32).sum(1) > 0
    biased = jnp.where(jnp.repeat(g_keep, N_EXPERTS // N_GROUPS, axis=1), biased, -jnp.inf)
    _, e_idx = lax.top_k(biased, TOP_K)
    e_w = jnp.take_along_axis(aff, e_idx, axis=1)
    e_w = e_w / e_w.sum(-1, keepdims=True) * ROUTED_SCALE
    y = routed_experts(hf, e_idx, e_w.astype(hf.dtype), exp_w1, exp_w3, exp_w2) + swiglu(hf, sh_w1, sh_w3, sh_w2)
    return y.reshape(b, t, d)


def decoder_layer(x, c, past_kv, past_win, conv_hist, q_pos0,
                  w_ada, b_ada, g_norm1, w_in, cmp_pe, cmp_w1, cmp_w2,
                  conv_w, conv_b, conv_ln_g, conv_ln_b, w_a, w_b, w_out,
                  g_norm2, router_w, router_b, exp_w1, exp_w3, exp_w2,
                  sh_w1, sh_w3, sh_w2):
    b, t, _ = x.shape
    mod = (jax.nn.silu(c) @ w_ada + b_ada)[:, None, :]
    sh1, sc1, gt1, sh2, sc2, gt2 = jnp.split(mod, 6, axis=-1)
    h = rms_norm(x, g_norm1) * (1 + sc1) + sh1
    q, kv, g_nsa, u, g_mrg = jnp.split(h @ w_in, SPLITS, axis=-1)
    q = q.reshape(b, t, KV_HEADS, Q_PER_KV, HEAD_DIM)
    kv = kv.reshape(b, t, 6, KV_HEADS, HEAD_DIM)
    g_nsa = jax.nn.sigmoid(g_nsa).reshape(b, t, KV_HEADS, Q_PER_KV, 3)
    g_mrg = jax.nn.sigmoid(g_mrg)
    kv_new = kv[:, :, :4]
    n_key = past_kv.shape[1] + t
    n_key_pad = -(-n_key // SEL_BLOCK) * SEL_BLOCK
    kv_all = jnp.concatenate(
        [past_kv, kv_new, jnp.zeros((b, n_key_pad - n_key, 4, KV_HEADS, HEAD_DIM), kv.dtype)], axis=1)
    k_cmp = compress(kv_all[:, :, 0], cmp_pe[0], cmp_w1[0], cmp_w2[0])
    v_cmp = compress(kv_all[:, :, 1], cmp_pe[1], cmp_w1[1], cmp_w2[1])
    k_sel = kv_all[:, :, 2].reshape(b, n_key_pad // SEL_BLOCK, SEL_BLOCK, KV_HEADS, HEAD_DIM)
    v_sel = kv_all[:, :, 3].reshape(b, n_key_pad // SEL_BLOCK, SEL_BLOCK, KV_HEADS, HEAD_DIM)
    win_full = jnp.concatenate([past_win, kv[:, :, 4:]], axis=1)
    o_attn = nsa_attention(q, g_nsa, q_pos0, k_cmp, v_cmp, k_sel, v_sel, win_full)
    u_a, u_g = jnp.split(u, 2, axis=-1)
    conv_full = jnp.concatenate([conv_hist, u_a * jax.nn.sigmoid(u_g)], axis=1)
    z = causal_depthwise_conv(conv_full, conv_w) + conv_b
    z = jax.nn.silu(layer_norm(z, conv_ln_g, conv_ln_b))
    mixed = g_mrg[..., :D_MODEL] * (o_attn @ w_a) + g_mrg[..., D_MODEL:] * (z @ w_b)
    x = x + gt1 * (mixed @ w_out)
    h2 = rms_norm(x, g_norm2) * (1 + sc2) + sh2
    x = x + gt2 * moe_ffn(h2, router_w, router_b, exp_w1, exp_w3, exp_w2, sh_w1, sh_w3, sh_w2)
    return x, kv_new, win_full, conv_full


def setup_inputs(seed: int = 0) -> dict:
    key = jax.random.key(seed)
    k = jax.random.split(key, 32)
    f32 = jnp.float32
    n_pages = PAST_LEN // PAGE_SIZE
    n_used = DEC_BATCH * n_pages
    n_pool = n_used + max(1, n_used // 4)
    w_buf = min(WINDOW, PAST_LEN)
    L = DEPTH

    def nrm(i, shape, scale):
        return jax.random.normal(k[i], shape, f32) * scale

    page_table = jax.random.permutation(k[7], n_pool)[:n_used].reshape(DEC_BATCH, n_pages).astype(jnp.int32)
    return {
        'x_prompt': nrm(0, (BATCH, SEQ, D_MODEL), 1.0),
        'x_sample': nrm(1, (DEC_BATCH, DEC_SEQ, D_MODEL), 1.0),
        'cache_kv': nrm(2, (L, n_pool, PAGE_SIZE, 4, KV_HEADS, HEAD_DIM), 1.0),
        'cache_win': nrm(3, (L, DEC_BATCH, w_buf, 2, KV_HEADS, HEAD_DIM), 1.0),
        'state_conv': nrm(4, (L, DEC_BATCH, CONV_W - 1, C_CONV), 0.5),
        'page_table': page_table,
        'c_prompt': nrm(5, (BATCH, D_MODEL), 1.0),
        'c_sample': nrm(6, (DEC_BATCH, D_MODEL), 1.0),
        'w_ada': nrm(8, (L, D_MODEL, 6 * D_MODEL), 0.5 * D_MODEL ** -0.5),
        'b_ada': nrm(9, (L, 6 * D_MODEL), 0.02),
        'g_norm1': 1.0 + nrm(10, (L, D_MODEL), 0.02),
        'w_in': nrm(11, (L, D_MODEL, N_IN), D_MODEL ** -0.5),
        'cmp_pe': nrm(12, (L, 2, CMP_BLOCK, HEAD_DIM), 0.1),
        'cmp_w1': nrm(13, (L, 2, CMP_BLOCK, HEAD_DIM, CMP_HID), (CMP_BLOCK * HEAD_DIM) ** -0.5),
        'cmp_w2': nrm(14, (L, 2, CMP_HID, HEAD_DIM), CMP_HID ** -0.5),
        'conv_w': nrm(15, (L, CONV_W, C_CONV), CONV_W ** -0.5),
        'conv_b': nrm(16, (L, C_CONV), 0.02),
        'conv_ln_g': 1.0 + nrm(17, (L, C_CONV), 0.02),
        'conv_ln_b': nrm(18, (L, C_CONV), 0.02),
        'w_a': nrm(19, (L, N_Q, D_MODEL), N_Q ** -0.5),
        'w_b': nrm(20, (L, C_CONV, D_MODEL), C_CONV ** -0.5),
        'w_out': nrm(21, (L, D_MODEL, D_MODEL), D_MODEL ** -0.5),
        'g_norm2': 1.0 + nrm(22, (L, D_MODEL), 0.02),
        'router_w': nrm(23, (L, D_MODEL, N_EXPERTS), D_MODEL ** -0.5),
        'router_b': nrm(24, (L, N_EXPERTS), 0.01),
        'exp_w1': nrm(25, (L, N_EXPERTS, D_MODEL, D_EXPERT), D_MODEL ** -0.5),
        'exp_w3': nrm(26, (L, N_EXPERTS, D_MODEL, D_EXPERT), D_MODEL ** -0.5),
        'exp_w2': nrm(27, (L, N_EXPERTS, D_EXPERT, D_MODEL), D_EXPERT ** -0.5),
        'sh_w1': nrm(28, (L, D_MODEL, D_EXPERT), D_MODEL ** -0.5),
        'sh_w3': nrm(29, (L, D_MODEL, D_EXPERT), D_MODEL ** -0.5),
        'sh_w2': nrm(30, (L, D_EXPERT, D_MODEL), D_EXPERT ** -0.5),
        'g_final': 1.0 + nrm(31, (D_MODEL,), 0.02),
    }


def reference(x_prompt, x_sample, cache_kv, cache_win, state_conv, page_table, c_prompt, c_sample,
              w_ada, b_ada, g_norm1, w_in, cmp_pe, cmp_w1, cmp_w2, conv_w, conv_b, conv_ln_g, conv_ln_b,
              w_a, w_b, w_out, g_norm2, router_w, router_b, exp_w1, exp_w3, exp_w2,
              sh_w1, sh_w3, sh_w2, g_final):
    bp, sp = x_prompt.shape[:2]
    bs = x_sample.shape[0]
    n_pages = page_table.shape[1]
    w_buf = cache_win.shape[2]
    xp, xs = x_prompt, x_sample
    kv_p, kv_s, win_p, win_s, conv_p, conv_s = [], [], [], [], [], []
    for l in range(DEPTH):
        lw = (w_ada[l], b_ada[l], g_norm1[l], w_in[l], cmp_pe[l], cmp_w1[l], cmp_w2[l],
              conv_w[l], conv_b[l], conv_ln_g[l], conv_ln_b[l], w_a[l], w_b[l], w_out[l],
              g_norm2[l], router_w[l], router_b[l], exp_w1[l], exp_w3[l], exp_w2[l],
              sh_w1[l], sh_w3[l], sh_w2[l])
        xp, kvn, winf, convf = decoder_layer(
            xp, c_prompt,
            jnp.zeros((bp, 0, 4, KV_HEADS, HEAD_DIM), xp.dtype),
            jnp.zeros((bp, WINDOW, 2, KV_HEADS, HEAD_DIM), xp.dtype),
            jnp.zeros((bp, CONV_W - 1, C_CONV), xp.dtype),
            0, *lw)
        kv_p.append(kvn)
        win_p.append(winf[:, -min(WINDOW, sp):])
        conv_p.append(convf[:, -(CONV_W - 1):])
        past = cache_kv[l][page_table].reshape(bs, n_pages * PAGE_SIZE, 4, KV_HEADS, HEAD_DIM)
        past_win = jnp.pad(cache_win[l], ((0, 0), (WINDOW - w_buf, 0), (0, 0), (0, 0), (0, 0)))
        xs, kvn, winf, convf = decoder_layer(xs, c_sample, past, past_win, state_conv[l], PAST_LEN, *lw)
        kv_s.append(kvn)
        win_s.append(winf[:, -w_buf:])
        conv_s.append(convf[:, -(CONV_W - 1):])
    y_prompt = rms_norm(xp, g_final)
    y_sample = rms_norm(xs, g_final)
    return (y_prompt, y_sample, jnp.stack(kv_p), jnp.stack(kv_s), jnp.stack(win_p), jnp.stack(win_s),
            jnp.stack(conv_p), jnp.stack(conv_s))
```

```python
import functools

import jax
import jax.numpy as jnp
from jax import lax
from jax.experimental import pallas as pl
from jax.experimental.pallas import tpu as pltpu

F32 = jnp.float32
BF16 = jnp.bfloat16
I32 = jnp.int32
MXU_DTYPE = jnp.bfloat16

N_HEADS = 8
KV_HEADS = 2
Q_PER_KV = N_HEADS // KV_HEADS
HEAD_DIM = 64
CMP_BLOCK = 32
CMP_STRIDE = 16
CMP_HID = 2 * HEAD_DIM
SEL_BLOCK = 64
TOP_N = 16
WINDOW = 512
CONV_W = 31
N_EXPERTS = 256
N_GROUPS = 8
TOPK_GROUPS = 4
TOP_K = 8
ROUTED_SCALE = 2.5
EPS = 1e-6
NEG_INF = -1e30
FORCE_SCORE = 1e4
ATTN_SCALE = HEAD_DIM ** -0.5

LANES = 128
SUBLANES = 8
VMEM_LIMIT = 56 * 1024 * 1024


def _cparams(n_axes):
    return pltpu.CompilerParams(
        dimension_semantics=("arbitrary",) * n_axes, vmem_limit_bytes=VMEM_LIMIT)


def _mm(a, b):
    return jnp.dot(a.astype(MXU_DTYPE), b.astype(MXU_DTYPE), preferred_element_type=F32)


def _mm_nt(a, b):
    return lax.dot_general(a.astype(MXU_DTYPE), b.astype(MXU_DTYPE),
                           (((1,), (1,)), ((), ())), preferred_element_type=F32)


def _mm_tn(a, b):
    return lax.dot_general(a.astype(MXU_DTYPE), b.astype(MXU_DTYPE),
                           (((0,), (0,)), ((), ())), preferred_element_type=F32)


def _sigmoid(x):
    return 1.0 / (1.0 + jnp.exp(-x))


def _silu(x):
    return x * _sigmoid(x)


def _rms(x, g):
    return x * lax.rsqrt(jnp.mean(x * x, axis=-1, keepdims=True) + EPS) * g


def _slope(head):
    return 2.0 ** (-8.0 * (head + 1) / N_HEADS)


def _ada_kernel(c_ref, w_ref, b_ref, o_ref):
    o_ref[...] = _mm(_silu(c_ref[...]), w_ref[...]) + b_ref[...]


def _ada_call(c, w_ada, b_ada):
    n, d = c.shape
    n_out = w_ada.shape[1]
    tn = n_out // 6
    return pl.pallas_call(
        _ada_kernel,
        grid=(n_out // tn,),
        in_specs=[pl.BlockSpec((n, d), lambda j: (0, 0)),
                  pl.BlockSpec((d, tn), lambda j: (0, j)),
                  pl.BlockSpec((1, tn), lambda j: (0, j))],
        out_specs=pl.BlockSpec((n, tn), lambda j: (0, j)),
        out_shape=jax.ShapeDtypeStruct((n, n_out), F32),
        compiler_params=_cparams(1),
        name="ada_mod",
    )(c, w_ada, b_ada.reshape(1, n_out))


def _inproj_kernel(x_ref, sc_ref, sh_ref, g_ref, w_ref, q_ref, kv_ref, glu_ref, gate_ref,
                   *, n_q, n_kv, c_conv):
    x = x_ref[0]
    h = _rms(x, g_ref[...]) * (1.0 + sc_ref[0]) + sh_ref[0]
    y = _mm(h, w_ref[...])
    o = 0
    q_ref[0] = y[:, o:o + n_q].astype(q_ref.dtype)
    o += n_q
    kv_ref[0] = y[:, o:o + n_kv]
    o += n_kv
    u_a = y[:, o:o + c_conv]
    u_g = y[:, o + c_conv:o + 2 * c_conv]
    glu_ref[0] = u_a * _sigmoid(u_g)
    o += 2 * c_conv
    gate_ref[0] = _sigmoid(y[:, o:o + LANES])


def _mod_spec(mod, tm, d):
    if mod.shape[1] == 1:
        return pl.BlockSpec((1, 1, d), lambda b, i: (b, 0, 0))
    return pl.BlockSpec((1, tm, d), lambda b, i: (b, i, 0))


def _inproj_call(x, sc1, sh1, g1, w_cat, tm):
    bx, tx, d = x.shape
    n_q = N_HEADS * HEAD_DIM
    n_kv = 6 * KV_HEADS * HEAD_DIM
    c_conv = d // 2
    n_cat = w_cat.shape[1]
    kern = functools.partial(_inproj_kernel, n_q=n_q, n_kv=n_kv, c_conv=c_conv)
    row = lambda w: pl.BlockSpec((1, tm, w), lambda b, i: (b, i, 0))
    return pl.pallas_call(
        kern,
        grid=(bx, tx // tm),
        in_specs=[row(d), _mod_spec(sc1, tm, d), _mod_spec(sh1, tm, d),
                  pl.BlockSpec((1, d), lambda b, i: (0, 0)),
                  pl.BlockSpec((d, n_cat), lambda b, i: (0, 0))],
        out_specs=[row(n_q), row(n_kv), row(c_conv), row(LANES)],
        out_shape=[jax.ShapeDtypeStruct((bx, tx, n_q), MXU_DTYPE),
                   jax.ShapeDtypeStruct((bx, tx, n_kv), F32),
                   jax.ShapeDtypeStruct((bx, tx, c_conv), F32),
                   jax.ShapeDtypeStruct((bx, tx, LANES), F32)],
        compiler_params=_cparams(2),
        name="in_proj",
    )(x, sc1, sh1, g1.reshape(1, d), w_cat)


def _cmp_first_layer(k_rows, v_rows, w1bd_ref, n_chunks):
    acc = None
    for l in range(CMP_STRIDE):
        rows_l = pl.ds(l, n_chunks, stride=CMP_STRIDE)
        x_l = jnp.concatenate([k_rows[rows_l, :], v_rows[rows_l, :]], axis=1)
        part = _mm(x_l, w1bd_ref[l])
        acc = part if acc is None else acc + part
    return acc


def _cmp_second_layer(first, second_next, pe_ref, w1pe_ref, w2bd_ref):
    pe_term = _mm(pe_ref[...], w1pe_ref[...])[0:1]
    return _mm(_silu(first + second_next + pe_term), w2bd_ref[...])


def _compress_prompt_kernel(k_ref, v_ref, w1bd_ref, pe_ref, w1pe_ref, w2bd_ref, o_ref, fs_ref,
                            *, n_chunks):
    half = 4 * CMP_HID
    fs_ref[pl.ds(0, n_chunks), :] = _cmp_first_layer(k_ref.at[0], v_ref.at[0], w1bd_ref, n_chunks)
    fs_ref[pl.ds(n_chunks, SUBLANES), :] = jnp.zeros((SUBLANES, 2 * half), F32)
    first = fs_ref[pl.ds(0, n_chunks), pl.ds(0, half)]
    second_next = fs_ref[pl.ds(1, n_chunks), pl.ds(half, half)]
    o_ref[0] = _cmp_second_layer(first, second_next, pe_ref, w1pe_ref, w2bd_ref)


def _compress_prompt_call(kvw, cw):
    b, t, _ = kvw.shape
    n_chunks = t // CMP_STRIDE
    wid = 4 * HEAD_DIM
    kern = functools.partial(_compress_prompt_kernel, n_chunks=n_chunks)
    full = lambda a: pl.BlockSpec(a.shape, lambda i: (0,) * a.ndim)
    return pl.pallas_call(
        kern,
        grid=(b,),
        in_specs=[pl.BlockSpec((1, t, LANES), lambda i: (i, 0, 0)),
                  pl.BlockSpec((1, t, LANES), lambda i: (i, 0, 1)),
                  full(cw["w1bd"]), full(cw["pe"]), full(cw["w1pe"]), full(cw["w2bd"])],
        out_specs=pl.BlockSpec((1, n_chunks, wid), lambda i: (i, 0, 0)),
        out_shape=jax.ShapeDtypeStruct((b, n_chunks, wid), F32),
        scratch_shapes=[pltpu.VMEM((n_chunks + SUBLANES, 8 * CMP_HID), F32)],
        compiler_params=_cparams(1),
        name="compress_prompt",
    )(kvw, kvw, cw["w1bd"], cw["pe"], cw["w1pe"], cw["w2bd"])


def _compress_weights(cmp_pe, cmp_w1, cmp_w2):
    hd, hid = HEAD_DIM, CMP_HID
    slab_kv = (0, 0, 1, 1)
    w1bd = jnp.zeros((CMP_STRIDE, 4 * hd, 8 * hid), F32)
    w2bd = jnp.zeros((4 * hid, 4 * hd), F32)
    for s, kv in enumerate(slab_kv):
        w1bd = w1bd.at[:, s * hd:(s + 1) * hd, s * hid:(s + 1) * hid].set(cmp_w1[kv, :CMP_STRIDE])
        w1bd = w1bd.at[:, s * hd:(s + 1) * hd, (4 + s) * hid:(5 + s) * hid].set(cmp_w1[kv, CMP_STRIDE:])
        w2bd = w2bd.at[s * hid:(s + 1) * hid, s * hd:(s + 1) * hd].set(cmp_w2[kv])
    pe = jnp.broadcast_to(cmp_pe.reshape(1, -1), (SUBLANES, 2 * CMP_BLOCK * hd))
    w1pe = jnp.zeros((2 * CMP_BLOCK * hd, 4 * hid), F32)
    n_flat = CMP_BLOCK * hd
    for s, kv in enumerate(slab_kv):
        w1pe = w1pe.at[kv * n_flat:(kv + 1) * n_flat, s * hid:(s + 1) * hid].set(
            cmp_w1[kv].reshape(n_flat, hid))
    return {"w1bd": w1bd.astype(MXU_DTYPE), "w2bd": w2bd.astype(MXU_DTYPE),
            "pe": pe, "w1pe": w1pe.astype(MXU_DTYPE)}


def _softmax_cols(s, mask):
    m = jnp.max(jnp.where(mask, s, NEG_INF), axis=0, keepdims=True)
    p = jnp.where(mask, jnp.exp(s - m), 0.0)
    l = jnp.sum(p, axis=0, keepdims=True)
    return p / jnp.where(l > 0.0, l, 1.0)


def _top_n_rows(imp, top_n):
    n = imp.shape[0]
    row_id = lax.broadcasted_iota(I32, imp.shape, 0)
    beaten = jnp.zeros(imp.shape, F32)
    for m in range(n):
        other = imp[m:m + 1, :]
        wins = (other > imp) | ((other == imp) & (row_id > m))
        beaten = beaten + jnp.where(wins, 1.0, 0.0)
    return beaten < float(top_n)


def _attn_prompt_kernel(q_ref, ksv_ref, kwv_ref, kc_ref, gate_ref, o_ref,
                        ksel, vsel, kwin, vwin, psum_ref, *, tq, kc_len, t_len):
    qt = pl.program_id(1)
    q0 = qt * tq
    hd = HEAD_DIM
    n_cmp_rows = kc_ref.shape[1]
    n_blk = t_len // SEL_BLOCK
    top_n = min(TOP_N, n_blk)
    wk = min(WINDOW + tq, t_len)

    @pl.when(qt == 0)
    def _():
        for g in range(KV_HEADS):
            ksel[g] = ksv_ref[0, :, g * hd:(g + 1) * hd].astype(ksel.dtype)
            vsel[g] = ksv_ref[0, :, (2 + g) * hd:(3 + g) * hd].astype(vsel.dtype)
            kwin[g] = kwv_ref[0, :, g * hd:(g + 1) * hd].astype(kwin.dtype)
            vwin[g] = kwv_ref[0, :, (2 + g) * hd:(3 + g) * hd].astype(vwin.dtype)
        psum_ref[...] = jnp.zeros(psum_ref.shape, F32)

    q_blk = q_ref[0]
    gates = gate_ref[0]
    rows = Q_PER_KV * tq
    row_t = q0 + lax.rem(lax.broadcasted_iota(I32, (rows, 1), 0), tq)
    row_head = lax.div(lax.broadcasted_iota(I32, (rows, 1), 0), tq)
    out_heads = []
    for g in range(KV_HEADS):
        q_heads = [q_blk[:, (g * Q_PER_KV + r) * hd:(g * Q_PER_KV + r + 1) * hd]
                   for r in range(Q_PER_KV)]
        k_cmp = kc_ref[0, :, g * hd:(g + 1) * hd]
        v_cmp = kc_ref[0, :, (2 + g) * hd:(3 + g) * hd]
        t_lane = q0 + lax.broadcasted_iota(I32, (n_cmp_rows, tq), 1)
        c_pos = lax.broadcasted_iota(I32, (n_cmp_rows, tq), 0) * CMP_STRIDE + (CMP_BLOCK - 1)
        dist_c = t_lane - c_pos
        mask_c = dist_c >= 0
        dist_cf = dist_c.astype(F32)
        o_cmp = []
        p_sum = None
        for r in range(Q_PER_KV):
            s = _mm_nt(k_cmp, q_heads[r]) * ATTN_SCALE - _slope(g * Q_PER_KV + r) * dist_cf
            p = _softmax_cols(s, mask_c)
            o_cmp.append(_mm_tn(p, v_cmp))
            p_sum = p if p_sum is None else p_sum + p
        psum_ref[pl.ds(SUBLANES, n_cmp_rows), :] = p_sum
        per = SEL_BLOCK // CMP_STRIDE
        taps = [psum_ref[pl.ds(SUBLANES - 1 + k, n_blk, stride=per), :] for k in range(per + 1)]
        imp = 0.5 * taps[0] + 0.5 * taps[per]
        for k in range(1, per):
            imp = imp + taps[k]
        blk = lax.broadcasted_iota(I32, (n_blk, tq), 0)
        t_blk = q0 + lax.broadcasted_iota(I32, (n_blk, tq), 1)
        cur = lax.div(t_blk, SEL_BLOCK)
        forced = (blk == 0) | (blk == cur) | (blk == cur - 1)
        imp = jnp.where(forced, FORCE_SCORE, imp)
        imp = jnp.where(blk * SEL_BLOCK <= t_blk, imp, -jnp.inf)
        sel_t = jnp.where(_top_n_rows(imp, top_n), 1.0, 0.0)
        q_stack = jnp.concatenate(q_heads, axis=0)
        slope_rows = jnp.zeros((rows, 1), F32)
        for r in range(Q_PER_KV):
            slope_rows = jnp.where(row_head == r, _slope(g * Q_PER_KV + r), slope_rows)
        blocks_per_chunk = kc_len // SEL_BLOCK

        def sel_chunk(j, carry):
            m_run, l_run, acc = carry
            k0 = pl.multiple_of(j * kc_len, kc_len)
            k_c = ksel[g, pl.ds(k0, kc_len), :]
            v_c = vsel[g, pl.ds(k0, kc_len), :]
            key_blk = j * blocks_per_chunk + lax.div(
                lax.broadcasted_iota(I32, (n_blk, kc_len), 1), SEL_BLOCK)
            expand = jnp.where(key_blk == lax.broadcasted_iota(I32, (n_blk, kc_len), 0), 1.0, 0.0)
            picked = _mm_tn(sel_t, expand)
            picked = jnp.concatenate([picked] * Q_PER_KV, axis=0)
            k_pos = k0 + lax.broadcasted_iota(I32, (rows, kc_len), 1)
            dist = row_t - k_pos
            valid = (picked > 0.5) & (dist >= 0)
            s = _mm_nt(q_stack, k_c) * ATTN_SCALE - slope_rows * dist.astype(F32)
            m_new = jnp.maximum(m_run, jnp.max(jnp.where(valid, s, NEG_INF), axis=1, keepdims=True))
            alpha = jnp.exp(m_run - m_new)
            p = jnp.where(valid, jnp.exp(s - m_new), 0.0)
            l_new = alpha * l_run + jnp.sum(p, axis=1, keepdims=True)
            acc_new = alpha * acc + _mm(p, v_c)
            return m_new, l_new, acc_new

        n_chunks = lax.div(q0 + tq - 1, kc_len) + 1
        init = (jnp.full((rows, 1), NEG_INF, F32), jnp.zeros((rows, 1), F32),
                jnp.zeros((rows, hd), F32))
        _, l_sel, acc_sel = lax.fori_loop(0, n_chunks, sel_chunk, init)
        o_sel = acc_sel / l_sel
        w0 = pl.multiple_of(jnp.maximum(q0 + tq - wk, 0), tq)
        k_w = kwin[g, pl.ds(w0, wk), :]
        v_w = vwin[g, pl.ds(w0, wk), :]
        dist_w = row_t - (w0 + lax.broadcasted_iota(I32, (rows, wk), 1))
        valid_w = (dist_w >= 0) & (dist_w < WINDOW)
        s = _mm_nt(q_stack, k_w) * ATTN_SCALE - slope_rows * dist_w.astype(F32)
        m_w = jnp.max(jnp.where(valid_w, s, NEG_INF), axis=1, keepdims=True)
        p = jnp.where(valid_w, jnp.exp(s - m_w), 0.0)
        o_win = _mm(p, v_w) / jnp.sum(p, axis=1, keepdims=True)
        for r in range(Q_PER_KV):
            c = (g * Q_PER_KV + r) * 3
            out_heads.append(gates[:, c:c + 1] * o_cmp[r]
                             + gates[:, c + 1:c + 2] * o_sel[r * tq:(r + 1) * tq]
                             + gates[:, c + 2:c + 3] * o_win[r * tq:(r + 1) * tq])
    o_ref[0] = jnp.concatenate(out_heads, axis=1).astype(o_ref.dtype)


def _attn_prompt_call(q, kvw, kcvc, gates, tq, kc_len):
    b, t, n_q = q.shape
    wid = 4 * HEAD_DIM
    kern = functools.partial(_attn_prompt_kernel, tq=tq, kc_len=kc_len, t_len=t)
    kv_scratch = pltpu.VMEM((KV_HEADS, t, HEAD_DIM), MXU_DTYPE)
    n_cmp_rows = kcvc.shape[1]
    return pl.pallas_call(
        kern,
        grid=(b, t // tq),
        in_specs=[pl.BlockSpec((1, tq, n_q), lambda i, j: (i, j, 0)),
                  pl.BlockSpec((1, t, wid), lambda i, j: (i, 0, 1)),
                  pl.BlockSpec((1, t, wid), lambda i, j: (i, 0, 2)),
                  pl.BlockSpec((1, n_cmp_rows, wid), lambda i, j: (i, 0, 0)),
                  pl.BlockSpec((1, tq, LANES), lambda i, j: (i, j, 0))],
        out_specs=pl.BlockSpec((1, tq, n_q), lambda i, j: (i, j, 0)),
        out_shape=jax.ShapeDtypeStruct((b, t, n_q), MXU_DTYPE),
        scratch_shapes=[kv_scratch, kv_scratch, kv_scratch, kv_scratch,
                        pltpu.VMEM((n_cmp_rows + 2 * SUBLANES, tq), F32)],
        compiler_params=_cparams(2),
        name="attn_prompt",
    )(q, kvw, kvw, kcvc, gates)


CONV_PAD = 32


def _ln_silu(z, lg, lb):
    mu = jnp.mean(z, axis=-1, keepdims=True)
    zc = z - mu
    var = jnp.mean(zc * zc, axis=-1, keepdims=True)
    return _silu(zc * lax.rsqrt(var + EPS) * lg + lb)


def _conv_prompt_kernel(glu_ref, cw_ref, cb_ref, lg_ref, lb_ref, z_ref, full_ref, *, t_len, tt):
    full_ref[pl.ds(0, CONV_PAD), :] = jnp.zeros((CONV_PAD, full_ref.shape[1]), F32)
    full_ref[pl.ds(CONV_PAD, t_len), :] = glu_ref[0]
    first = CONV_PAD - (CONV_W - 1)

    def tile(i, carry):
        r0 = pl.multiple_of(i * tt, tt)
        acc = jnp.zeros((tt, full_ref.shape[1]), F32) + cb_ref[...]
        win = full_ref[pl.ds(r0, tt + CONV_PAD), :]
        for phase in range(SUBLANES):
            offs = [o for o in range(first, first + CONV_W) if o % SUBLANES == phase]
            if not offs:
                continue
            shifted = win[phase:max(offs) + tt]
            for o in offs:
                acc = acc + shifted[o - phase:o - phase + tt] * cw_ref[o - first:o - first + 1, :]
        z_ref[0, pl.ds(r0, tt), :] = _ln_silu(acc, lg_ref[...], lb_ref[...]).astype(z_ref.dtype)
        return carry

    lax.fori_loop(0, t_len // tt, tile, 0)


def _conv_prompt_call(glu, conv_w, conv_b, ln_g, ln_b, tt):
    b, t, c = glu.shape
    kern = functools.partial(_conv_prompt_kernel, t_len=t, tt=tt)
    vec = pl.BlockSpec((1, c), lambda i: (0, 0))
    return pl.pallas_call(
        kern,
        grid=(b,),
        in_specs=[pl.BlockSpec((1, t, c), lambda i: (i, 0, 0)),
                  pl.BlockSpec((CONV_PAD, c), lambda i: (0, 0)), vec, vec, vec],
        out_specs=pl.BlockSpec((1, t, c), lambda i: (i, 0, 0)),
        out_shape=jax.ShapeDtypeStruct((b, t, c), MXU_DTYPE),
        scratch_shapes=[pltpu.VMEM((CONV_PAD + t, c), F32)],
        compiler_params=_cparams(1),
        name="conv_prompt",
    )(glu, jnp.pad(conv_w, ((0, CONV_PAD - CONV_W), (0, 0))), conv_b.reshape(1, c),
      ln_g.reshape(1, c), ln_b.reshape(1, c))


def _conv_sample_kernel(full_ref, cw_ref, cb_ref, lg_ref, lb_ref, z_ref):
    z = jnp.sum(full_ref[...] * cw_ref[...][None], axis=1) + cb_ref[...]
    z_ref[...] = _ln_silu(z, lg_ref[...], lb_ref[...]).astype(z_ref.dtype)


def _conv_sample_call(conv_full, conv_w, conv_b, ln_g, ln_b):
    bs, _, c = conv_full.shape
    full = jnp.pad(conv_full, ((0, 0), (0, CONV_PAD - CONV_W), (0, 0)))
    return pl.pallas_call(
        _conv_sample_kernel,
        out_shape=jax.ShapeDtypeStruct((bs, c), MXU_DTYPE),
        name="conv_sample",
    )(full, jnp.pad(conv_w, ((0, CONV_PAD - CONV_W), (0, 0))), conv_b.reshape(1, c),
      ln_g.reshape(1, c), ln_b.reshape(1, c))


def _route_tile(h2, rwt_ref, rb_ref, cnt_ref):
    n_e = N_EXPERTS
    per_grp = n_e // N_GROUPS
    tm = h2.shape[0]
    aff = _sigmoid(_mm_nt(rwt_ref[...], h2))
    biased = aff + rb_ref[...]
    neg = -jnp.inf
    g_rows = []
    for g in range(N_GROUPS):
        v = biased[g * per_grp:(g + 1) * per_grp]
        m1 = jnp.max(v, axis=0, keepdims=True)
        is_m1 = v == m1
        n_m1 = jnp.sum(jnp.where(is_m1, 1.0, 0.0), axis=0, keepdims=True)
        m2 = jnp.max(jnp.where(is_m1, neg, v), axis=0, keepdims=True)
        g_rows.append(m1 + jnp.where(n_m1 >= 2.0, m1, m2))
    g_keep = _top_n_rows(jnp.concatenate(g_rows, axis=0), TOPK_GROUPS)
    cur = jnp.concatenate(
        [jnp.where(g_keep[g:g + 1], biased[g * per_grp:(g + 1) * per_grp], neg)
         for g in range(N_GROUPS)], axis=0)
    row_id = lax.broadcasted_iota(I32, (n_e, tm), 0).astype(F32)
    ids, wts, hots = [], [], []
    for _ in range(TOP_K):
        m = jnp.max(cur, axis=0, keepdims=True)
        idx = jnp.min(jnp.where(cur == m, row_id, float(n_e)), axis=0, keepdims=True)
        hot = row_id == idx
        ids.append(idx)
        wts.append(jnp.sum(jnp.where(hot, aff, 0.0), axis=0, keepdims=True))
        hots.append(hot)
        cur = jnp.where(hot, neg, cur)
    w_sum = wts[0]
    for w in wts[1:]:
        w_sum = w_sum + w
    wts = [w / w_sum * ROUTED_SCALE for w in wts]
    hot_all = jnp.where(hots[0], 1.0, 0.0)
    for hot in hots[1:]:
        hot_all = hot_all + jnp.where(hot, 1.0, 0.0)
    earlier = jnp.where(lax.broadcasted_iota(I32, (tm, tm), 0) < lax.broadcasted_iota(I32, (tm, tm), 1),
                        1.0, 0.0)
    before = cnt_ref[:, 0:1] + jnp.dot(hot_all.astype(BF16), earlier.astype(BF16),
                                       preferred_element_type=F32)
    ranks = [jnp.sum(jnp.where(hot, before, 0.0), axis=0, keepdims=True) for hot in hots]
    cnt_ref[...] = cnt_ref[...] + jnp.sum(hot_all, axis=1, keepdims=True)
    cat = lambda rows: jnp.concatenate(rows, axis=0)
    return cat(ids).astype(I32), cat(wts), cat(ranks).astype(I32)


def _merge_kernel(x_ref, oa_ref, z_ref, sc1_ref, sh1_ref, gt1_ref, sc2_ref, sh2_ref,
                  g1_ref, wm_ref, wa_ref, wb_ref, wo_ref, g2_ref, rwt_ref, rb_ref,
                  s13_ref, s2_ref,
                  x1_ref, h2_ref, ysh_ref, eid_ref, ew_ref, rk_ref, cnt_out_ref, cnt_ref, *, d_exp):
    first_step = (pl.program_id(0) == 0) & (pl.program_id(1) == 0)

    @pl.when(first_step)
    def _():
        cnt_ref[...] = jnp.zeros(cnt_ref.shape, F32)

    x = x_ref[0]
    d = x.shape[1]
    tm = x.shape[0]
    h = _rms(x, g1_ref[...]) * (1.0 + sc1_ref[0]) + sh1_ref[0]
    g_mrg = _sigmoid(_mm(h, wm_ref[...]))
    mixed = g_mrg[:, :d] * _mm(oa_ref[0], wa_ref[...]) + g_mrg[:, d:] * _mm(z_ref[0], wb_ref[...])
    x1 = x + gt1_ref[0] * _mm(mixed, wo_ref[...])
    x1_ref[0] = x1
    h2 = _rms(x1, g2_ref[...]) * (1.0 + sc2_ref[0]) + sh2_ref[0]
    for s in range(d // LANES):
        h2_ref[pl.ds(s, tm, stride=d // LANES), :] = h2[:, s * LANES:(s + 1) * LANES]
    hs = _mm(h2, s13_ref[...])
    ysh_ref[0] = _mm(_silu(hs[:, :d_exp]) * hs[:, d_exp:], s2_ref[...])
    ids, wts, ranks = _route_tile(h2, rwt_ref, rb_ref, cnt_ref)
    eid_ref[...] = ids
    ew_ref[...] = wts
    rk_ref[...] = ranks
    cnt_out_ref[...] = cnt_ref[...]


def _merge_call(x, o_attn, z, mods, mw, tm):
    bx, tx, d = x.shape
    n = bx * tx
    nt = tx // tm
    d_exp = mw["s2"].shape[0]
    kern = functools.partial(_merge_kernel, d_exp=d_exp)
    row = lambda w: pl.BlockSpec((1, tm, w), lambda b, i: (b, i, 0))
    full = lambda a: pl.BlockSpec(a.shape, lambda b, i: (0,) * a.ndim)
    tok = pl.BlockSpec((TOP_K, tm), lambda b, i: (0, b * nt + i))
    wnames = ("g1", "wm", "wa", "wb", "wo", "g2", "rwt", "rb", "s13", "s2")
    return pl.pallas_call(
        kern,
        grid=(bx, nt),
        in_specs=[row(d), row(o_attn.shape[2]), row(z.shape[2])]
                 + [_mod_spec(m, tm, d) for m in mods]
                 + [full(mw[k]) for k in wnames],
        out_specs=[row(d),
                   pl.BlockSpec((tm * (d // LANES), LANES), lambda b, i: (b * nt + i, 0)),
                   row(d), tok, tok, tok,
                   pl.BlockSpec((N_EXPERTS, LANES), lambda b, i: (0, 0))],
        out_shape=[jax.ShapeDtypeStruct((bx, tx, d), F32),
                   jax.ShapeDtypeStruct((n * (d // LANES), LANES), F32),
                   jax.ShapeDtypeStruct((bx, tx, d), F32),
                   jax.ShapeDtypeStruct((TOP_K, n), I32),
                   jax.ShapeDtypeStruct((TOP_K, n), F32),
                   jax.ShapeDtypeStruct((TOP_K, n), I32),
                   jax.ShapeDtypeStruct((N_EXPERTS, LANES), F32)],
        scratch_shapes=[pltpu.VMEM((N_EXPERTS, LANES), F32)],
        compiler_params=_cparams(2),
        name="merge_route",
    )(x, o_attn, z, *mods, *[mw[k] for k in wnames])


def _start_tile_copy(src_hbm, src_row, buf, slot, dst_row, sem, n_sub):
    pltpu.make_async_copy(src_hbm.at[pl.ds(pl.multiple_of(src_row, n_sub), n_sub), :],
                          buf.at[slot, pl.ds(pl.multiple_of(dst_row, n_sub), n_sub), :],
                          sem.at[slot]).start()


def _wait_slot(buf, sem, slot):
    pltpu.make_async_copy(buf.at[slot], buf.at[slot], sem.at[slot]).wait()


def _experts_kernel(blk_e_ref, n_used_ref, src_ref, src_next_ref, h2_hbm, w1_ref, w3_ref, w2_ref,
                    y_ref, buf, sem, *, rb, n_sub):
    i = pl.program_id(0)
    n_used = n_used_ref[0]
    slot = lax.rem(i, 2)

    def gather(rows_ref, to_slot):
        def issue(r, carry):
            _start_tile_copy(h2_hbm, rows_ref[0, 0, r], buf, to_slot, r * n_sub, sem, n_sub)
            return carry
        lax.fori_loop(0, rb, issue, 0, unroll=8)

    @pl.when(i == 0)
    def _():
        gather(src_ref, 0)

    @pl.when(i + 1 < n_used)
    def _():
        gather(src_next_ref, 1 - slot)

    @pl.when(i < n_used)
    def _():
        _wait_slot(buf, sem, slot)
        x = jnp.concatenate([buf[slot, pl.ds(s, rb, stride=n_sub), :] for s in range(n_sub)], axis=1)
        hid = _silu(_mm(x, w1_ref[0])) * _mm(x, w3_ref[0])
        y = _mm(hid, w2_ref[0])
        for s in range(n_sub):
            y_ref[pl.ds(s, rb, stride=n_sub), :] = y[:, s * LANES:(s + 1) * LANES]

    @pl.when(i >= n_used)
    def _():
        y_ref[...] = jnp.zeros(y_ref.shape, F32)


def _experts_call(h2_rows, row_src, blk_e, n_used, w1, w3, w2, rb):
    n_blk = row_src.shape[0]
    _, d, d_exp = w1.shape
    n_sub = d // LANES
    kern = functools.partial(_experts_kernel, rb=rb, n_sub=n_sub)
    smem_rows = lambda f: pl.BlockSpec((1, 1, rb), f, memory_space=pltpu.SMEM)
    grid_spec = pltpu.PrefetchScalarGridSpec(
        num_scalar_prefetch=2,
        grid=(n_blk,),
        in_specs=[smem_rows(lambda i, be, nu: (i, 0, 0)),
                  smem_rows(lambda i, be, nu: (jnp.minimum(i + 1, n_blk - 1), 0, 0)),
                  pl.BlockSpec(memory_space=pl.ANY),
                  pl.BlockSpec((1, d, d_exp), lambda i, be, nu: (be[i], 0, 0)),
                  pl.BlockSpec((1, d, d_exp), lambda i, be, nu: (be[i], 0, 0)),
                  pl.BlockSpec((1, d_exp, d), lambda i, be, nu: (be[i], 0, 0))],
        out_specs=pl.BlockSpec((rb * n_sub, LANES), lambda i, be, nu: (i, 0)),
        scratch_shapes=[pltpu.VMEM((2, rb * n_sub, LANES), F32), pltpu.SemaphoreType.DMA((2,))],
    )
    return pl.pallas_call(
        kern,
        grid_spec=grid_spec,
        out_shape=jax.ShapeDtypeStruct((n_blk * rb * n_sub, LANES), F32),
        compiler_params=_cparams(1),
        name="routed_experts",
    )(blk_e, n_used, row_src, row_src, h2_rows, w1, w3, w2)


def _moe_plan(eid, rank, counts, n_tok, rb, n_sub):
    n_asg = n_tok * TOP_K
    n_blk = -(-(n_asg + N_EXPERTS * (rb - 1)) // rb)
    padded = (counts + rb - 1) // rb * rb
    pad_end = jnp.cumsum(padded)
    dest = (pad_end - padded)[eid] + rank
    tok = jnp.broadcast_to(jnp.arange(n_tok, dtype=I32)[None, :], eid.shape)
    row_src = jnp.full((n_blk * rb,), n_tok * n_sub, I32).at[dest.reshape(-1)].set(
        (tok * n_sub).reshape(-1), unique_indices=True)
    blk_e = jnp.minimum(jnp.searchsorted(pad_end, jnp.arange(n_blk, dtype=I32) * rb, side="right"),
                        N_EXPERTS - 1).astype(I32)
    n_used = (pad_end[-1] // rb).astype(I32).reshape(1)
    return dest, row_src.reshape(n_blk, 1, rb), blk_e, n_used


def _combine_kernel(dest_ref, dest_next_ref, y_hbm, ew_ref, x1_ref, ysh_ref, gt2_ref, gf_ref, o_ref,
                    buf, sem, *, tm, n_sub, nt, n_steps):
    step = pl.program_id(0) * nt + pl.program_id(1)
    slot = lax.rem(step, 2)

    def gather(rows_ref, to_slot):
        def issue(t, carry):
            for k in range(TOP_K):
                _start_tile_copy(y_hbm, rows_ref[t * TOP_K + k], buf, to_slot,
                                 (k * tm + t) * n_sub, sem, n_sub)
            return carry
        lax.fori_loop(0, tm, issue, 0)

    @pl.when(step == 0)
    def _():
        gather(dest_ref, 0)

    @pl.when(step + 1 < n_steps)
    def _():
        gather(dest_next_ref, 1 - slot)

    _wait_slot(buf, sem, slot)
    ew = ew_ref[...]
    cols = []
    for s in range(n_sub):
        acc = None
        for k in range(TOP_K):
            term = ew[:, k:k + 1] * buf[slot, pl.ds(k * tm * n_sub + s, tm, stride=n_sub), :]
            acc = term if acc is None else acc + term
        cols.append(acc)
    y_routed = jnp.concatenate(cols, axis=1)
    x2 = x1_ref[0] + gt2_ref[0] * (y_routed + ysh_ref[0])
    o_ref[0] = _rms(x2, gf_ref[...])


def _combine_call(dest_rows, y_rows, ew, x1, ysh, gt2, g_final, tm):
    bx, tx, d = x1.shape
    nt = tx // tm
    n_sub = d // LANES
    n_steps = bx * nt
    kern = functools.partial(_combine_kernel, tm=tm, n_sub=n_sub, nt=nt, n_steps=n_steps)
    row = pl.BlockSpec((1, tm, d), lambda b, i: (b, i, 0))
    smem = lambda f: pl.BlockSpec((tm * TOP_K,), f, memory_space=pltpu.SMEM)
    return pl.pallas_call(
        kern,
        grid=(bx, nt),
        in_specs=[smem(lambda b, i: (b * nt + i,)),
                  smem(lambda b, i: (jnp.minimum(b * nt + i + 1, n_steps - 1),)),
                  pl.BlockSpec(memory_space=pl.ANY),
                  pl.BlockSpec((tm, TOP_K), lambda b, i: (b * nt + i, 0)),
                  row, row, _mod_spec(gt2, tm, d),
                  pl.BlockSpec((1, d), lambda b, i: (0, 0))],
        out_specs=row,
        out_shape=jax.ShapeDtypeStruct((bx, tx, d), F32),
        scratch_shapes=[pltpu.VMEM((2, TOP_K * tm * n_sub, LANES), F32),
                        pltpu.SemaphoreType.DMA((2,))],
        compiler_params=_cparams(2),
        name="combine_final",
    )(dest_rows, dest_rows, y_rows, ew, x1, ysh, gt2, g_final.reshape(1, d))


def _compress_sample_kernel(pt_ref, cache_hbm, w1bd_ref, fs_ref, kbuf, vbuf, sem,
                            *, pp, page_rows, n_steps):
    step = pl.program_id(0) * pl.num_programs(1) + pl.program_id(1)
    slot = lax.rem(step, 2)

    def fetch(of_step, to_slot):
        def issue(p, carry):
            page = pt_ref[of_step * pp + p]
            dst = pl.ds(pl.multiple_of(p * page_rows, page_rows), page_rows)
            pltpu.make_async_copy(cache_hbm.at[page, :, pl.ds(0, LANES)],
                                  kbuf.at[to_slot, dst, :], sem.at[to_slot]).start()
            pltpu.make_async_copy(cache_hbm.at[page, :, pl.ds(LANES, LANES)],
                                  vbuf.at[to_slot, dst, :], sem.at[to_slot]).start()
            return carry
        lax.fori_loop(0, pp, issue, 0)

    @pl.when(step == 0)
    def _():
        fetch(0, 0)

    @pl.when(step + 1 < n_steps)
    def _():
        fetch(step + 1, 1 - slot)

    _wait_slot(kbuf, sem, slot)
    _wait_slot(vbuf, sem, slot)
    n_chunks = pp * page_rows // CMP_STRIDE
    fs_ref[0] = _cmp_first_layer(kbuf.at[slot], vbuf.at[slot], w1bd_ref, n_chunks)


def _compress_sample_call(page_table, cache3d, w1bd, pp):
    bs, n_pages = page_table.shape
    page_rows = cache3d.shape[1]
    n_chunks = pp * page_rows // CMP_STRIDE
    spb = n_pages // pp
    kern = functools.partial(_compress_sample_kernel, pp=pp, page_rows=page_rows, n_steps=bs * spb)
    buf = pltpu.VMEM((2, pp * page_rows, LANES), F32)
    grid_spec = pltpu.PrefetchScalarGridSpec(
        num_scalar_prefetch=1,
        grid=(bs, spb),
        in_specs=[pl.BlockSpec(memory_space=pl.ANY),
                  pl.BlockSpec(w1bd.shape, lambda b, i, pt: (0, 0, 0))],
        out_specs=pl.BlockSpec((1, n_chunks, 8 * CMP_HID), lambda b, i, pt: (b, i, 0)),
        scratch_shapes=[buf, buf, pltpu.SemaphoreType.DMA((2,))],
    )
    return pl.pallas_call(
        kern,
        grid_spec=grid_spec,
        out_shape=jax.ShapeDtypeStruct((bs, spb * n_chunks, 8 * CMP_HID), F32),
        compiler_params=_cparams(2),
        name="compress_sample",
    )(page_table.reshape(-1), cache3d, w1bd)


def _head_slopes(n_rows):
    head = lax.broadcasted_iota(I32, (n_rows, 1), 0)
    slopes = jnp.zeros((n_rows, 1), F32)
    for h in range(N_HEADS):
        slopes = jnp.where(head == h, _slope(h), slopes)
    return slopes


def _attn_sample_a_kernel(q_ref, fs_ref, pe_ref, w1pe_ref, w2bd_ref, ocmp_ref, idx_ref,
                          *, t_pos, n_blk, n_blk_pad):
    hd = HEAD_DIM
    f = fs_ref[0]
    n_c = f.shape[0]
    half = 4 * CMP_HID
    second_next = pltpu.roll(f[:, half:], n_c - 1, 0)
    kcvc = _cmp_second_layer(f[:, :half], second_next, pe_ref, w1pe_ref, w2bd_ref)
    q = q_ref[0]
    c_idx = lax.broadcasted_iota(I32, (1, n_c), 1)
    dist_c = t_pos - (c_idx * CMP_STRIDE + (CMP_BLOCK - 1))
    mask_c = dist_c >= 0
    slopes = _head_slopes(N_HEADS)
    head_grp = lax.div(lax.broadcasted_iota(I32, (N_HEADS, 1), 0), Q_PER_KV)
    per = SEL_BLOCK // CMP_STRIDE
    c_row = lax.broadcasted_iota(I32, (n_c, n_blk_pad), 0)
    lo = lax.broadcasted_iota(I32, (n_c, n_blk_pad), 1) * per
    spread = jnp.where((c_row >= lo) & (c_row < lo + per - 1), 1.0, 0.0) \
        + jnp.where((c_row == lo - 1) | (c_row == lo + per - 1), 0.5, 0.0)
    blk_lane = lax.broadcasted_iota(I32, (1, n_blk_pad), 1)
    cur = t_pos // SEL_BLOCK
    forced = (blk_lane == 0) | (blk_lane == cur) | (blk_lane == cur - 1)
    in_range = (blk_lane * SEL_BLOCK <= t_pos) & (blk_lane < n_blk)
    n_sq = (n_blk_pad, n_blk_pad)
    sub_id = lax.broadcasted_iota(I32, n_sq, 0)
    lane_id = lax.broadcasted_iota(I32, n_sq, 1)
    top_n = min(TOP_N, n_blk)
    o_cmp = jnp.zeros((N_HEADS, hd), F32)
    idx_rows = []
    for g in range(KV_HEADS):
        k_cmp = kcvc[:, g * hd:(g + 1) * hd]
        v_cmp = kcvc[:, (2 + g) * hd:(3 + g) * hd]
        s = _mm_nt(q, k_cmp) * ATTN_SCALE - slopes * dist_c.astype(F32)
        m = jnp.max(jnp.where(mask_c, s, NEG_INF), axis=1, keepdims=True)
        p = jnp.where(mask_c, jnp.exp(s - m), 0.0)
        l = jnp.sum(p, axis=1, keepdims=True)
        p = p / jnp.where(l > 0.0, l, 1.0)
        o_cmp = jnp.where(head_grp == g, _mm(p, v_cmp), o_cmp)
        p_sum = jnp.sum(jnp.where(head_grp == g, p, 0.0), axis=0, keepdims=True)
        imp = jnp.dot(jnp.broadcast_to(p_sum, (SUBLANES, n_c)), spread,
                      precision=lax.Precision.HIGHEST, preferred_element_type=F32)[0:1]
        imp = jnp.where(forced, FORCE_SCORE, imp)
        imp = jnp.where(in_range, imp, -jnp.inf)
        imp_col = jnp.transpose(jnp.broadcast_to(imp, (LANES, n_blk_pad)))[:, 0:1]
        beats = (imp > imp_col) | ((imp == imp_col) & (lane_id < sub_id))
        rank_col = jnp.sum(jnp.where(beats, 1.0, 0.0), axis=1, keepdims=True)
        sel_col = jnp.where(rank_col < float(top_n), 1.0, 0.0)
        before = jnp.dot(jnp.where(lane_id < sub_id, 1.0, 0.0).astype(BF16),
                         jnp.broadcast_to(sel_col, (n_blk_pad, LANES)).astype(BF16),
                         preferred_element_type=F32)
        slot_lane = lax.broadcasted_iota(I32, (n_blk_pad, LANES), 1).astype(F32)
        blk_sub = lax.broadcasted_iota(I32, (n_blk_pad, LANES), 0).astype(F32)
        hit = (sel_col > 0.5) & (before == slot_lane)
        idx_rows.append(jnp.sum(jnp.where(hit, blk_sub, 0.0), axis=0, keepdims=True))
    ocmp_ref[0] = o_cmp
    pad = jnp.zeros((SUBLANES - KV_HEADS, LANES), F32)
    idx_ref[0] = jnp.concatenate(idx_rows + [pad], axis=0).astype(I32)


def _attn_sample_a_call(q_heads, fs, cw, t_pos, n_blk):
    bs = q_heads.shape[0]
    n_blk_pad = -(-n_blk // LANES) * LANES
    kern = functools.partial(_attn_sample_a_kernel, t_pos=t_pos, n_blk=n_blk, n_blk_pad=n_blk_pad)
    full = lambda a: pl.BlockSpec(a.shape, lambda b: (0,) * a.ndim)
    per_b = lambda a: pl.BlockSpec((1,) + a.shape[1:], lambda b: (b,) + (0,) * (a.ndim - 1))
    return pl.pallas_call(
        kern,
        grid=(bs,),
        in_specs=[per_b(q_heads), per_b(fs), full(cw["pe"]), full(cw["w1pe"]), full(cw["w2bd"])],
        out_specs=[pl.BlockSpec((1, N_HEADS, HEAD_DIM), lambda b: (b, 0, 0)),
                   pl.BlockSpec((1, SUBLANES, LANES), lambda b: (b, 0, 0))],
        out_shape=[jax.ShapeDtypeStruct((bs, N_HEADS, HEAD_DIM), F32),
                   jax.ShapeDtypeStruct((bs, SUBLANES, LANES), I32)],
        compiler_params=_cparams(1),
        name="attn_sample_select",
    )(q_heads, fs, cw["pe"], cw["w1pe"], cw["w2bd"])


def _attn_sample_b_kernel(sel_ref, pt_ref, q_ref, ks_ref, vs_ref, kvn_ref, win_ref, gate_ref, ocmp_ref,
                          o_ref, m_ref, l_ref, acc_ref, out_ref, *, t_pos, top_n, n_past_blk):
    b, g, i = pl.program_id(0), pl.program_id(1), pl.program_id(2)
    hd = HEAD_DIM
    is_g0 = g == 0
    pick = lambda a, c0, c1: jnp.where(is_g0, a[:, c0:c0 + hd], a[:, c1:c1 + hd])
    q = q_ref[0]
    slopes = _head_slopes(N_HEADS)

    @pl.when(i == 0)
    def _():
        m_ref[...] = jnp.full(m_ref.shape, NEG_INF, F32)
        l_ref[...] = jnp.zeros(l_ref.shape, F32)
        acc_ref[...] = jnp.zeros(acc_ref.shape, F32)

    @pl.when((i == 0) & is_g0)
    def _():
        out_ref[...] = jnp.zeros(out_ref.shape, F32)

    blk = sel_ref[(b * KV_HEADS + g) * top_n + i]
    k_blk = pick(ks_ref[0], 0, hd)
    v_blk = pick(vs_ref[0], 0, hd)
    dist = (t_pos - (blk * SEL_BLOCK + lax.broadcasted_iota(I32, (1, SEL_BLOCK), 1))).astype(F32)
    dist = jnp.where(blk < n_past_blk, dist, -1.0)
    valid = dist >= 0.0
    s = _mm_nt(q, k_blk) * ATTN_SCALE - slopes * dist
    m_new = jnp.maximum(m_ref[...], jnp.max(jnp.where(valid, s, NEG_INF), axis=1, keepdims=True))
    alpha = jnp.exp(m_ref[...] - m_new)
    p = jnp.where(valid, jnp.exp(s - m_new), 0.0)
    l_ref[...] = alpha * l_ref[...] + jnp.sum(p, axis=1, keepdims=True)
    acc_ref[...] = alpha * acc_ref[...] + _mm(p, v_blk)
    m_ref[...] = m_new

    @pl.when(i == top_n - 1)
    def _():
        kvn = kvn_ref[0]
        rnd = lambda a: a.astype(MXU_DTYPE).astype(F32)
        qf = q.astype(F32)
        k_new = rnd(pick(kvn, 4 * hd, 5 * hd))
        v_new = rnd(pick(kvn, 6 * hd, 7 * hd))
        s_new = jnp.sum(qf * k_new, axis=1, keepdims=True) * ATTN_SCALE
        m_fin = jnp.maximum(m_ref[...], s_new)
        a_old = jnp.exp(m_ref[...] - m_fin)
        p_new = jnp.exp(s_new - m_fin)
        o_sel = (a_old * acc_ref[...] + p_new * v_new) / (a_old * l_ref[...] + p_new)
        win = win_ref[0]
        w_rows = win.shape[0]
        k_w = pick(win, 0, hd)
        v_w = pick(win, 2 * hd, 3 * hd)
        dist_w = (w_rows - lax.broadcasted_iota(I32, (1, w_rows), 1)).astype(F32)
        valid_w = dist_w < float(WINDOW)
        s_w = _mm_nt(q, k_w) * ATTN_SCALE - slopes * dist_w
        kw_new = rnd(pick(kvn, 8 * hd, 9 * hd))
        vw_new = rnd(pick(kvn, 10 * hd, 11 * hd))
        sw_new = jnp.sum(qf * kw_new, axis=1, keepdims=True) * ATTN_SCALE
        m_w = jnp.maximum(jnp.max(jnp.where(valid_w, s_w, NEG_INF), axis=1, keepdims=True), sw_new)
        p_w = jnp.where(valid_w, jnp.exp(s_w - m_w), 0.0)
        pw_new = jnp.exp(sw_new - m_w)
        o_win = (_mm(p_w, v_w) + pw_new * vw_new) / (jnp.sum(p_w, axis=1, keepdims=True) + pw_new)
        gates = gate_ref[0]
        o = gates[:, 0:1] * ocmp_ref[0] + gates[:, 1:2] * o_sel + gates[:, 2:3] * o_win
        head_grp = lax.div(lax.broadcasted_iota(I32, (N_HEADS, 1), 0), Q_PER_KV)
        out_ref[...] = jnp.where(head_grp == g, o, out_ref[...])

    @pl.when((i == top_n - 1) & (g == KV_HEADS - 1))
    def _():
        o_ref[0] = out_ref[...].astype(o_ref.dtype)


def _attn_sample_b_call(sel_idx, page_table, q_heads, cache3d, kv_new, cache_win2, gates3, o_cmp,
                        t_pos, top_n):
    bs, n_pages = page_table.shape
    page_rows = cache3d.shape[1]
    per_page = page_rows // SEL_BLOCK
    n_past_blk = n_pages * per_page
    kern = functools.partial(_attn_sample_b_kernel, t_pos=t_pos, top_n=top_n, n_past_blk=n_past_blk)

    def blk_map(col):
        def f(b, g, i, sel, pt):
            blk = jnp.minimum(sel[(b * KV_HEADS + g) * top_n + i], n_past_blk - 1)
            return (pt[b * n_pages + blk // per_page], blk % per_page, col)
        return f

    per_b = lambda a: pl.BlockSpec((1,) + a.shape[1:], lambda b, g, i, sel, pt: (b,) + (0,) * (a.ndim - 1))
    small = pltpu.VMEM((N_HEADS, 1), F32)
    wide = pltpu.VMEM((N_HEADS, HEAD_DIM), F32)
    grid_spec = pltpu.PrefetchScalarGridSpec(
        num_scalar_prefetch=2,
        grid=(bs, KV_HEADS, top_n),
        in_specs=[per_b(q_heads),
                  pl.BlockSpec((1, SEL_BLOCK, LANES), blk_map(2)),
                  pl.BlockSpec((1, SEL_BLOCK, LANES), blk_map(3)),
                  per_b(kv_new), per_b(cache_win2), per_b(gates3), per_b(o_cmp)],
        out_specs=pl.BlockSpec((1, N_HEADS, HEAD_DIM), lambda b, g, i, sel, pt: (b, 0, 0)),
        scratch_shapes=[small, small, wide, wide],
    )
    return pl.pallas_call(
        kern,
        grid_spec=grid_spec,
        out_shape=jax.ShapeDtypeStruct((bs, N_HEADS, HEAD_DIM), MXU_DTYPE),
        compiler_params=_cparams(3),
        name="attn_sample_gather",
    )(sel_idx, page_table.reshape(-1), q_heads, cache3d, cache3d, kv_new, cache_win2, gates3, o_cmp)


def _prep_weights(w_in, w_a, w_b, w_out, router_w, router_b, sh_w1, sh_w3, sh_w2, g_norm1, g_norm2):
    d = w_in.shape[0]
    n_q = N_HEADS * HEAD_DIM
    n_kv = 6 * KV_HEADS * HEAD_DIM
    n_gate = 3 * N_HEADS
    o_gate = n_q + n_kv
    o_glu = o_gate + n_gate
    o_mrg = o_glu + d
    c = lambda a: a.astype(MXU_DTYPE)
    w_cat = jnp.concatenate(
        [w_in[:, :o_gate], w_in[:, o_glu:o_mrg],
         jnp.pad(w_in[:, o_gate:o_glu], ((0, 0), (0, LANES - n_gate)))], axis=1)
    mw = {"g1": g_norm1.reshape(1, d), "wm": c(w_in[:, o_mrg:]), "wa": c(w_a), "wb": c(w_b),
          "wo": c(w_out), "g2": g_norm2.reshape(1, d), "rwt": c(router_w.T),
          "rb": router_b.reshape(-1, 1), "s13": c(jnp.concatenate([sh_w1, sh_w3], axis=1)),
          "s2": c(sh_w2)}
    return c(w_cat), mw


def kernel(x_prompt, x_sample, cache_kv, cache_win, state_conv, page_table, c_prompt, c_sample, w_ada, b_ada, g_norm1, w_in, cmp_pe, cmp_w1, cmp_w2, conv_w, conv_b, conv_ln_g, conv_ln_b, w_a, w_b, w_out, g_norm2, router_w, router_b, exp_w1, exp_w3, exp_w2, sh_w1, sh_w3, sh_w2, g_final):
    l = 0
    assert w_ada.shape[0] == 1 and x_sample.shape[1] == 1 and cache_win.shape[2] == WINDOW
    bp, sp, d = x_prompt.shape
    bs = x_sample.shape[0]
    n_pool, page_rows = cache_kv.shape[1], cache_kv.shape[2]
    n_pages = page_table.shape[1]
    past_len = n_pages * page_rows
    hd = HEAD_DIM
    c_conv = d // 2
    n_sub = d // LANES
    n_kv_new = 4 * KV_HEADS * hd
    rb = 256

    mod = _ada_call(jnp.concatenate([c_prompt, c_sample], axis=0), w_ada[l], b_ada[l])
    mods_p = [mod[:bp, None, k * d:(k + 1) * d] for k in range(6)]
    mods_s = [mod[None, bp:, k * d:(k + 1) * d] for k in range(6)]
    w_cat, mw = _prep_weights(w_in[l], w_a[l], w_b[l], w_out[l], router_w[l], router_b[l],
                              sh_w1[l], sh_w3[l], sh_w2[l], g_norm1[l], g_norm2[l])
    cw = _compress_weights(cmp_pe[l], cmp_w1[l], cmp_w2[l])

    sh1, sc1, gt1, sh2, sc2, gt2 = mods_p
    q, kvw, glu, gates = _inproj_call(x_prompt, sc1, sh1, g_norm1[l], w_cat, min(sp, 512))
    kcvc = _compress_prompt_call(kvw, cw)
    o_attn = _attn_prompt_call(q, kvw, kcvc, gates, 128, 512 if sp % 512 == 0 else 256)
    z = _conv_prompt_call(glu, conv_w[l], conv_b[l], conv_ln_g[l], conv_ln_b[l], 64)
    x1_p, h2_p, ysh_p, eid_p, ew_p, rank_p, cnt_p = _merge_call(
        x_prompt, o_attn, z, (sc1, sh1, gt1, sc2, sh2), mw, min(sp, 256))

    sh1s, sc1s, gt1s, sh2s, sc2s, gt2s = mods_s
    xs = x_sample.reshape(1, bs, d)
    q_s, kvw_s, glu_s, gates_s = _inproj_call(xs, sc1s, sh1s, g_norm1[l], w_cat, bs)
    cache3d = cache_kv[l].reshape(n_pool, page_rows, n_kv_new)
    fs = _compress_sample_call(page_table, cache3d, cw["w1bd"], min(n_pages, 64))
    q_heads = q_s.reshape(bs, N_HEADS, hd)
    n_blk = -(-(past_len + 1) // SEL_BLOCK)
    top_n = min(TOP_N, n_blk)
    o_cmp_s, sel = _attn_sample_a_call(q_heads, fs, cw, past_len, n_blk)
    o_s = _attn_sample_b_call(
        sel[:, :KV_HEADS, :top_n].reshape(-1), page_table, q_heads, cache3d,
        kvw_s.reshape(bs, 1, -1), cache_win[l].reshape(bs, WINDOW, 2 * KV_HEADS * hd),
        gates_s[0, :, :3 * N_HEADS].reshape(bs, N_HEADS, 3), o_cmp_s, past_len, top_n)
    glu_new = glu_s.reshape(bs, 1, c_conv)
    z_s = _conv_sample_call(jnp.concatenate([state_conv[l], glu_new], axis=1),
                            conv_w[l], conv_b[l], conv_ln_g[l], conv_ln_b[l])
    x1_s, h2_s, ysh_s, eid_s, ew_s, rank_s, cnt_s = _merge_call(
        xs, o_s.reshape(1, bs, N_HEADS * hd), z_s.reshape(1, bs, c_conv),
        (sc1s, sh1s, gt1s, sc2s, sh2s), mw, bs)

    n_p = bp * sp
    n_tok = n_p + bs
    cnt_p = cnt_p[:, 0].astype(I32)
    cnt_s = cnt_s[:, 0].astype(I32)
    eid = jnp.concatenate([eid_p, eid_s], axis=1)
    rank = jnp.concatenate([rank_p, rank_s + cnt_p[eid_s]], axis=1)
    h2_all = jnp.concatenate([h2_p, h2_s, jnp.zeros((n_sub, LANES), F32)], axis=0)
    dest, row_src, blk_e, n_used = _moe_plan(eid, rank, cnt_p + cnt_s, n_tok, rb, n_sub)
    y_rows = _experts_call(h2_all, row_src, blk_e, n_used, exp_w1[l], exp_w3[l], exp_w2[l], rb)
    dest_rows = dest.T * n_sub
    y_prompt = _combine_call(dest_rows[:n_p].reshape(-1), y_rows, ew_p.T, x1_p, ysh_p, gt2,
                             g_final, min(sp, 128))
    y_sample = _combine_call(dest_rows[n_p:].reshape(-1), y_rows, ew_s.T, x1_s, ysh_s, gt2s,
                             g_final, bs)

    w_keep = min(WINDOW, sp)
    kv_prompt = kvw[:, :, :n_kv_new].reshape(1, bp, sp, 4, KV_HEADS, hd)
    kv_sample = kvw_s[0, :, :n_kv_new].reshape(1, bs, 1, 4, KV_HEADS, hd)
    win_prompt = kvw[:, sp - w_keep:, n_kv_new:].reshape(1, bp, w_keep, 2, KV_HEADS, hd)
    win_new = kvw_s[0, :, n_kv_new:].reshape(bs, 1, 2, KV_HEADS, hd)
    win_sample = jnp.concatenate([cache_win[l][:, 1:], win_new], axis=1)[None]
    conv_prompt = glu[:, sp - (CONV_W - 1):][None]
    conv_sample = jnp.concatenate([state_conv[l][:, 1:], glu_new], axis=1)[None]
    return (y_prompt, y_sample.reshape(bs, 1, d), kv_prompt, kv_sample, win_prompt, win_sample,
            conv_prompt, conv_sample)
```

```python
import functools

import jax
import jax.numpy as jnp
from jax import lax
from jax.experimental import pallas as pl
from jax.experimental.pallas import tpu as pltpu

F32 = jnp.float32
BF16 = jnp.bfloat16
I32 = jnp.int32
MXU_DTYPE = jnp.bfloat16

N_HEADS = 8
KV_HEADS = 2
Q_PER_KV = N_HEADS // KV_HEADS
HEAD_DIM = 64
CMP_BLOCK = 32
CMP_STRIDE = 16
CMP_HID = 2 * HEAD_DIM
SEL_BLOCK = 64
TOP_N = 16
WINDOW = 512
CONV_W = 31
N_EXPERTS = 256
N_GROUPS = 8
TOPK_GROUPS = 4
TOP_K = 8
ROUTED_SCALE = 2.5
EPS = 1e-6
NEG_INF = -1e30
FORCE_SCORE = 1e4
ATTN_SCALE = HEAD_DIM ** -0.5

LANES = 128
SUBLANES = 8
VMEM_LIMIT = 56 * 1024 * 1024


def _cparams(n_axes):
    return pltpu.CompilerParams(
        dimension_semantics=("arbitrary",) * n_axes, vmem_limit_bytes=VMEM_LIMIT)


def _mm(a, b):
    return jnp.dot(a.astype(MXU_DTYPE), b.astype(MXU_DTYPE), preferred_element_type=F32)


def _mm_nt(a, b):
    return lax.dot_general(a.astype(MXU_DTYPE), b.astype(MXU_DTYPE),
                           (((1,), (1,)), ((), ())), preferred_element_type=F32)


def _mm_tn(a, b):
    return lax.dot_general(a.astype(MXU_DTYPE), b.astype(MXU_DTYPE),
                           (((0,), (0,)), ((), ())), preferred_element_type=F32)


def _sigmoid(x):
    return 1.0 / (1.0 + jnp.exp(-x))


def _silu(x):
    return x * _sigmoid(x)


def _rms(x, g):
    return x * lax.rsqrt(jnp.mean(x * x, axis=-1, keepdims=True) + EPS) * g


def _slope(head):
    return 2.0 ** (-8.0 * (head + 1) / N_HEADS)


def _ada_kernel(c_ref, w_ref, b_ref, o_ref):
    o_ref[...] = _mm(_silu(c_ref[...]), w_ref[...]) + b_ref[...]


def _ada_call(c, w_ada, b_ada):
    n, d = c.shape
    n_out = w_ada.shape[1]
    tn = n_out // 6
    return pl.pallas_call(
        _ada_kernel,
        grid=(n_out // tn,),
        in_specs=[pl.BlockSpec((n, d), lambda j: (0, 0)),
                  pl.BlockSpec((d, tn), lambda j: (0, j)),
                  pl.BlockSpec((1, tn), lambda j: (0, j))],
        out_specs=pl.BlockSpec((n, tn), lambda j: (0, j)),
        out_shape=jax.ShapeDtypeStruct((n, n_out), F32),
        compiler_params=_cparams(1),
        name="ada_mod",
    )(c, w_ada, b_ada.reshape(1, n_out))


def _inproj_kernel(x_ref, sc_ref, sh_ref, g_ref, w_ref, q_ref, kv_ref, glu_ref, gate_ref,
                   *, n_q, n_kv, c_conv):
    x = x_ref[0]
    h = _rms(x, g_ref[...]) * (1.0 + sc_ref[0]) + sh_ref[0]
    y = _mm(h, w_ref[...])
    o = 0
    q_ref[0] = y[:, o:o + n_q].astype(q_ref.dtype)
    o += n_q
    kv_ref[0] = y[:, o:o + n_kv]
    o += n_kv
    u_a = y[:, o:o + c_conv]
    u_g = y[:, o + c_conv:o + 2 * c_conv]
    glu_ref[0] = u_a * _sigmoid(u_g)
    o += 2 * c_conv
    gate_ref[0] = _sigmoid(y[:, o:o + LANES])


def _mod_spec(mod, tm, d):
    if mod.shape[1] == 1:
        return pl.BlockSpec((1, 1, d), lambda b, i: (b, 0, 0))
    return pl.BlockSpec((1, tm, d), lambda b, i: (b, i, 0))


def _inproj_call(x, sc1, sh1, g1, w_cat, tm):
    bx, tx, d = x.shape
    n_q = N_HEADS * HEAD_DIM
    n_kv = 6 * KV_HEADS * HEAD_DIM
    c_conv = d // 2
    n_cat = w_cat.shape[1]
    kern = functools.partial(_inproj_kernel, n_q=n_q, n_kv=n_kv, c_conv=c_conv)
    row = lambda w: pl.BlockSpec((1, tm, w), lambda b, i: (b, i, 0))
    return pl.pallas_call(
        kern,
        grid=(bx, tx // tm),
        in_specs=[row(d), _mod_spec(sc1, tm, d), _mod_spec(sh1, tm, d),
                  pl.BlockSpec((1, d), lambda b, i: (0, 0)),
                  pl.BlockSpec((d, n_cat), lambda b, i: (0, 0))],
        out_specs=[row(n_q), row(n_kv), row(c_conv), row(LANES)],
        out_shape=[jax.ShapeDtypeStruct((bx, tx, n_q), MXU_DTYPE),
                   jax.ShapeDtypeStruct((bx, tx, n_kv), F32),
                   jax.ShapeDtypeStruct((bx, tx, c_conv), F32),
                   jax.ShapeDtypeStruct((bx, tx, LANES), F32)],
        compiler_params=_cparams(2),
        name="in_proj",
    )(x, sc1, sh1, g1.reshape(1, d), w_cat)


def _cmp_first_layer(k_rows, v_rows, w1bd_ref, n_chunks):
    acc = None
    for l in range(CMP_STRIDE):
        rows_l = pl.ds(l, n_chunks, stride=CMP_STRIDE)
        x_l = jnp.concatenate([k_rows[rows_l, :], v_rows[rows_l, :]], axis=1)
        part = _mm(x_l, w1bd_ref[l])
        acc = part if acc is None else acc + part
    return acc


def _cmp_second_layer(first, second_next, pe_ref, w1pe_ref, w2bd_ref):
    pe_term = _mm(pe_ref[...], w1pe_ref[...])[0:1]
    return _mm(_silu(first + second_next + pe_term), w2bd_ref[...])


def _compress_prompt_kernel(k_ref, v_ref, w1bd_ref, pe_ref, w1pe_ref, w2bd_ref, o_ref, fs_ref,
                            *, n_chunks):
    half = 4 * CMP_HID
    fs_ref[pl.ds(0, n_chunks), :] = _cmp_first_layer(k_ref.at[0], v_ref.at[0], w1bd_ref, n_chunks)
    fs_ref[pl.ds(n_chunks, SUBLANES), :] = jnp.zeros((SUBLANES, 2 * half), F32)
    first = fs_ref[pl.ds(0, n_chunks), pl.ds(0, half)]
    second_next = fs_ref[pl.ds(1, n_chunks), pl.ds(half, half)]
    o_ref[0] = _cmp_second_layer(first, second_next, pe_ref, w1pe_ref, w2bd_ref)


def _compress_prompt_call(kvw, cw):
    b, t, _ = kvw.shape
    n_chunks = t // CMP_STRIDE
    wid = 4 * HEAD_DIM
    kern = functools.partial(_compress_prompt_kernel, n_chunks=n_chunks)
    full = lambda a: pl.BlockSpec(a.shape, lambda i: (0,) * a.ndim)
    return pl.pallas_call(
        kern,
        grid=(b,),
        in_specs=[pl.BlockSpec((1, t, LANES), lambda i: (i, 0, 0)),
                  pl.BlockSpec((1, t, LANES), lambda i: (i, 0, 1)),
                  full(cw["w1bd"]), full(cw["pe"]), full(cw["w1pe"]), full(cw["w2bd"])],
        out_specs=pl.BlockSpec((1, n_chunks, wid), lambda i: (i, 0, 0)),
        out_shape=jax.ShapeDtypeStruct((b, n_chunks, wid), F32),
        scratch_shapes=[pltpu.VMEM((n_chunks + SUBLANES, 8 * CMP_HID), F32)],
        compiler_params=_cparams(1),
        name="compress_prompt",
    )(kvw, kvw, cw["w1bd"], cw["pe"], cw["w1pe"], cw["w2bd"])


def _compress_weights(cmp_pe, cmp_w1, cmp_w2):
    hd, hid = HEAD_DIM, CMP_HID
    slab_kv = (0, 0, 1, 1)
    w1bd = jnp.zeros((CMP_STRIDE, 4 * hd, 8 * hid), F32)
    w2bd = jnp.zeros((4 * hid, 4 * hd), F32)
    for s, kv in enumerate(slab_kv):
        w1bd = w1bd.at[:, s * hd:(s + 1) * hd, s * hid:(s + 1) * hid].set(cmp_w1[kv, :CMP_STRIDE])
        w1bd = w1bd.at[:, s * hd:(s + 1) * hd, (4 + s) * hid:(5 + s) * hid].set(cmp_w1[kv, CMP_STRIDE:])
        w2bd = w2bd.at[s * hid:(s + 1) * hid, s * hd:(s + 1) * hd].set(cmp_w2[kv])
    pe = jnp.broadcast_to(cmp_pe.reshape(1, -1), (SUBLANES, 2 * CMP_BLOCK * hd))
    w1pe = jnp.zeros((2 * CMP_BLOCK * hd, 4 * hid), F32)
    n_flat = CMP_BLOCK * hd
    for s, kv in enumerate(slab_kv):
        w1pe = w1pe.at[kv * n_flat:(kv + 1) * n_flat, s * hid:(s + 1) * hid].set(
            cmp_w1[kv].reshape(n_flat, hid))
    return {"w1bd": w1bd.astype(MXU_DTYPE), "w2bd": w2bd.astype(MXU_DTYPE),
            "pe": pe, "w1pe": w1pe.astype(MXU_DTYPE)}


def _softmax_cols(s, mask):
    m = jnp.max(jnp.where(mask, s, NEG_INF), axis=0, keepdims=True)
    p = jnp.where(mask, jnp.exp(s - m), 0.0)
    l = jnp.sum(p, axis=0, keepdims=True)
    return p / jnp.where(l > 0.0, l, 1.0)


def _top_n_rows(imp, top_n):
    n = imp.shape[0]
    row_id = lax.broadcasted_iota(I32, imp.shape, 0)
    beaten = jnp.zeros(imp.shape, F32)
    for m in range(n):
        other = imp[m:m + 1, :]
        wins = (other > imp) | ((other == imp) & (row_id > m))
        beaten = beaten + jnp.where(wins, 1.0, 0.0)
    return beaten < float(top_n)


SEL_FEAT0 = HEAD_DIM + SUBLANES
MASK_BIG = -2.0 ** 100


def _attn_prompt_kernel(q_ref, ksv_ref, kwv_ref, kc_ref, gate_ref, o_ref,
                        kaug, vsel, kwaug, vwin, psum_ref, *, tq, kc_len, t_len):
    qt = pl.program_id(1)
    q0 = qt * tq
    hd = HEAD_DIM
    n_cmp_rows = kc_ref.shape[1]
    n_blk = t_len // SEL_BLOCK
    top_n = min(TOP_N, n_blk)
    wk = min(WINDOW + tq, t_len)

    @pl.when(qt == 0)
    def _():
        lane = lax.broadcasted_iota(I32, (t_len, LANES), 1)
        pos = lax.broadcasted_iota(I32, (t_len, LANES), 0)
        pos_hi = lax.div(pos, SEL_BLOCK)
        alibi = jnp.where(lane < hd + 2, 1.0,
                          jnp.where(lane == hd + 2, pos_hi.astype(F32),
                                    jnp.where(lane == hd + 3, lax.rem(pos, SEL_BLOCK).astype(F32), 0.0)))
        in_blk = jnp.where(lane - SEL_FEAT0 == pos_hi, 1.0, 0.0)
        for g in range(KV_HEADS):
            k_s = ksv_ref[0, :, 0:LANES]
            k_w = kwv_ref[0, :, 0:LANES]
            if g:
                k_s = pltpu.roll(k_s, LANES - g * hd, 1)
                k_w = pltpu.roll(k_w, LANES - g * hd, 1)
            kaug[g] = jnp.where(lane < hd, k_s, alibi + in_blk).astype(kaug.dtype)
            kwaug[g] = jnp.where(lane < hd, k_w, alibi).astype(kwaug.dtype)
            vsel[g] = ksv_ref[0, :, (2 + g) * hd:(3 + g) * hd].astype(vsel.dtype)
            vwin[g] = kwv_ref[0, :, (2 + g) * hd:(3 + g) * hd].astype(vwin.dtype)
        psum_ref[...] = jnp.zeros(psum_ref.shape, F32)

    q_blk = q_ref[0]
    gates = gate_ref[0]
    rows = Q_PER_KV * tq
    row_t = q0 + lax.rem(lax.broadcasted_iota(I32, (rows, 1), 0), tq)
    lane_q = lax.broadcasted_iota(I32, (tq, LANES), 1)
    t_q = q0 + lax.broadcasted_iota(I32, (tq, LANES), 0)
    t_hi = (lax.div(t_q, SEL_BLOCK) * SEL_BLOCK).astype(F32)
    t_lo = lax.rem(t_q, SEL_BLOCK).astype(F32)
    w0 = pl.multiple_of(jnp.maximum(q0 + tq - wk, 0), tq)
    dist_w = row_t - (w0 + lax.broadcasted_iota(I32, (rows, wk), 1))
    band = jnp.where((dist_w >= 0) & (dist_w < WINDOW), 0.0, NEG_INF)
    out_heads = []
    for g in range(KV_HEADS):
        q_heads = [q_blk[:, (g * Q_PER_KV + r) * hd:(g * Q_PER_KV + r + 1) * hd]
                   for r in range(Q_PER_KV)]
        k_cmp = kc_ref[0, :, g * hd:(g + 1) * hd]
        v_cmp = kc_ref[0, :, (2 + g) * hd:(3 + g) * hd]
        t_lane = q0 + lax.broadcasted_iota(I32, (n_cmp_rows, tq), 1)
        c_pos = lax.broadcasted_iota(I32, (n_cmp_rows, tq), 0) * CMP_STRIDE + (CMP_BLOCK - 1)
        dist_c = t_lane - c_pos
        mask_c = dist_c >= 0
        dist_cf = dist_c.astype(F32)
        o_cmp = []
        p_sum = None
        for r in range(Q_PER_KV):
            s = _mm_nt(k_cmp, q_heads[r]) * ATTN_SCALE - _slope(g * Q_PER_KV + r) * dist_cf
            p = _softmax_cols(s, mask_c)
            o_cmp.append(_mm_tn(p, v_cmp))
            p_sum = p if p_sum is None else p_sum + p
        psum_ref[pl.ds(SUBLANES, n_cmp_rows), :] = p_sum
        per = SEL_BLOCK // CMP_STRIDE
        taps = [psum_ref[pl.ds(SUBLANES - 1 + k, n_blk, stride=per), :] for k in range(per + 1)]
        imp = 0.5 * taps[0] + 0.5 * taps[per]
        for k in range(1, per):
            imp = imp + taps[k]
        blk = lax.broadcasted_iota(I32, (n_blk, tq), 0)
        t_blk = q0 + lax.broadcasted_iota(I32, (n_blk, tq), 1)
        cur = lax.div(t_blk, SEL_BLOCK)
        forced = (blk == 0) | (blk == cur) | (blk == cur - 1)
        imp = jnp.where(forced, FORCE_SCORE, imp)
        imp = jnp.where(blk * SEL_BLOCK <= t_blk, imp, -jnp.inf)
        sel_neg = jnp.where(_top_n_rows(imp, top_n), 0.0, MASK_BIG)
        sel_lanes = jnp.transpose(jnp.concatenate(
            [jnp.zeros((SEL_FEAT0, tq), F32), sel_neg,
             jnp.zeros((LANES - SEL_FEAT0 - n_blk, tq), F32)], axis=0))
        q_aug = []
        for r in range(Q_PER_KV):
            head = g * Q_PER_KV + r
            slope = _slope(head)
            pair = q_blk[:, (head // 2) * LANES:(head // 2 + 1) * LANES].astype(F32)
            if head % 2:
                pair = pltpu.roll(pair, LANES - hd, 1)
            feats = jnp.where(lane_q == hd, -slope * t_hi,
                              jnp.where(lane_q == hd + 1, -slope * t_lo,
                                        jnp.where(lane_q == hd + 2, slope * SEL_BLOCK,
                                                  jnp.where(lane_q == hd + 3, slope, sel_lanes))))
            q_aug.append(jnp.where(lane_q < hd, pair * ATTN_SCALE, feats).astype(MXU_DTYPE))
        q_aug = jnp.concatenate(q_aug, axis=0)

        def sel_chunk(j, carry, causal):
            m_run, l_run, acc = carry
            k0 = pl.multiple_of(j * kc_len, kc_len)
            s = _mm_nt(q_aug, kaug[g, pl.ds(k0, kc_len), :])
            if causal:
                k_pos = k0 + lax.broadcasted_iota(I32, (rows, kc_len), 1)
                s = jnp.where(k_pos <= row_t, s, NEG_INF)
            m_new = jnp.maximum(m_run, jnp.max(s, axis=1, keepdims=True))
            alpha = jnp.exp(m_run - m_new)
            p = jnp.exp(s - m_new)
            l_new = alpha * l_run + jnp.sum(p, axis=1, keepdims=True)
            acc_new = alpha * acc + _mm(p, vsel[g, pl.ds(k0, kc_len), :])
            return m_new, l_new, acc_new

        n_full = lax.div(q0, kc_len)
        init = (jnp.full((rows, 1), NEG_INF, F32), jnp.zeros((rows, 1), F32),
                jnp.zeros((rows, hd), F32))
        carry = lax.fori_loop(0, n_full, functools.partial(sel_chunk, causal=False), init)
        _, l_sel, acc_sel = sel_chunk(n_full, carry, causal=True)
        o_sel = acc_sel / l_sel
        s = _mm_nt(q_aug, kwaug[g, pl.ds(w0, wk), :]) + band
        p = jnp.exp(s - jnp.max(s, axis=1, keepdims=True))
        o_win = _mm(p, vwin[g, pl.ds(w0, wk), :]) / jnp.sum(p, axis=1, keepdims=True)
        for r in range(Q_PER_KV):
            c = (g * Q_PER_KV + r) * 3
            out_heads.append(gates[:, c:c + 1] * o_cmp[r]
                             + gates[:, c + 1:c + 2] * o_sel[r * tq:(r + 1) * tq]
                             + gates[:, c + 2:c + 3] * o_win[r * tq:(r + 1) * tq])
    o_ref[0] = jnp.concatenate(out_heads, axis=1).astype(o_ref.dtype)


def _attn_prompt_call(q, kvw, kcvc, gates, tq, kc_len):
    b, t, n_q = q.shape
    wid = 4 * HEAD_DIM
    kern = functools.partial(_attn_prompt_kernel, tq=tq, kc_len=kc_len, t_len=t)
    assert SEL_FEAT0 + t // SEL_BLOCK <= LANES and kc_len % tq == 0
    k_scratch = pltpu.VMEM((KV_HEADS, t, LANES), MXU_DTYPE)
    v_scratch = pltpu.VMEM((KV_HEADS, t, HEAD_DIM), MXU_DTYPE)
    n_cmp_rows = kcvc.shape[1]
    return pl.pallas_call(
        kern,
        grid=(b, t // tq),
        in_specs=[pl.BlockSpec((1, tq, n_q), lambda i, j: (i, j, 0)),
                  pl.BlockSpec((1, t, wid), lambda i, j: (i, 0, 1)),
                  pl.BlockSpec((1, t, wid), lambda i, j: (i, 0, 2)),
                  pl.BlockSpec((1, n_cmp_rows, wid), lambda i, j: (i, 0, 0)),
                  pl.BlockSpec((1, tq, LANES), lambda i, j: (i, j, 0))],
        out_specs=pl.BlockSpec((1, tq, n_q), lambda i, j: (i, j, 0)),
        out_shape=jax.ShapeDtypeStruct((b, t, n_q), MXU_DTYPE),
        scratch_shapes=[k_scratch, v_scratch, k_scratch, v_scratch,
                        pltpu.VMEM((n_cmp_rows + 2 * SUBLANES, tq), F32)],
        compiler_params=_cparams(2),
        name="attn_prompt",
    )(q, kvw, kvw, kcvc, gates)


CONV_PAD = 32


def _ln_silu(z, lg, lb):
    mu = jnp.mean(z, axis=-1, keepdims=True)
    zc = z - mu
    var = jnp.mean(zc * zc, axis=-1, keepdims=True)
    return _silu(zc * lax.rsqrt(var + EPS) * lg + lb)


def _conv_prompt_kernel(glu_ref, cw_ref, cb_ref, lg_ref, lb_ref, z_ref, full_ref, *, t_len, tt):
    full_ref[pl.ds(0, CONV_PAD), :] = jnp.zeros((CONV_PAD, full_ref.shape[1]), F32)
    full_ref[pl.ds(CONV_PAD, t_len), :] = glu_ref[0]
    first = CONV_PAD - (CONV_W - 1)

    def tile(i, carry):
        r0 = pl.multiple_of(i * tt, tt)
        acc = jnp.zeros((tt, full_ref.shape[1]), F32) + cb_ref[...]
        win = full_ref[pl.ds(r0, tt + CONV_PAD), :]
        for phase in range(SUBLANES):
            offs = [o for o in range(first, first + CONV_W) if o % SUBLANES == phase]
            if not offs:
                continue
            shifted = win[phase:max(offs) + tt]
            for o in offs:
                acc = acc + shifted[o - phase:o - phase + tt] * cw_ref[o - first:o - first + 1, :]
        z_ref[0, pl.ds(r0, tt), :] = _ln_silu(acc, lg_ref[...], lb_ref[...]).astype(z_ref.dtype)
        return carry

    lax.fori_loop(0, t_len // tt, tile, 0)


def _conv_prompt_call(glu, conv_w, conv_b, ln_g, ln_b, tt):
    b, t, c = glu.shape
    kern = functools.partial(_conv_prompt_kernel, t_len=t, tt=tt)
    vec = pl.BlockSpec((1, c), lambda i: (0, 0))
    return pl.pallas_call(
        kern,
        grid=(b,),
        in_specs=[pl.BlockSpec((1, t, c), lambda i: (i, 0, 0)),
                  pl.BlockSpec((CONV_PAD, c), lambda i: (0, 0)), vec, vec, vec],
        out_specs=pl.BlockSpec((1, t, c), lambda i: (i, 0, 0)),
        out_shape=jax.ShapeDtypeStruct((b, t, c), MXU_DTYPE),
        scratch_shapes=[pltpu.VMEM((CONV_PAD + t, c), F32)],
        compiler_params=_cparams(1),
        name="conv_prompt",
    )(glu, jnp.pad(conv_w, ((0, CONV_PAD - CONV_W), (0, 0))), conv_b.reshape(1, c),
      ln_g.reshape(1, c), ln_b.reshape(1, c))


def _conv_sample_kernel(hist_ref, new_ref, cw_ref, cb_ref, lg_ref, lb_ref, z_ref):
    z = new_ref[...] * cw_ref[CONV_W - 1:CONV_W, :] + cb_ref[...]
    for w in range(CONV_W - 1):
        z = z + hist_ref[w] * cw_ref[w:w + 1, :]
    z_ref[...] = _ln_silu(z, lg_ref[...], lb_ref[...]).astype(z_ref.dtype)


def _conv_sample_call(hist, glu_new, conv_w, conv_b, ln_g, ln_b):
    bs, c = glu_new.shape
    return pl.pallas_call(
        _conv_sample_kernel,
        out_shape=jax.ShapeDtypeStruct((bs, c), MXU_DTYPE),
        name="conv_sample",
    )(hist, glu_new, jnp.pad(conv_w, ((0, CONV_PAD - CONV_W), (0, 0))), conv_b.reshape(1, c),
      ln_g.reshape(1, c), ln_b.reshape(1, c))


def _route_tile(h2, rwt_ref, rb_ref, cnt_ref):
    n_e = N_EXPERTS
    per_grp = n_e // N_GROUPS
    tm = h2.shape[0]
    aff = _sigmoid(_mm_nt(rwt_ref[...], h2))
    biased = aff + rb_ref[...]
    neg = -jnp.inf
    g_rows = []
    for g in range(N_GROUPS):
        v = biased[g * per_grp:(g + 1) * per_grp]
        m1 = jnp.max(v, axis=0, keepdims=True)
        is_m1 = v == m1
        n_m1 = jnp.sum(jnp.where(is_m1, 1.0, 0.0), axis=0, keepdims=True)
        m2 = jnp.max(jnp.where(is_m1, neg, v), axis=0, keepdims=True)
        g_rows.append(m1 + jnp.where(n_m1 >= 2.0, m1, m2))
    g_keep = _top_n_rows(jnp.concatenate(g_rows, axis=0), TOPK_GROUPS)
    cur = jnp.concatenate(
        [jnp.where(g_keep[g:g + 1], biased[g * per_grp:(g + 1) * per_grp], neg)
         for g in range(N_GROUPS)], axis=0)
    row_id = lax.broadcasted_iota(I32, (n_e, tm), 0).astype(F32)
    ids, wts, hots = [], [], []
    for _ in range(TOP_K):
        m = jnp.max(cur, axis=0, keepdims=True)
        idx = jnp.min(jnp.where(cur == m, row_id, float(n_e)), axis=0, keepdims=True)
        hot = row_id == idx
        ids.append(idx)
        wts.append(jnp.sum(jnp.where(hot, aff, 0.0), axis=0, keepdims=True))
        hots.append(hot)
        cur = jnp.where(hot, neg, cur)
    w_sum = wts[0]
    for w in wts[1:]:
        w_sum = w_sum + w
    wts = [w / w_sum * ROUTED_SCALE for w in wts]
    hot_all = jnp.where(hots[0], 1.0, 0.0)
    for hot in hots[1:]:
        hot_all = hot_all + jnp.where(hot, 1.0, 0.0)
    earlier = jnp.where(lax.broadcasted_iota(I32, (tm, tm), 0) < lax.broadcasted_iota(I32, (tm, tm), 1),
                        1.0, 0.0)
    before = cnt_ref[:, 0:1] + jnp.dot(hot_all.astype(BF16), earlier.astype(BF16),
                                       preferred_element_type=F32)
    ranks = [jnp.sum(jnp.where(hot, before, 0.0), axis=0, keepdims=True) for hot in hots]
    cnt_ref[...] = cnt_ref[...] + jnp.sum(hot_all, axis=1, keepdims=True)
    cat = lambda rows: jnp.concatenate(rows, axis=0)
    return cat(ids).astype(I32), cat(wts), cat(ranks).astype(I32)


def _merge_kernel(x_ref, oa_ref, z_ref, sc1_ref, sh1_ref, gt1_ref, sc2_ref, sh2_ref,
                  g1_ref, wm_ref, wa_ref, wb_ref, wo_ref, g2_ref, rwt_ref, rb_ref,
                  s13_ref, s2_ref,
                  x1_ref, h2_ref, ysh_ref, eid_ref, ew_ref, rk_ref, cnt_out_ref, cnt_ref, *, d_exp):
    first_step = (pl.program_id(0) == 0) & (pl.program_id(1) == 0)

    @pl.when(first_step)
    def _():
        cnt_ref[...] = jnp.zeros(cnt_ref.shape, F32)

    x = x_ref[0]
    d = x.shape[1]
    tm = x.shape[0]
    h = _rms(x, g1_ref[...]) * (1.0 + sc1_ref[0]) + sh1_ref[0]
    g_mrg = _sigmoid(_mm(h, wm_ref[...]))
    mixed = g_mrg[:, :d] * _mm(oa_ref[0], wa_ref[...]) + g_mrg[:, d:] * _mm(z_ref[0], wb_ref[...])
    x1 = x + gt1_ref[0] * _mm(mixed, wo_ref[...])
    x1_ref[0] = x1
    h2 = _rms(x1, g2_ref[...]) * (1.0 + sc2_ref[0]) + sh2_ref[0]
    for s in range(d // LANES):
        h2_ref[pl.ds(s, tm, stride=d // LANES), :] = h2[:, s * LANES:(s + 1) * LANES]
    hs = _mm(h2, s13_ref[...])
    ysh_ref[0] = _mm(_silu(hs[:, :d_exp]) * hs[:, d_exp:], s2_ref[...])
    ids, wts, ranks = _route_tile(h2, rwt_ref, rb_ref, cnt_ref)
    eid_ref[...] = ids
    ew_ref[...] = wts
    rk_ref[...] = ranks
    cnt_out_ref[...] = cnt_ref[...]


def _merge_call(x, o_attn, z, mods, mw, tm):
    bx, tx, d = x.shape
    n = bx * tx
    nt = tx // tm
    d_exp = mw["s2"].shape[0]
    kern = functools.partial(_merge_kernel, d_exp=d_exp)
    row = lambda w: pl.BlockSpec((1, tm, w), lambda b, i: (b, i, 0))
    full = lambda a: pl.BlockSpec(a.shape, lambda b, i: (0,) * a.ndim)
    tok = pl.BlockSpec((TOP_K, tm), lambda b, i: (0, b * nt + i))
    wnames = ("g1", "wm", "wa", "wb", "wo", "g2", "rwt", "rb", "s13", "s2")
    return pl.pallas_call(
        kern,
        grid=(bx, nt),
        in_specs=[row(d), row(o_attn.shape[2]), row(z.shape[2])]
                 + [_mod_spec(m, tm, d) for m in mods]
                 + [full(mw[k]) for k in wnames],
        out_specs=[row(d),
                   pl.BlockSpec((tm * (d // LANES), LANES), lambda b, i: (b * nt + i, 0)),
                   row(d), tok, tok, tok,
                   pl.BlockSpec((N_EXPERTS, LANES), lambda b, i: (0, 0))],
        out_shape=[jax.ShapeDtypeStruct((bx, tx, d), F32),
                   jax.ShapeDtypeStruct((n * (d // LANES), LANES), F32),
                   jax.ShapeDtypeStruct((bx, tx, d), F32),
                   jax.ShapeDtypeStruct((TOP_K, n), I32),
                   jax.ShapeDtypeStruct((TOP_K, n), F32),
                   jax.ShapeDtypeStruct((TOP_K, n), I32),
                   jax.ShapeDtypeStruct((N_EXPERTS, LANES), F32)],
        scratch_shapes=[pltpu.VMEM((N_EXPERTS, LANES), F32)],
        compiler_params=_cparams(2),
        name="merge_route",
    )(x, o_attn, z, *mods, *[mw[k] for k in wnames])


def _start_tile_copy(src_hbm, src_row, buf, slot, dst_row, sem, n_sub):
    pltpu.make_async_copy(src_hbm.at[pl.ds(pl.multiple_of(src_row, n_sub), n_sub), :],
                          buf.at[slot, pl.ds(pl.multiple_of(dst_row, n_sub), n_sub), :],
                          sem.at[slot]).start()


def _wait_slot(buf, sem, slot):
    pltpu.make_async_copy(buf.at[slot], buf.at[slot], sem.at[slot]).wait()


def _moe_plan(counts, n_asg, rb):
    n_blk = -(-(n_asg + N_EXPERTS * (rb - 1)) // rb)
    padded = (counts + rb - 1) // rb * rb
    pad_end = jnp.cumsum(padded)
    start = pad_end - padded
    blk_row = jnp.arange(n_blk, dtype=I32)[:, None] * rb
    blk_e = jnp.minimum(jnp.sum((pad_end[None, :] <= blk_row).astype(I32), axis=1), N_EXPERTS - 1)
    n_used = (pad_end[-1] // rb).astype(I32).reshape(1)
    return start.astype(I32), padded.astype(I32), blk_e.astype(I32), n_used, n_blk


def _dispatch_kernel(start_ref, cnt_ref, pad_ref, n_used_ref,
                     eid_ref, rk_ref, eid_s_ref, rk_s_ref, h2p_hbm, h2s_hbm, xs_hbm, zeros, sem,
                     *, tmd, n_p_steps, bs, n_sub, rb, n_blk):
    i = pl.program_id(0)

    def scatter_tokens(src_hbm, src_base, e_ref, r_ref, n_tok):
        def issue(t, carry):
            src = src_hbm.at[pl.ds(pl.multiple_of((src_base + t) * n_sub, n_sub), n_sub), :]
            for k in range(TOP_K):
                dst_row = (start_ref[e_ref[k, t]] + r_ref[k, t]) * n_sub
                pltpu.make_async_copy(
                    src, xs_hbm.at[pl.ds(pl.multiple_of(dst_row, n_sub), n_sub), :], sem).start()
            return carry
        lax.fori_loop(0, n_tok, issue, 0)
        done = xs_hbm.at[pl.ds(0, n_tok * TOP_K * n_sub), :]
        pltpu.make_async_copy(done, done, sem).wait()

    @pl.when(i < n_p_steps)
    def _():
        scatter_tokens(h2p_hbm, i * tmd, eid_ref, rk_ref, tmd)

    @pl.when(i == n_p_steps)
    def _():
        scatter_tokens(h2s_hbm, 0, eid_s_ref, rk_s_ref, bs)
        zeros[...] = jnp.zeros(zeros.shape, F32)

        def zero_rows(first_row, n_rows):
            pltpu.make_async_copy(
                zeros.at[pl.ds(0, n_rows * n_sub), :],
                xs_hbm.at[pl.ds(pl.multiple_of(first_row * n_sub, n_sub), n_rows * n_sub), :],
                sem).start()

        def pad_expert(e, carry):
            n_pad = pad_ref[e] - cnt_ref[e]
            row = start_ref[e] + cnt_ref[e]
            piece = rb // 2
            while piece >= 1:
                take = (n_pad & piece) != 0

                @pl.when(take)
                def _(row=row, piece=piece):
                    zero_rows(row, piece)

                row = row + jnp.where(take, piece, 0)
                piece //= 2
            return carry

        lax.fori_loop(0, N_EXPERTS, pad_expert, 0)

        def pad_block(blk, carry):
            zero_rows(blk * rb, rb)
            return carry

        lax.fori_loop(n_used_ref[0], n_blk, pad_block, 0)
        n_zero = n_blk * rb - (n_p_steps * tmd + bs) * TOP_K
        done = xs_hbm.at[pl.ds(0, n_zero * n_sub), :]
        pltpu.make_async_copy(done, done, sem).wait()


def _dispatch_call(start, counts, padded, n_used, eid_p, rank_p, eid_s, rank_s, h2_p, h2_s,
                   n_blk, rb, tmd):
    n_p = eid_p.shape[1]
    bs = eid_s.shape[1]
    n_sub = h2_p.shape[0] // n_p
    n_p_steps = n_p // tmd
    kern = functools.partial(_dispatch_kernel, tmd=tmd, n_p_steps=n_p_steps, bs=bs, n_sub=n_sub,
                             rb=rb, n_blk=n_blk)
    tile = pl.BlockSpec((TOP_K, tmd), lambda i, *_: (0, jnp.minimum(i, n_p_steps - 1)),
                        memory_space=pltpu.SMEM)
    whole = pl.BlockSpec((TOP_K, bs), lambda i, *_: (0, 0), memory_space=pltpu.SMEM)
    hbm = pl.BlockSpec(memory_space=pl.ANY)
    grid_spec = pltpu.PrefetchScalarGridSpec(
        num_scalar_prefetch=4,
        grid=(n_p_steps + 1,),
        in_specs=[tile, tile, whole, whole, hbm, hbm],
        out_specs=hbm,
        scratch_shapes=[pltpu.VMEM((rb * n_sub, LANES), F32), pltpu.SemaphoreType.DMA],
    )
    return pl.pallas_call(
        kern,
        grid_spec=grid_spec,
        out_shape=jax.ShapeDtypeStruct((n_blk * rb * n_sub, LANES), F32),
        compiler_params=_cparams(1),
        name="moe_dispatch",
    )(start, counts, padded, n_used, eid_p, rank_p, eid_s, rank_s, h2_p, h2_s)


def _experts_kernel(blk_e_ref, n_used_ref, x_ref, w1_ref, w3_ref, w2_ref, y_ref, *, rb, n_sub):
    i = pl.program_id(0)

    @pl.when(i < n_used_ref[0])
    def _():
        x = jnp.concatenate([x_ref[pl.ds(s, rb, stride=n_sub), :] for s in range(n_sub)], axis=1)
        hid = _silu(_mm(x, w1_ref[0])) * _mm(x, w3_ref[0])
        y = _mm(hid, w2_ref[0])
        for s in range(n_sub):
            y_ref[pl.ds(s, rb, stride=n_sub), :] = y[:, s * LANES:(s + 1) * LANES]

    @pl.when(i >= n_used_ref[0])
    def _():
        y_ref[...] = jnp.zeros(y_ref.shape, F32)


def _experts_call(x_rows, blk_e, n_used, w1, w3, w2, rb):
    _, d, d_exp = w1.shape
    n_sub = d // LANES
    n_blk = x_rows.shape[0] // (rb * n_sub)
    kern = functools.partial(_experts_kernel, rb=rb, n_sub=n_sub)
    grid_spec = pltpu.PrefetchScalarGridSpec(
        num_scalar_prefetch=2,
        grid=(n_blk,),
        in_specs=[pl.BlockSpec((rb * n_sub, LANES), lambda i, be, nu: (jnp.minimum(i, nu[0] - 1), 0)),
                  pl.BlockSpec((1, d, d_exp), lambda i, be, nu: (be[i], 0, 0)),
                  pl.BlockSpec((1, d, d_exp), lambda i, be, nu: (be[i], 0, 0)),
                  pl.BlockSpec((1, d_exp, d), lambda i, be, nu: (be[i], 0, 0))],
        out_specs=pl.BlockSpec((rb * n_sub, LANES), lambda i, be, nu: (i, 0)),
    )
    return pl.pallas_call(
        kern,
        grid_spec=grid_spec,
        out_shape=jax.ShapeDtypeStruct((n_blk * rb * n_sub, LANES), F32),
        compiler_params=_cparams(1),
        name="routed_experts",
    )(blk_e, n_used, x_rows, w1, w3, w2)


def _combine_kernel(start_ref, eid_ref, rk_ref, eid_next_ref, rk_next_ref, y_hbm, ew_ref, x1_ref,
                    ysh_ref, gt2_ref, gf_ref, o_ref, buf, sem, *, tm, n_sub, nt, n_steps):
    step = pl.program_id(0) * nt + pl.program_id(1)
    slot = lax.rem(step, 2)

    def gather(e_ref, r_ref, to_slot):
        def issue(t, carry):
            for k in range(TOP_K):
                src_row = (start_ref[e_ref[k, t]] + r_ref[k, t]) * n_sub
                _start_tile_copy(y_hbm, src_row, buf, to_slot, (k * tm + t) * n_sub, sem, n_sub)
            return carry
        lax.fori_loop(0, tm, issue, 0)

    @pl.when(step == 0)
    def _():
        gather(eid_ref, rk_ref, 0)

    @pl.when(step + 1 < n_steps)
    def _():
        gather(eid_next_ref, rk_next_ref, 1 - slot)

    _wait_slot(buf, sem, slot)
    ew = ew_ref[...]
    cols = []
    for s in range(n_sub):
        acc = None
        for k in range(TOP_K):
            term = ew[:, k:k + 1] * buf[slot, pl.ds(k * tm * n_sub + s, tm, stride=n_sub), :]
            acc = term if acc is None else acc + term
        cols.append(acc)
    y_routed = jnp.concatenate(cols, axis=1)
    x2 = x1_ref[0] + gt2_ref[0] * (y_routed + ysh_ref[0])
    o_ref[0] = _rms(x2, gf_ref[...])


def _combine_call(start, eid, rank, y_rows, ew, x1, ysh, gt2, g_final, tm):
    bx, tx, d = x1.shape
    nt = tx // tm
    n_sub = d // LANES
    n_steps = bx * nt
    kern = functools.partial(_combine_kernel, tm=tm, n_sub=n_sub, nt=nt, n_steps=n_steps)
    row = pl.BlockSpec((1, tm, d), lambda b, i, st: (b, i, 0))
    cur = pl.BlockSpec((TOP_K, tm), lambda b, i, st: (0, b * nt + i), memory_space=pltpu.SMEM)
    nxt = pl.BlockSpec((TOP_K, tm), lambda b, i, st: (0, jnp.minimum(b * nt + i + 1, n_steps - 1)),
                       memory_space=pltpu.SMEM)
    if gt2.shape[1] == 1:
        gt2_spec = pl.BlockSpec((1, 1, d), lambda b, i, st: (b, 0, 0))
    else:
        gt2_spec = pl.BlockSpec((1, tm, d), lambda b, i, st: (b, i, 0))
    grid_spec = pltpu.PrefetchScalarGridSpec(
        num_scalar_prefetch=1,
        grid=(bx, nt),
        in_specs=[cur, cur, nxt, nxt,
                  pl.BlockSpec(memory_space=pl.ANY),
                  pl.BlockSpec((tm, TOP_K), lambda b, i, st: (b * nt + i, 0)),
                  row, row, gt2_spec,
                  pl.BlockSpec((1, d), lambda b, i, st: (0, 0))],
        out_specs=row,
        scratch_shapes=[pltpu.VMEM((2, TOP_K * tm * n_sub, LANES), F32),
                        pltpu.SemaphoreType.DMA((2,))],
    )
    return pl.pallas_call(
        kern,
        grid_spec=grid_spec,
        out_shape=jax.ShapeDtypeStruct((bx, tx, d), F32),
        compiler_params=_cparams(2),
        name="combine_final",
    )(start, eid, rank, eid, rank, y_rows, ew, x1, ysh, gt2, g_final.reshape(1, d))


def _compress_sample_kernel(pt_ref, cache_hbm, w1bd_ref, fs_ref, raw, kbuf, vbuf, sem,
                            *, pp, page_rows, n_steps):
    step = pl.program_id(0) * pl.num_programs(1) + pl.program_id(1)
    slot = lax.rem(step, 2)

    def fetch(of_step, to_slot):
        def issue(p, carry):
            page = pt_ref[of_step * pp + p]
            pltpu.make_async_copy(cache_hbm.at[page, pl.ds(0, 2)], raw.at[to_slot, p],
                                  sem.at[to_slot]).start()
            return carry
        lax.fori_loop(0, pp, issue, 0)

    @pl.when(step == 0)
    def _():
        fetch(0, 0)

    @pl.when(step + 1 < n_steps)
    def _():
        fetch(step + 1, 1 - slot)

    _wait_slot(raw, sem, slot)

    def to_rows(p, carry):
        rows = pl.ds(pl.multiple_of(p * page_rows, page_rows), page_rows)
        kbuf[rows, :] = raw[slot, p, 0].reshape(LANES, page_rows).T
        vbuf[rows, :] = raw[slot, p, 1].reshape(LANES, page_rows).T
        return carry

    lax.fori_loop(0, pp, to_rows, 0)
    n_chunks = pp * page_rows // CMP_STRIDE
    fs_ref[0] = _cmp_first_layer(kbuf, vbuf, w1bd_ref, n_chunks)


def _compress_sample_call(page_table, cache5, w1bd, pp):
    bs, n_pages = page_table.shape
    _, _, n_g, hd, page_rows = cache5.shape
    assert n_g * hd == LANES
    n_chunks = pp * page_rows // CMP_STRIDE
    spb = n_pages // pp
    kern = functools.partial(_compress_sample_kernel, pp=pp, page_rows=page_rows, n_steps=bs * spb)
    buf = pltpu.VMEM((pp * page_rows, LANES), F32)
    grid_spec = pltpu.PrefetchScalarGridSpec(
        num_scalar_prefetch=1,
        grid=(bs, spb),
        in_specs=[pl.BlockSpec(memory_space=pl.ANY),
                  pl.BlockSpec(w1bd.shape, lambda b, i, pt: (0, 0, 0))],
        out_specs=pl.BlockSpec((1, n_chunks, 8 * CMP_HID), lambda b, i, pt: (b, i, 0)),
        scratch_shapes=[pltpu.VMEM((2, pp, 2, n_g, hd, page_rows), F32), buf, buf,
                        pltpu.SemaphoreType.DMA((2,))],
    )
    return pl.pallas_call(
        kern,
        grid_spec=grid_spec,
        out_shape=jax.ShapeDtypeStruct((bs, spb * n_chunks, 8 * CMP_HID), F32),
        compiler_params=_cparams(2),
        name="compress_sample",
    )(page_table.reshape(-1), cache5, w1bd)


def _head_slopes(n_rows):
    head = lax.broadcasted_iota(I32, (n_rows, 1), 0)
    slopes = jnp.zeros((n_rows, 1), F32)
    for h in range(N_HEADS):
        slopes = jnp.where(head == h, _slope(h), slopes)
    return slopes


def _attn_sample_a_kernel(q_ref, fs_ref, pe_ref, w1pe_ref, w2bd_ref, ocmp_ref, idx_ref,
                          *, t_pos, n_blk, n_blk_pad):
    hd = HEAD_DIM
    f = fs_ref[0]
    n_c = f.shape[0]
    half = 4 * CMP_HID
    second_next = pltpu.roll(f[:, half:], n_c - 1, 0)
    kcvc = _cmp_second_layer(f[:, :half], second_next, pe_ref, w1pe_ref, w2bd_ref)
    q = q_ref[0]
    c_idx = lax.broadcasted_iota(I32, (1, n_c), 1)
    dist_c = t_pos - (c_idx * CMP_STRIDE + (CMP_BLOCK - 1))
    mask_c = dist_c >= 0
    slopes = _head_slopes(N_HEADS)
    head_grp = lax.div(lax.broadcasted_iota(I32, (N_HEADS, 1), 0), Q_PER_KV)
    per = SEL_BLOCK // CMP_STRIDE
    c_row = lax.broadcasted_iota(I32, (n_c, n_blk_pad), 0)
    lo = lax.broadcasted_iota(I32, (n_c, n_blk_pad), 1) * per
    spread = jnp.where((c_row >= lo) & (c_row < lo + per - 1), 1.0, 0.0) \
        + jnp.where((c_row == lo - 1) | (c_row == lo + per - 1), 0.5, 0.0)
    blk_lane = lax.broadcasted_iota(I32, (1, n_blk_pad), 1)
    cur = t_pos // SEL_BLOCK
    forced = (blk_lane == 0) | (blk_lane == cur) | (blk_lane == cur - 1)
    in_range = (blk_lane * SEL_BLOCK <= t_pos) & (blk_lane < n_blk)
    n_sq = (n_blk_pad, n_blk_pad)
    sub_id = lax.broadcasted_iota(I32, n_sq, 0)
    lane_id = lax.broadcasted_iota(I32, n_sq, 1)
    top_n = min(TOP_N, n_blk)
    o_cmp = jnp.zeros((N_HEADS, hd), F32)
    idx_rows = []
    for g in range(KV_HEADS):
        k_cmp = kcvc[:, g * hd:(g + 1) * hd]
        v_cmp = kcvc[:, (2 + g) * hd:(3 + g) * hd]
        s = _mm_nt(q, k_cmp) * ATTN_SCALE - slopes * dist_c.astype(F32)
        m = jnp.max(jnp.where(mask_c, s, NEG_INF), axis=1, keepdims=True)
        p = jnp.where(mask_c, jnp.exp(s - m), 0.0)
        l = jnp.sum(p, axis=1, keepdims=True)
        p = p / jnp.where(l > 0.0, l, 1.0)
        o_cmp = jnp.where(head_grp == g, _mm(p, v_cmp), o_cmp)
        p_sum = jnp.sum(jnp.where(head_grp == g, p, 0.0), axis=0, keepdims=True)
        imp = jnp.dot(jnp.broadcast_to(p_sum, (SUBLANES, n_c)), spread,
                      precision=lax.Precision.HIGHEST, preferred_element_type=F32)[0:1]
        imp = jnp.where(forced, FORCE_SCORE, imp)
        imp = jnp.where(in_range, imp, -jnp.inf)
        imp_col = jnp.transpose(jnp.broadcast_to(imp, (LANES, n_blk_pad)))[:, 0:1]
        beats = (imp > imp_col) | ((imp == imp_col) & (lane_id < sub_id))
        rank_col = jnp.sum(jnp.where(beats, 1.0, 0.0), axis=1, keepdims=True)
        sel_col = jnp.where(rank_col < float(top_n), 1.0, 0.0)
        before = jnp.dot(jnp.where(lane_id < sub_id, 1.0, 0.0).astype(BF16),
                         jnp.broadcast_to(sel_col, (n_blk_pad, LANES)).astype(BF16),
                         preferred_element_type=F32)
        slot_lane = lax.broadcasted_iota(I32, (n_blk_pad, LANES), 1).astype(F32)
        blk_sub = lax.broadcasted_iota(I32, (n_blk_pad, LANES), 0).astype(F32)
        hit = (sel_col > 0.5) & (before == slot_lane)
        idx_rows.append(jnp.sum(jnp.where(hit, blk_sub, 0.0), axis=0, keepdims=True))
    ocmp_ref[0] = o_cmp
    pad = jnp.zeros((SUBLANES - KV_HEADS, LANES), F32)
    idx_ref[0] = jnp.concatenate(idx_rows + [pad], axis=0).astype(I32)


def _attn_sample_a_call(q_heads, fs, cw, t_pos, n_blk):
    bs = q_heads.shape[0]
    n_blk_pad = -(-n_blk // LANES) * LANES
    kern = functools.partial(_attn_sample_a_kernel, t_pos=t_pos, n_blk=n_blk, n_blk_pad=n_blk_pad)
    full = lambda a: pl.BlockSpec(a.shape, lambda b: (0,) * a.ndim)
    per_b = lambda a: pl.BlockSpec((1,) + a.shape[1:], lambda b: (b,) + (0,) * (a.ndim - 1))
    return pl.pallas_call(
        kern,
        grid=(bs,),
        in_specs=[per_b(q_heads), per_b(fs), full(cw["pe"]), full(cw["w1pe"]), full(cw["w2bd"])],
        out_specs=[pl.BlockSpec((1, N_HEADS, HEAD_DIM), lambda b: (b, 0, 0)),
                   pl.BlockSpec((1, SUBLANES, LANES), lambda b: (b, 0, 0))],
        out_shape=[jax.ShapeDtypeStruct((bs, N_HEADS, HEAD_DIM), F32),
                   jax.ShapeDtypeStruct((bs, SUBLANES, LANES), I32)],
        compiler_params=_cparams(1),
        name="attn_sample_select",
    )(q_heads, fs, cw["pe"], cw["w1pe"], cw["w2bd"])


def _attn_sample_b_kernel(sel_ref, pt_ref, q_ref, cache_hbm, kvn_ref, win_ref, gate_ref, ocmp_ref,
                          o_ref, kbuf, vbuf, sem,
                          *, t_pos, top_n, n_past_blk, n_pages, per_page, n_steps):
    b = pl.program_id(0)
    slot = lax.rem(b, 2)
    hd = HEAD_DIM
    page_rows = kbuf.shape[-1]

    def fetch(of_b, to_slot):
        for g in range(KV_HEADS):
            def issue(i, carry, g=g):
                blk = jnp.minimum(sel_ref[(of_b * KV_HEADS + g) * top_n + i], n_past_blk - 1)
                page = pt_ref[of_b * n_pages + blk // per_page]
                pltpu.make_async_copy(cache_hbm.at[page, 2, g], kbuf.at[to_slot, g, i],
                                      sem.at[to_slot]).start()
                pltpu.make_async_copy(cache_hbm.at[page, 3, g], vbuf.at[to_slot, g, i],
                                      sem.at[to_slot]).start()
                return carry
            lax.fori_loop(0, top_n, issue, 0)

    @pl.when(b == 0)
    def _():
        fetch(0, 0)

    @pl.when(b + 1 < n_steps)
    def _():
        fetch(b + 1, 1 - slot)

    _wait_slot(kbuf, sem, slot)
    _wait_slot(vbuf, sem, slot)
    q = q_ref[0]
    qf = q.astype(F32)
    slopes = _head_slopes(N_HEADS)
    kvn = kvn_ref[0]
    rnd = lambda a: a.astype(MXU_DTYPE).astype(F32)
    new_col = lambda c: rnd(kvn[:, c * hd:(c + 1) * hd])
    gates = gate_ref[0]
    head_grp = lax.div(lax.broadcasted_iota(I32, (N_HEADS, 1), 0), Q_PER_KV)
    lane = lax.broadcasted_iota(I32, (1, page_rows), 1)
    out = jnp.zeros((N_HEADS, hd), F32)
    for g in range(KV_HEADS):
        scores = []
        for i in range(top_n):
            blk = sel_ref[(b * KV_HEADS + g) * top_n + i]
            k_pos = (blk // per_page) * page_rows + lane
            picked = (lax.div(lane, SEL_BLOCK) == lax.rem(blk, per_page)) & (blk < n_past_blk)
            s = _mm(q, kbuf[slot, g, i]) * ATTN_SCALE - slopes * (t_pos - k_pos).astype(F32)
            scores.append(jnp.where(picked, s, NEG_INF))
        s_all = jnp.concatenate(scores, axis=1)
        s_new = jnp.sum(qf * new_col(4 + g), axis=1, keepdims=True) * ATTN_SCALE
        m = jnp.maximum(jnp.max(s_all, axis=1, keepdims=True), s_new)
        p_all = jnp.exp(s_all - m)
        p_new = jnp.exp(s_new - m)
        acc = p_new * new_col(6 + g)
        for i in range(top_n):
            acc = acc + _mm_nt(p_all[:, i * page_rows:(i + 1) * page_rows], vbuf[slot, g, i])
        o_sel = acc / (jnp.sum(p_all, axis=1, keepdims=True) + p_new)
        w_rows = win_ref.shape[-1]
        dist_w = (w_rows - lax.broadcasted_iota(I32, (1, w_rows), 1)).astype(F32)
        s_w = _mm(q, win_ref[0, 0, g]) * ATTN_SCALE - slopes * dist_w
        s_w = jnp.where(dist_w < float(WINDOW), s_w, NEG_INF)
        sw_new = jnp.sum(qf * new_col(8 + g), axis=1, keepdims=True) * ATTN_SCALE
        m_w = jnp.maximum(jnp.max(s_w, axis=1, keepdims=True), sw_new)
        p_w = jnp.exp(s_w - m_w)
        pw_new = jnp.exp(sw_new - m_w)
        o_win = (_mm_nt(p_w, win_ref[0, 1, g]) + pw_new * new_col(10 + g)) \
            / (jnp.sum(p_w, axis=1, keepdims=True) + pw_new)
        o = gates[:, 0:1] * ocmp_ref[0] + gates[:, 1:2] * o_sel + gates[:, 2:3] * o_win
        out = jnp.where(head_grp == g, o, out)
    o_ref[0] = out.astype(o_ref.dtype)


def _attn_sample_b_call(sel_idx, page_table, q_heads, cache5, kv_new, win5, gates3, o_cmp,
                        t_pos, top_n):
    bs, n_pages = page_table.shape
    _, _, n_g, hd, page_rows = cache5.shape
    per_page = page_rows // SEL_BLOCK
    n_past_blk = n_pages * per_page
    kern = functools.partial(_attn_sample_b_kernel, t_pos=t_pos, top_n=top_n, n_past_blk=n_past_blk,
                             n_pages=n_pages, per_page=per_page, n_steps=bs)
    per_b = lambda a: pl.BlockSpec((1,) + a.shape[1:], lambda b, sel, pt: (b,) + (0,) * (a.ndim - 1))
    tiles = pltpu.VMEM((2, n_g, top_n, hd, page_rows), F32)
    grid_spec = pltpu.PrefetchScalarGridSpec(
        num_scalar_prefetch=2,
        grid=(bs,),
        in_specs=[per_b(q_heads), pl.BlockSpec(memory_space=pl.ANY),
                  per_b(kv_new), per_b(win5), per_b(gates3), per_b(o_cmp)],
        out_specs=pl.BlockSpec((1, N_HEADS, HEAD_DIM), lambda b, sel, pt: (b, 0, 0)),
        scratch_shapes=[tiles, tiles, pltpu.SemaphoreType.DMA((2,))],
    )
    return pl.pallas_call(
        kern,
        grid_spec=grid_spec,
        out_shape=jax.ShapeDtypeStruct((bs, N_HEADS, HEAD_DIM), MXU_DTYPE),
        compiler_params=_cparams(1),
        name="attn_sample_gather",
    )(sel_idx, page_table.reshape(-1), q_heads, cache5, kv_new, win5, gates3, o_cmp)


def _prep_weights(w_in, w_a, w_b, w_out, router_w, router_b, sh_w1, sh_w3, sh_w2, g_norm1, g_norm2):
    d = w_in.shape[0]
    n_q = N_HEADS * HEAD_DIM
    n_kv = 6 * KV_HEADS * HEAD_DIM
    n_gate = 3 * N_HEADS
    o_gate = n_q + n_kv
    o_glu = o_gate + n_gate
    o_mrg = o_glu + d
    c = lambda a: a.astype(MXU_DTYPE)
    w_cat = jnp.concatenate(
        [w_in[:, :o_gate], w_in[:, o_glu:o_mrg],
         jnp.pad(w_in[:, o_gate:o_glu], ((0, 0), (0, LANES - n_gate)))], axis=1)
    mw = {"g1": g_norm1.reshape(1, d), "wm": c(w_in[:, o_mrg:]), "wa": c(w_a), "wb": c(w_b),
          "wo": c(w_out), "g2": g_norm2.reshape(1, d), "rwt": c(router_w.T),
          "rb": router_b.reshape(-1, 1), "s13": c(jnp.concatenate([sh_w1, sh_w3], axis=1)),
          "s2": c(sh_w2)}
    return c(w_cat), mw


def kernel(x_prompt, x_sample, cache_kv, cache_win, state_conv, page_table, c_prompt, c_sample, w_ada, b_ada, g_norm1, w_in, cmp_pe, cmp_w1, cmp_w2, conv_w, conv_b, conv_ln_g, conv_ln_b, w_a, w_b, w_out, g_norm2, router_w, router_b, exp_w1, exp_w3, exp_w2, sh_w1, sh_w3, sh_w2, g_final):
    l = 0
    assert w_ada.shape[0] == 1 and x_sample.shape[1] == 1 and cache_win.shape[2] == WINDOW
    bp, sp, d = x_prompt.shape
    bs = x_sample.shape[0]
    n_pool, page_rows = cache_kv.shape[1], cache_kv.shape[2]
    n_pages = page_table.shape[1]
    past_len = n_pages * page_rows
    hd = HEAD_DIM
    c_conv = d // 2
    n_sub = d // LANES
    n_kv_new = 4 * KV_HEADS * hd
    rb = 256

    mod = _ada_call(jnp.concatenate([c_prompt, c_sample], axis=0), w_ada[l], b_ada[l])
    mods_p = [mod[:bp, None, k * d:(k + 1) * d] for k in range(6)]
    mods_s = [mod[None, bp:, k * d:(k + 1) * d] for k in range(6)]
    w_cat, mw = _prep_weights(w_in[l], w_a[l], w_b[l], w_out[l], router_w[l], router_b[l],
                              sh_w1[l], sh_w3[l], sh_w2[l], g_norm1[l], g_norm2[l])
    cw = _compress_weights(cmp_pe[l], cmp_w1[l], cmp_w2[l])

    sh1, sc1, gt1, sh2, sc2, gt2 = mods_p
    q, kvw, glu, gates = _inproj_call(x_prompt, sc1, sh1, g_norm1[l], w_cat, min(sp, 512))
    kcvc = _compress_prompt_call(kvw, cw)
    o_attn = _attn_prompt_call(q, kvw, kcvc, gates, 128, 512 if sp % 512 == 0 else 256)
    z = _conv_prompt_call(glu, conv_w[l], conv_b[l], conv_ln_g[l], conv_ln_b[l], 64)
    x1_p, h2_p, ysh_p, eid_p, ew_p, rank_p, cnt_p = _merge_call(
        x_prompt, o_attn, z, (sc1, sh1, gt1, sc2, sh2), mw, min(sp, 256))

    sh1s, sc1s, gt1s, sh2s, sc2s, gt2s = mods_s
    xs = x_sample.reshape(1, bs, d)
    q_s, kvw_s, glu_s, gates_s = _inproj_call(xs, sc1s, sh1s, g_norm1[l], w_cat, bs)
    cache5 = jnp.transpose(cache_kv[l], (0, 2, 3, 4, 1))
    win5 = jnp.transpose(cache_win[l], (0, 2, 3, 4, 1))
    fs = _compress_sample_call(page_table, cache5, cw["w1bd"], min(n_pages, 32))
    q_heads = q_s.reshape(bs, N_HEADS, hd)
    n_blk = -(-(past_len + 1) // SEL_BLOCK)
    top_n = min(TOP_N, n_blk)
    o_cmp_s, sel = _attn_sample_a_call(q_heads, fs, cw, past_len, n_blk)
    o_s = _attn_sample_b_call(
        sel[:, :KV_HEADS, :top_n].reshape(-1), page_table, q_heads, cache5,
        kvw_s.reshape(bs, 1, -1), win5,
        gates_s[0, :, :3 * N_HEADS].reshape(bs, N_HEADS, 3), o_cmp_s, past_len, top_n)
    glu_new = glu_s.reshape(bs, 1, c_conv)
    z_s = _conv_sample_call(jnp.transpose(state_conv[l], (1, 0, 2)), glu_s[0],
                            conv_w[l], conv_b[l], conv_ln_g[l], conv_ln_b[l])
    x1_s, h2_s, ysh_s, eid_s, ew_s, rank_s, cnt_s = _merge_call(
        xs, o_s.reshape(1, bs, N_HEADS * hd), z_s.reshape(1, bs, c_conv),
        (sc1s, sh1s, gt1s, sc2s, sh2s), mw, bs)

    n_p = bp * sp
    cnt_p = cnt_p[:, 0].astype(I32)
    cnt_s = cnt_s[:, 0].astype(I32)
    counts = cnt_p + cnt_s
    rank_s = rank_s + cnt_p[eid_s]
    start, padded, blk_e, n_used, n_blk = _moe_plan(counts, (n_p + bs) * TOP_K, rb)
    x_rows = _dispatch_call(start, counts, padded, n_used, eid_p, rank_p, eid_s, rank_s,
                            h2_p, h2_s, n_blk, rb, min(n_p, 512))
    y_rows = _experts_call(x_rows, blk_e, n_used, exp_w1[l], exp_w3[l], exp_w2[l], rb)
    y_prompt = _combine_call(start, eid_p, rank_p, y_rows, ew_p.T, x1_p, ysh_p, gt2, g_final,
                             min(sp, 128))
    y_sample = _combine_call(start, eid_s, rank_s, y_rows, ew_s.T, x1_s, ysh_s, gt2s, g_final, bs)

    w_keep = min(WINDOW, sp)
    kv_prompt = kvw[:, :, :n_kv_new].reshape(1, bp, sp, 4, KV_HEADS, hd)
    kv_sample = kvw_s[0, :, :n_kv_new].reshape(1, bs, 1, 4, KV_HEADS, hd)
    win_prompt = kvw[:, sp - w_keep:, n_kv_new:].reshape(1, bp, w_keep, 2, KV_HEADS, hd)
    win_new = kvw_s[0, :, n_kv_new:].reshape(bs, 1, 2, KV_HEADS, hd)
    win_sample = jnp.concatenate([cache_win[l][:, 1:], win_new], axis=1)[None]
    conv_prompt = glu[:, sp - (CONV_W - 1):][None]
    conv_sample = jnp.concatenate([state_conv[l][:, 1:], glu_new], axis=1)[None]
    return (y_prompt, y_sample.reshape(bs, 1, d), kv_prompt, kv_sample, win_prompt, win_sample,
            conv_prompt, conv_sample)
```

```python
import functools

import jax
import jax.numpy as jnp
from jax import lax
from jax.experimental import pallas as pl
from jax.experimental.pallas import tpu as pltpu

F32 = jnp.float32
BF16 = jnp.bfloat16
I32 = jnp.int32
MXU_DTYPE = jnp.bfloat16

N_HEADS = 8
KV_HEADS = 2
Q_PER_KV = N_HEADS // KV_HEADS
HEAD_DIM = 64
CMP_BLOCK = 32
CMP_STRIDE = 16
CMP_HID = 2 * HEAD_DIM
SEL_BLOCK = 64
TOP_N = 16
WINDOW = 512
CONV_W = 31
N_EXPERTS = 256
N_GROUPS = 8
TOPK_GROUPS = 4
TOP_K = 8
ROUTED_SCALE = 2.5
EPS = 1e-6
NEG_INF = -1e30
FORCE_SCORE = 1e4
ATTN_SCALE = HEAD_DIM ** -0.5

LANES = 128
SUBLANES = 8
VMEM_LIMIT = 56 * 1024 * 1024


def _cparams(n_axes):
    return pltpu.CompilerParams(
        dimension_semantics=("arbitrary",) * n_axes, vmem_limit_bytes=VMEM_LIMIT)


def _mm(a, b):
    return jnp.dot(a.astype(MXU_DTYPE), b.astype(MXU_DTYPE), preferred_element_type=F32)


def _mm_nt(a, b):
    return lax.dot_general(a.astype(MXU_DTYPE), b.astype(MXU_DTYPE),
                           (((1,), (1,)), ((), ())), preferred_element_type=F32)


def _mm_tn(a, b):
    return lax.dot_general(a.astype(MXU_DTYPE), b.astype(MXU_DTYPE),
                           (((0,), (0,)), ((), ())), preferred_element_type=F32)


def _sigmoid(x):
    return 1.0 / (1.0 + jnp.exp(-x))


def _silu(x):
    return x * _sigmoid(x)


def _rms(x, g):
    return x * lax.rsqrt(jnp.mean(x * x, axis=-1, keepdims=True) + EPS) * g


def _slope(head):
    return 2.0 ** (-8.0 * (head + 1) / N_HEADS)


def _ada_kernel(c_ref, w_ref, b_ref, o_ref):
    o_ref[...] = _mm(_silu(c_ref[...]), w_ref[...]) + b_ref[...]


def _ada_call(c, w_ada, b_ada):
    n, d = c.shape
    n_out = w_ada.shape[1]
    tn = n_out // 6
    return pl.pallas_call(
        _ada_kernel,
        grid=(n_out // tn,),
        in_specs=[pl.BlockSpec((n, d), lambda j: (0, 0)),
                  pl.BlockSpec((d, tn), lambda j: (0, j)),
                  pl.BlockSpec((1, tn), lambda j: (0, j))],
        out_specs=pl.BlockSpec((n, tn), lambda j: (0, j)),
        out_shape=jax.ShapeDtypeStruct((n, n_out), F32),
        compiler_params=_cparams(1),
        name="ada_mod",
    )(c, w_ada, b_ada.reshape(1, n_out))


def _inproj_kernel(x_ref, sc_ref, sh_ref, g_ref, w_ref, q_ref, kv_ref, glu_ref, gate_ref,
                   *, n_q, n_kv, c_conv):
    x = x_ref[0]
    h = _rms(x, g_ref[...]) * (1.0 + sc_ref[0]) + sh_ref[0]
    y = _mm(h, w_ref[...])
    o = 0
    q_ref[0] = y[:, o:o + n_q].astype(q_ref.dtype)
    o += n_q
    kv_ref[0] = y[:, o:o + n_kv]
    o += n_kv
    u_a = y[:, o:o + c_conv]
    u_g = y[:, o + c_conv:o + 2 * c_conv]
    glu_ref[0] = u_a * _sigmoid(u_g)
    o += 2 * c_conv
    gate_ref[0] = _sigmoid(y[:, o:o + LANES])


def _mod_spec(mod, tm, d):
    if mod.shape[1] == 1:
        return pl.BlockSpec((1, 1, d), lambda b, i: (b, 0, 0))
    return pl.BlockSpec((1, tm, d), lambda b, i: (b, i, 0))


def _inproj_call(x, sc1, sh1, g1, w_cat, tm):
    bx, tx, d = x.shape
    n_q = N_HEADS * HEAD_DIM
    n_kv = 6 * KV_HEADS * HEAD_DIM
    c_conv = d // 2
    n_cat = w_cat.shape[1]
    kern = functools.partial(_inproj_kernel, n_q=n_q, n_kv=n_kv, c_conv=c_conv)
    row = lambda w: pl.BlockSpec((1, tm, w), lambda b, i: (b, i, 0))
    return pl.pallas_call(
        kern,
        grid=(bx, tx // tm),
        in_specs=[row(d), _mod_spec(sc1, tm, d), _mod_spec(sh1, tm, d),
                  pl.BlockSpec((1, d), lambda b, i: (0, 0)),
                  pl.BlockSpec((d, n_cat), lambda b, i: (0, 0))],
        out_specs=[row(n_q), row(n_kv), row(c_conv), row(LANES)],
        out_shape=[jax.ShapeDtypeStruct((bx, tx, n_q), MXU_DTYPE),
                   jax.ShapeDtypeStruct((bx, tx, n_kv), F32),
                   jax.ShapeDtypeStruct((bx, tx, c_conv), F32),
                   jax.ShapeDtypeStruct((bx, tx, LANES), F32)],
        compiler_params=_cparams(2),
        name="in_proj",
    )(x, sc1, sh1, g1.reshape(1, d), w_cat)


def _cmp_first_layer(k_rows, v_rows, w1bd_ref, n_chunks):
    acc = None
    for l in range(CMP_STRIDE):
        rows_l = pl.ds(l, n_chunks, stride=CMP_STRIDE)
        x_l = jnp.concatenate([k_rows[rows_l, :], v_rows[rows_l, :]], axis=1)
        part = _mm(x_l, w1bd_ref[l])
        acc = part if acc is None else acc + part
    return acc


def _cmp_second_layer(first, second_next, pe_ref, w1pe_ref, w2bd_ref):
    pe_term = _mm(pe_ref[...], w1pe_ref[...])[0:1]
    return _mm(_silu(first + second_next + pe_term), w2bd_ref[...])


def _compress_prompt_kernel(k_ref, v_ref, w1bd_ref, pe_ref, w1pe_ref, w2bd_ref, o_ref, fs_ref,
                            *, n_chunks):
    half = 4 * CMP_HID
    fs_ref[pl.ds(0, n_chunks), :] = _cmp_first_layer(k_ref.at[0], v_ref.at[0], w1bd_ref, n_chunks)
    fs_ref[pl.ds(n_chunks, SUBLANES), :] = jnp.zeros((SUBLANES, 2 * half), F32)
    first = fs_ref[pl.ds(0, n_chunks), pl.ds(0, half)]
    second_next = fs_ref[pl.ds(1, n_chunks), pl.ds(half, half)]
    o_ref[0] = _cmp_second_layer(first, second_next, pe_ref, w1pe_ref, w2bd_ref)


def _compress_prompt_call(kvw, cw):
    b, t, _ = kvw.shape
    n_chunks = t // CMP_STRIDE
    wid = 4 * HEAD_DIM
    kern = functools.partial(_compress_prompt_kernel, n_chunks=n_chunks)
    full = lambda a: pl.BlockSpec(a.shape, lambda i: (0,) * a.ndim)
    return pl.pallas_call(
        kern,
        grid=(b,),
        in_specs=[pl.BlockSpec((1, t, LANES), lambda i: (i, 0, 0)),
                  pl.BlockSpec((1, t, LANES), lambda i: (i, 0, 1)),
                  full(cw["w1bd"]), full(cw["pe"]), full(cw["w1pe"]), full(cw["w2bd"])],
        out_specs=pl.BlockSpec((1, n_chunks, wid), lambda i: (i, 0, 0)),
        out_shape=jax.ShapeDtypeStruct((b, n_chunks, wid), F32),
        scratch_shapes=[pltpu.VMEM((n_chunks + SUBLANES, 8 * CMP_HID), F32)],
        compiler_params=_cparams(1),
        name="compress_prompt",
    )(kvw, kvw, cw["w1bd"], cw["pe"], cw["w1pe"], cw["w2bd"])


def _compress_weights(cmp_pe, cmp_w1, cmp_w2):
    hd, hid = HEAD_DIM, CMP_HID
    slab_kv = (0, 0, 1, 1)
    w1bd = jnp.zeros((CMP_STRIDE, 4 * hd, 8 * hid), F32)
    w2bd = jnp.zeros((4 * hid, 4 * hd), F32)
    for s, kv in enumerate(slab_kv):
        w1bd = w1bd.at[:, s * hd:(s + 1) * hd, s * hid:(s + 1) * hid].set(cmp_w1[kv, :CMP_STRIDE])
        w1bd = w1bd.at[:, s * hd:(s + 1) * hd, (4 + s) * hid:(5 + s) * hid].set(cmp_w1[kv, CMP_STRIDE:])
        w2bd = w2bd.at[s * hid:(s + 1) * hid, s * hd:(s + 1) * hd].set(cmp_w2[kv])
    pe = jnp.broadcast_to(cmp_pe.reshape(1, -1), (SUBLANES, 2 * CMP_BLOCK * hd))
    w1pe = jnp.zeros((2 * CMP_BLOCK * hd, 4 * hid), F32)
    n_flat = CMP_BLOCK * hd
    for s, kv in enumerate(slab_kv):
        w1pe = w1pe.at[kv * n_flat:(kv + 1) * n_flat, s * hid:(s + 1) * hid].set(
            cmp_w1[kv].reshape(n_flat, hid))
    return {"w1bd": w1bd.astype(MXU_DTYPE), "w2bd": w2bd.astype(MXU_DTYPE),
            "pe": pe, "w1pe": w1pe.astype(MXU_DTYPE)}


def _softmax_cols(s, mask):
    m = jnp.max(jnp.where(mask, s, NEG_INF), axis=0, keepdims=True)
    p = jnp.where(mask, jnp.exp(s - m), 0.0)
    l = jnp.sum(p, axis=0, keepdims=True)
    return p / jnp.where(l > 0.0, l, 1.0)


def _top_n_rows(imp, top_n):
    n = imp.shape[0]
    row_id = lax.broadcasted_iota(I32, imp.shape, 0)
    beaten = jnp.zeros(imp.shape, F32)
    for m in range(n):
        other = imp[m:m + 1, :]
        wins = (other > imp) | ((other == imp) & (row_id > m))
        beaten = beaten + jnp.where(wins, 1.0, 0.0)
    return beaten < float(top_n)


SEL_FEAT0 = HEAD_DIM + SUBLANES
MASK_BIG = -2.0 ** 100


def _attn_prompt_kernel(q_ref, ksv_ref, kwv_ref, kc_ref, gate_ref, o_ref,
                        kaug, vsel, kwaug, vwin, psum_ref, *, tq, kc_len, t_len):
    qt = pl.program_id(1)
    q0 = qt * tq
    hd = HEAD_DIM
    n_cmp_rows = kc_ref.shape[1]
    n_blk = t_len // SEL_BLOCK
    top_n = min(TOP_N, n_blk)
    wk = min(WINDOW + tq, t_len)

    @pl.when(qt == 0)
    def _():
        lane = lax.broadcasted_iota(I32, (t_len, LANES), 1)
        pos = lax.broadcasted_iota(I32, (t_len, LANES), 0)
        pos_hi = lax.div(pos, SEL_BLOCK)
        alibi = jnp.where(lane < hd + 2, 1.0,
                          jnp.where(lane == hd + 2, pos_hi.astype(F32),
                                    jnp.where(lane == hd + 3, lax.rem(pos, SEL_BLOCK).astype(F32), 0.0)))
        in_blk = jnp.where(lane - SEL_FEAT0 == pos_hi, 1.0, 0.0)
        for g in range(KV_HEADS):
            k_s = ksv_ref[0, :, 0:LANES]
            k_w = kwv_ref[0, :, 0:LANES]
            if g:
                k_s = pltpu.roll(k_s, LANES - g * hd, 1)
                k_w = pltpu.roll(k_w, LANES - g * hd, 1)
            kaug[g] = jnp.where(lane < hd, k_s, alibi + in_blk).astype(kaug.dtype)
            kwaug[g] = jnp.where(lane < hd, k_w, alibi).astype(kwaug.dtype)
            v_s = ksv_ref[0, :, LANES:2 * LANES]
            v_w = kwv_ref[0, :, LANES:2 * LANES]
            if g:
                v_s = pltpu.roll(v_s, LANES - g * hd, 1)
                v_w = pltpu.roll(v_w, LANES - g * hd, 1)
            ones_lane = jnp.where(lane == hd, 1.0, 0.0)
            vsel[g] = jnp.where(lane < hd, v_s, ones_lane).astype(vsel.dtype)
            vwin[g] = jnp.where(lane < hd, v_w, ones_lane).astype(vwin.dtype)
        psum_ref[...] = jnp.zeros(psum_ref.shape, F32)

    q_blk = q_ref[0]
    gates = gate_ref[0]
    rows = Q_PER_KV * tq
    row_t = q0 + lax.rem(lax.broadcasted_iota(I32, (rows, 1), 0), tq)
    lane_q = lax.broadcasted_iota(I32, (tq, LANES), 1)
    t_q = q0 + lax.broadcasted_iota(I32, (tq, LANES), 0)
    t_hi = (lax.div(t_q, SEL_BLOCK) * SEL_BLOCK).astype(F32)
    t_lo = lax.rem(t_q, SEL_BLOCK).astype(F32)
    w0 = pl.multiple_of(jnp.maximum(q0 + tq - wk, 0), tq)
    dist_w = row_t - (w0 + lax.broadcasted_iota(I32, (rows, wk), 1))
    band = jnp.where((dist_w >= 0) & (dist_w < WINDOW), 0.0, NEG_INF)
    out_heads = []
    for g in range(KV_HEADS):
        q_heads = [q_blk[:, (g * Q_PER_KV + r) * hd:(g * Q_PER_KV + r + 1) * hd]
                   for r in range(Q_PER_KV)]
        k_cmp = kc_ref[0, :, g * hd:(g + 1) * hd]
        v_cmp = kc_ref[0, :, (2 + g) * hd:(3 + g) * hd]
        t_lane = q0 + lax.broadcasted_iota(I32, (n_cmp_rows, tq), 1)
        c_pos = lax.broadcasted_iota(I32, (n_cmp_rows, tq), 0) * CMP_STRIDE + (CMP_BLOCK - 1)
        dist_c = t_lane - c_pos
        mask_c = dist_c >= 0
        dist_cf = dist_c.astype(F32)
        o_cmp = []
        p_sum = None
        for r in range(Q_PER_KV):
            s = _mm_nt(k_cmp, q_heads[r]) * ATTN_SCALE - _slope(g * Q_PER_KV + r) * dist_cf
            p = _softmax_cols(s, mask_c)
            o_cmp.append(_mm_tn(p, v_cmp))
            p_sum = p if p_sum is None else p_sum + p
        psum_ref[pl.ds(SUBLANES, n_cmp_rows), :] = p_sum
        per = SEL_BLOCK // CMP_STRIDE
        taps = [psum_ref[pl.ds(SUBLANES - 1 + k, n_blk, stride=per), :] for k in range(per + 1)]
        imp = 0.5 * taps[0] + 0.5 * taps[per]
        for k in range(1, per):
            imp = imp + taps[k]
        blk = lax.broadcasted_iota(I32, (n_blk, tq), 0)
        t_blk = q0 + lax.broadcasted_iota(I32, (n_blk, tq), 1)
        cur = lax.div(t_blk, SEL_BLOCK)
        forced = (blk == 0) | (blk == cur) | (blk == cur - 1)
        imp = jnp.where(forced, FORCE_SCORE, imp)
        imp = jnp.where(blk * SEL_BLOCK <= t_blk, imp, -jnp.inf)
        sel_neg = jnp.where(_top_n_rows(imp, top_n), 0.0, MASK_BIG)
        sel_lanes = jnp.transpose(jnp.concatenate(
            [jnp.zeros((SEL_FEAT0, tq), F32), sel_neg,
             jnp.zeros((LANES - SEL_FEAT0 - n_blk, tq), F32)], axis=0))
        q_aug = []
        for r in range(Q_PER_KV):
            head = g * Q_PER_KV + r
            slope = _slope(head)
            pair = q_blk[:, (head // 2) * LANES:(head // 2 + 1) * LANES].astype(F32)
            if head % 2:
                pair = pltpu.roll(pair, LANES - hd, 1)
            feats = jnp.where(lane_q == hd, -slope * t_hi,
                              jnp.where(lane_q == hd + 1, -slope * t_lo,
                                        jnp.where(lane_q == hd + 2, slope * SEL_BLOCK,
                                                  jnp.where(lane_q == hd + 3, slope, sel_lanes))))
            q_aug.append(jnp.where(lane_q < hd, pair * ATTN_SCALE, feats).astype(MXU_DTYPE))
        q_aug = jnp.concatenate(q_aug, axis=0)

        def sel_chunk(j, carry, causal):
            m_run, acc = carry
            k0 = pl.multiple_of(j * kc_len, kc_len)
            s = _mm_nt(q_aug, kaug[g, pl.ds(k0, kc_len), :])
            if causal:
                k_pos = k0 + lax.broadcasted_iota(I32, (rows, kc_len), 1)
                s = jnp.where(k_pos <= row_t, s, NEG_INF)
            m_new = jnp.maximum(m_run, jnp.max(s, axis=1, keepdims=True))
            p = jnp.exp((s - m_new).astype(MXU_DTYPE))
            acc_new = jnp.exp(m_run - m_new) * acc + _mm(p, vsel[g, pl.ds(k0, kc_len), :])
            return m_new, acc_new

        n_full = lax.div(q0, kc_len)
        init = (jnp.full((rows, 1), NEG_INF, F32), jnp.zeros((rows, LANES), F32))
        carry = lax.fori_loop(0, n_full, functools.partial(sel_chunk, causal=False), init)
        _, acc_sel = sel_chunk(n_full, carry, causal=True)
        o_sel = acc_sel[:, :hd] / acc_sel[:, hd:hd + 1]
        s = _mm_nt(q_aug, kwaug[g, pl.ds(w0, wk), :]) + band
        p = jnp.exp((s - jnp.max(s, axis=1, keepdims=True)).astype(MXU_DTYPE))
        acc_win = _mm(p, vwin[g, pl.ds(w0, wk), :])
        o_win = acc_win[:, :hd] / acc_win[:, hd:hd + 1]
        for r in range(Q_PER_KV):
            c = (g * Q_PER_KV + r) * 3
            out_heads.append(gates[:, c:c + 1] * o_cmp[r]
                             + gates[:, c + 1:c + 2] * o_sel[r * tq:(r + 1) * tq]
                             + gates[:, c + 2:c + 3] * o_win[r * tq:(r + 1) * tq])
    o_ref[0] = jnp.concatenate(out_heads, axis=1).astype(o_ref.dtype)


def _attn_prompt_call(q, kvw, kcvc, gates, tq, kc_len):
    b, t, n_q = q.shape
    wid = 4 * HEAD_DIM
    kern = functools.partial(_attn_prompt_kernel, tq=tq, kc_len=kc_len, t_len=t)
    assert SEL_FEAT0 + t // SEL_BLOCK <= LANES and kc_len % tq == 0
    k_scratch = pltpu.VMEM((KV_HEADS, t, LANES), MXU_DTYPE)
    v_scratch = pltpu.VMEM((KV_HEADS, t, LANES), MXU_DTYPE)
    n_cmp_rows = kcvc.shape[1]
    return pl.pallas_call(
        kern,
        grid=(b, t // tq),
        in_specs=[pl.BlockSpec((1, tq, n_q), lambda i, j: (i, j, 0)),
                  pl.BlockSpec((1, t, wid), lambda i, j: (i, 0, 1)),
                  pl.BlockSpec((1, t, wid), lambda i, j: (i, 0, 2)),
                  pl.BlockSpec((1, n_cmp_rows, wid), lambda i, j: (i, 0, 0)),
                  pl.BlockSpec((1, tq, LANES), lambda i, j: (i, j, 0))],
        out_specs=pl.BlockSpec((1, tq, n_q), lambda i, j: (i, j, 0)),
        out_shape=jax.ShapeDtypeStruct((b, t, n_q), MXU_DTYPE),
        scratch_shapes=[k_scratch, v_scratch, k_scratch, v_scratch,
                        pltpu.VMEM((n_cmp_rows + 2 * SUBLANES, tq), F32)],
        compiler_params=_cparams(2),
        name="attn_prompt",
    )(q, kvw, kvw, kcvc, gates)


CONV_PAD = 32


def _ln_silu(z, lg, lb):
    mu = jnp.mean(z, axis=-1, keepdims=True)
    zc = z - mu
    var = jnp.mean(zc * zc, axis=-1, keepdims=True)
    return _silu(zc * lax.rsqrt(var + EPS) * lg + lb)


def _conv_prompt_kernel(glu_ref, cw_ref, cb_ref, lg_ref, lb_ref, z_ref, full_ref, *, t_len, tt):
    full_ref[pl.ds(0, CONV_PAD), :] = jnp.zeros((CONV_PAD, full_ref.shape[1]), F32)
    full_ref[pl.ds(CONV_PAD, t_len), :] = glu_ref[0]
    first = CONV_PAD - (CONV_W - 1)

    def tile(i, carry):
        r0 = pl.multiple_of(i * tt, tt)
        acc = jnp.zeros((tt, full_ref.shape[1]), F32) + cb_ref[...]
        win = full_ref[pl.ds(r0, tt + CONV_PAD), :]
        for phase in range(SUBLANES):
            offs = [o for o in range(first, first + CONV_W) if o % SUBLANES == phase]
            if not offs:
                continue
            shifted = win[phase:max(offs) + tt]
            for o in offs:
                acc = acc + shifted[o - phase:o - phase + tt] * cw_ref[o - first:o - first + 1, :]
        z_ref[0, pl.ds(r0, tt), :] = _ln_silu(acc, lg_ref[...], lb_ref[...]).astype(z_ref.dtype)
        return carry

    lax.fori_loop(0, t_len // tt, tile, 0)


def _conv_prompt_call(glu, conv_w, conv_b, ln_g, ln_b, tt):
    b, t, c = glu.shape
    kern = functools.partial(_conv_prompt_kernel, t_len=t, tt=tt)
    vec = pl.BlockSpec((1, c), lambda i: (0, 0))
    return pl.pallas_call(
        kern,
        grid=(b,),
        in_specs=[pl.BlockSpec((1, t, c), lambda i: (i, 0, 0)),
                  pl.BlockSpec((CONV_PAD, c), lambda i: (0, 0)), vec, vec, vec],
        out_specs=pl.BlockSpec((1, t, c), lambda i: (i, 0, 0)),
        out_shape=jax.ShapeDtypeStruct((b, t, c), MXU_DTYPE),
        scratch_shapes=[pltpu.VMEM((CONV_PAD + t, c), F32)],
        compiler_params=_cparams(1),
        name="conv_prompt",
    )(glu, jnp.pad(conv_w, ((0, CONV_PAD - CONV_W), (0, 0))), conv_b.reshape(1, c),
      ln_g.reshape(1, c), ln_b.reshape(1, c))


def _conv_sample_kernel(hist_ref, new_ref, cw_ref, cb_ref, lg_ref, lb_ref, z_ref):
    z = new_ref[...] * cw_ref[CONV_W - 1:CONV_W, :] + cb_ref[...]
    for w in range(CONV_W - 1):
        z = z + hist_ref[w] * cw_ref[w:w + 1, :]
    z_ref[...] = _ln_silu(z, lg_ref[...], lb_ref[...]).astype(z_ref.dtype)


def _conv_sample_call(hist, glu_new, conv_w, conv_b, ln_g, ln_b):
    bs, c = glu_new.shape
    return pl.pallas_call(
        _conv_sample_kernel,
        out_shape=jax.ShapeDtypeStruct((bs, c), MXU_DTYPE),
        name="conv_sample",
    )(hist, glu_new, jnp.pad(conv_w, ((0, CONV_PAD - CONV_W), (0, 0))), conv_b.reshape(1, c),
      ln_g.reshape(1, c), ln_b.reshape(1, c))


def _route_tile(h2, rwt_ref, rb_ref, cnt_ref):
    n_e = N_EXPERTS
    per_grp = n_e // N_GROUPS
    tm = h2.shape[0]
    aff = _sigmoid(_mm_nt(rwt_ref[...], h2))
    biased = aff + rb_ref[...]
    neg = -jnp.inf
    g_rows = []
    for g in range(N_GROUPS):
        v = biased[g * per_grp:(g + 1) * per_grp]
        m1 = jnp.max(v, axis=0, keepdims=True)
        is_m1 = v == m1
        n_m1 = jnp.sum(jnp.where(is_m1, 1.0, 0.0), axis=0, keepdims=True)
        m2 = jnp.max(jnp.where(is_m1, neg, v), axis=0, keepdims=True)
        g_rows.append(m1 + jnp.where(n_m1 >= 2.0, m1, m2))
    g_keep = _top_n_rows(jnp.concatenate(g_rows, axis=0), TOPK_GROUPS)
    cur = jnp.concatenate(
        [jnp.where(g_keep[g:g + 1], biased[g * per_grp:(g + 1) * per_grp], neg)
         for g in range(N_GROUPS)], axis=0)
    row_id = lax.broadcasted_iota(I32, (n_e, tm), 0).astype(F32)
    ids, wts, hots = [], [], []
    for _ in range(TOP_K):
        m = jnp.max(cur, axis=0, keepdims=True)
        idx = jnp.min(jnp.where(cur == m, row_id, float(n_e)), axis=0, keepdims=True)
        hot = row_id == idx
        ids.append(idx)
        wts.append(jnp.sum(jnp.where(hot, aff, 0.0), axis=0, keepdims=True))
        hots.append(hot)
        cur = jnp.where(hot, neg, cur)
    w_sum = wts[0]
    for w in wts[1:]:
        w_sum = w_sum + w
    wts = [w / w_sum * ROUTED_SCALE for w in wts]
    hot_all = jnp.where(hots[0], 1.0, 0.0)
    for hot in hots[1:]:
        hot_all = hot_all + jnp.where(hot, 1.0, 0.0)
    earlier = jnp.where(lax.broadcasted_iota(I32, (tm, tm), 0) < lax.broadcasted_iota(I32, (tm, tm), 1),
                        1.0, 0.0)
    before = cnt_ref[:, 0:1] + jnp.dot(hot_all.astype(BF16), earlier.astype(BF16),
                                       preferred_element_type=F32)
    ranks = [jnp.sum(jnp.where(hot, before, 0.0), axis=0, keepdims=True) for hot in hots]
    cnt_ref[...] = cnt_ref[...] + jnp.sum(hot_all, axis=1, keepdims=True)
    cat = lambda rows: jnp.concatenate(rows, axis=0)
    return cat(ids).astype(I32), cat(wts), cat(ranks).astype(I32)


def _merge_kernel(x_ref, oa_ref, z_ref, sc1_ref, sh1_ref, gt1_ref, sc2_ref, sh2_ref,
                  g1_ref, wm_ref, wa_ref, wb_ref, wo_ref, g2_ref, rwt_ref, rb_ref,
                  s13_ref, s2_ref,
                  x1_ref, h2_ref, ysh_ref, eid_ref, ew_ref, rk_ref, cnt_out_ref, cnt_ref, *, d_exp):
    first_step = (pl.program_id(0) == 0) & (pl.program_id(1) == 0)

    @pl.when(first_step)
    def _():
        cnt_ref[...] = jnp.zeros(cnt_ref.shape, F32)

    x = x_ref[0]
    d = x.shape[1]
    tm = x.shape[0]
    h = _rms(x, g1_ref[...]) * (1.0 + sc1_ref[0]) + sh1_ref[0]
    g_mrg = _sigmoid(_mm(h, wm_ref[...]))
    mixed = g_mrg[:, :d] * _mm(oa_ref[0], wa_ref[...]) + g_mrg[:, d:] * _mm(z_ref[0], wb_ref[...])
    x1 = x + gt1_ref[0] * _mm(mixed, wo_ref[...])
    x1_ref[0] = x1
    h2 = _rms(x1, g2_ref[...]) * (1.0 + sc2_ref[0]) + sh2_ref[0]
    for s in range(d // LANES):
        h2_ref[pl.ds(s, tm, stride=d // LANES), :] = h2[:, s * LANES:(s + 1) * LANES]
    hs = _mm(h2, s13_ref[...])
    ysh_ref[0] = _mm(_silu(hs[:, :d_exp]) * hs[:, d_exp:], s2_ref[...])
    ids, wts, ranks = _route_tile(h2, rwt_ref, rb_ref, cnt_ref)
    eid_ref[...] = ids
    ew_ref[...] = wts
    rk_ref[...] = ranks
    cnt_out_ref[...] = cnt_ref[...]


def _merge_call(x, o_attn, z, mods, mw, tm):
    bx, tx, d = x.shape
    n = bx * tx
    nt = tx // tm
    d_exp = mw["s2"].shape[0]
    kern = functools.partial(_merge_kernel, d_exp=d_exp)
    row = lambda w: pl.BlockSpec((1, tm, w), lambda b, i: (b, i, 0))
    full = lambda a: pl.BlockSpec(a.shape, lambda b, i: (0,) * a.ndim)
    tok = pl.BlockSpec((TOP_K, tm), lambda b, i: (0, b * nt + i))
    wnames = ("g1", "wm", "wa", "wb", "wo", "g2", "rwt", "rb", "s13", "s2")
    return pl.pallas_call(
        kern,
        grid=(bx, nt),
        in_specs=[row(d), row(o_attn.shape[2]), row(z.shape[2])]
                 + [_mod_spec(m, tm, d) for m in mods]
                 + [full(mw[k]) for k in wnames],
        out_specs=[row(d),
                   pl.BlockSpec((tm * (d // LANES), LANES), lambda b, i: (b * nt + i, 0)),
                   row(d), tok, tok, tok,
                   pl.BlockSpec((N_EXPERTS, LANES), lambda b, i: (0, 0))],
        out_shape=[jax.ShapeDtypeStruct((bx, tx, d), F32),
                   jax.ShapeDtypeStruct((n * (d // LANES), LANES), F32),
                   jax.ShapeDtypeStruct((bx, tx, d), F32),
                   jax.ShapeDtypeStruct((TOP_K, n), I32),
                   jax.ShapeDtypeStruct((TOP_K, n), F32),
                   jax.ShapeDtypeStruct((TOP_K, n), I32),
                   jax.ShapeDtypeStruct((N_EXPERTS, LANES), F32)],
        scratch_shapes=[pltpu.VMEM((N_EXPERTS, LANES), F32)],
        compiler_params=_cparams(2),
        name="merge_route",
    )(x, o_attn, z, *mods, *[mw[k] for k in wnames])


def _start_tile_copy(src_hbm, src_row, buf, slot, dst_row, sem, n_sub):
    pltpu.make_async_copy(src_hbm.at[pl.ds(pl.multiple_of(src_row, n_sub), n_sub), :],
                          buf.at[slot, pl.ds(pl.multiple_of(dst_row, n_sub), n_sub), :],
                          sem.at[slot]).start()


def _wait_slot(buf, sem, slot):
    pltpu.make_async_copy(buf.at[slot], buf.at[slot], sem.at[slot]).wait()


def _moe_plan(counts, n_asg, rb):
    n_blk = -(-(n_asg + N_EXPERTS * (rb - 1)) // rb)
    padded = (counts + rb - 1) // rb * rb
    pad_end = jnp.cumsum(padded)
    start = pad_end - padded
    blk_row = jnp.arange(n_blk, dtype=I32)[:, None] * rb
    blk_e = jnp.minimum(jnp.sum((pad_end[None, :] <= blk_row).astype(I32), axis=1), N_EXPERTS - 1)
    n_used = (pad_end[-1] // rb).astype(I32).reshape(1)
    return start.astype(I32), padded.astype(I32), blk_e.astype(I32), n_used, n_blk


def _dispatch_kernel(start_ref, cnt_ref, pad_ref, n_used_ref,
                     eid_ref, rk_ref, eid_s_ref, rk_s_ref, h2p_ref, h2s_ref,
                     xs_hbm, dst_ref, dst_s_ref, zeros, sem,
                     *, tmd, n_p_steps, bs, n_sub, rb, n_blk):
    i = pl.program_id(0)

    def scatter_tokens(src_ref, e_ref, r_ref, d_ref, n_tok):
        def issue(t, carry):
            src = src_ref.at[pl.ds(pl.multiple_of(t * n_sub, n_sub), n_sub), :]
            for k in range(TOP_K):
                dst_row = (start_ref[e_ref[k, t]] + r_ref[k, t]) * n_sub
                d_ref[k, t] = dst_row
                pltpu.make_async_copy(
                    src, xs_hbm.at[pl.ds(pl.multiple_of(dst_row, n_sub), n_sub), :], sem).start()
            return carry
        lax.fori_loop(0, n_tok, issue, 0)
        done = xs_hbm.at[pl.ds(0, n_tok * TOP_K * n_sub), :]
        pltpu.make_async_copy(done, done, sem).wait()

    @pl.when(i < n_p_steps)
    def _():
        scatter_tokens(h2p_ref, eid_ref, rk_ref, dst_ref, tmd)

    @pl.when(i == n_p_steps)
    def _():
        scatter_tokens(h2s_ref, eid_s_ref, rk_s_ref, dst_s_ref, bs)
        zeros[...] = jnp.zeros(zeros.shape, F32)

        def zero_rows(first_row, n_rows):
            pltpu.make_async_copy(
                zeros.at[pl.ds(0, n_rows * n_sub), :],
                xs_hbm.at[pl.ds(pl.multiple_of(first_row * n_sub, n_sub), n_rows * n_sub), :],
                sem).start()

        def pad_expert(e, carry):
            n_pad = pad_ref[e] - cnt_ref[e]
            row = start_ref[e] + cnt_ref[e]
            piece = rb // 2
            while piece >= 1:
                take = (n_pad & piece) != 0

                @pl.when(take)
                def _(row=row, piece=piece):
                    zero_rows(row, piece)

                row = row + jnp.where(take, piece, 0)
                piece //= 2
            return carry

        lax.fori_loop(0, N_EXPERTS, pad_expert, 0)

        def pad_block(blk, carry):
            zero_rows(blk * rb, rb)
            return carry

        lax.fori_loop(n_used_ref[0], n_blk, pad_block, 0)
        n_zero = n_blk * rb - (n_p_steps * tmd + bs) * TOP_K
        done = xs_hbm.at[pl.ds(0, n_zero * n_sub), :]
        pltpu.make_async_copy(done, done, sem).wait()


def _dispatch_call(start, counts, padded, n_used, eid_p, rank_p, eid_s, rank_s, h2_p, h2_s,
                   n_blk, rb, tmd):
    n_p = eid_p.shape[1]
    bs = eid_s.shape[1]
    n_sub = h2_p.shape[0] // n_p
    n_p_steps = n_p // tmd
    kern = functools.partial(_dispatch_kernel, tmd=tmd, n_p_steps=n_p_steps, bs=bs, n_sub=n_sub,
                             rb=rb, n_blk=n_blk)
    last = n_p_steps - 1
    tile = pl.BlockSpec((TOP_K, tmd), lambda i, *_: (0, jnp.minimum(i, last)), memory_space=pltpu.SMEM)
    whole = pl.BlockSpec((TOP_K, bs), lambda i, *_: (0, 0), memory_space=pltpu.SMEM)
    grid_spec = pltpu.PrefetchScalarGridSpec(
        num_scalar_prefetch=4,
        grid=(n_p_steps + 1,),
        in_specs=[tile, tile, whole, whole,
                  pl.BlockSpec((tmd * n_sub, LANES), lambda i, *_: (jnp.minimum(i, last), 0)),
                  pl.BlockSpec((bs * n_sub, LANES), lambda i, *_: (0, 0))],
        out_specs=[pl.BlockSpec(memory_space=pl.ANY), tile, whole],
        scratch_shapes=[pltpu.VMEM((rb * n_sub, LANES), F32), pltpu.SemaphoreType.DMA],
    )
    return pl.pallas_call(
        kern,
        grid_spec=grid_spec,
        out_shape=[jax.ShapeDtypeStruct((n_blk * rb * n_sub, LANES), F32),
                   jax.ShapeDtypeStruct((TOP_K, n_p), I32),
                   jax.ShapeDtypeStruct((TOP_K, bs), I32)],
        compiler_params=_cparams(1),
        name="moe_dispatch",
    )(start, counts, padded, n_used, eid_p, rank_p, eid_s, rank_s, h2_p, h2_s)


def _experts_kernel(blk_e_ref, n_used_ref, x_ref, w1_ref, w3_ref, w2_ref, y_ref, *, rb, n_sub):
    i = pl.program_id(0)

    @pl.when(i < n_used_ref[0])
    def _():
        x = jnp.concatenate([x_ref[pl.ds(s, rb, stride=n_sub), :] for s in range(n_sub)], axis=1)
        hid = _silu(_mm(x, w1_ref[0])) * _mm(x, w3_ref[0])
        y = _mm(hid, w2_ref[0])
        for s in range(n_sub):
            y_ref[pl.ds(s, rb, stride=n_sub), :] = y[:, s * LANES:(s + 1) * LANES]

    @pl.when(i >= n_used_ref[0])
    def _():
        y_ref[...] = jnp.zeros(y_ref.shape, F32)


def _experts_call(x_rows, blk_e, n_used, w1, w3, w2, rb):
    _, d, d_exp = w1.shape
    n_sub = d // LANES
    n_blk = x_rows.shape[0] // (rb * n_sub)
    kern = functools.partial(_experts_kernel, rb=rb, n_sub=n_sub)
    grid_spec = pltpu.PrefetchScalarGridSpec(
        num_scalar_prefetch=2,
        grid=(n_blk,),
        in_specs=[pl.BlockSpec((rb * n_sub, LANES), lambda i, be, nu: (jnp.minimum(i, nu[0] - 1), 0)),
                  pl.BlockSpec((1, d, d_exp), lambda i, be, nu: (be[i], 0, 0)),
                  pl.BlockSpec((1, d, d_exp), lambda i, be, nu: (be[i], 0, 0)),
                  pl.BlockSpec((1, d_exp, d), lambda i, be, nu: (be[i], 0, 0))],
        out_specs=pl.BlockSpec((rb * n_sub, LANES), lambda i, be, nu: (i, 0)),
    )
    return pl.pallas_call(
        kern,
        grid_spec=grid_spec,
        out_shape=jax.ShapeDtypeStruct((n_blk * rb * n_sub, LANES), F32),
        compiler_params=_cparams(1),
        name="routed_experts",
    )(blk_e, n_used, x_rows, w1, w3, w2)


def _combine_kernel(rows_ref, rows_next_ref, y_hbm, ew_ref, x1_ref,
                    ysh_ref, gt2_ref, gf_ref, o_ref, buf, sem, *, tm, n_sub, nt, n_steps):
    step = pl.program_id(0) * nt + pl.program_id(1)
    slot = lax.rem(step, 2)

    def gather(r_ref, to_slot):
        def issue(t, carry):
            for k in range(TOP_K):
                _start_tile_copy(y_hbm, r_ref[k, t], buf, to_slot, (k * tm + t) * n_sub, sem, n_sub)
            return carry
        lax.fori_loop(0, tm, issue, 0)

    @pl.when(step == 0)
    def _():
        gather(rows_ref, 0)

    @pl.when(step + 1 < n_steps)
    def _():
        gather(rows_next_ref, 1 - slot)

    _wait_slot(buf, sem, slot)
    ew = ew_ref[...]
    cols = []
    for s in range(n_sub):
        acc = None
        for k in range(TOP_K):
            term = ew[:, k:k + 1] * buf[slot, pl.ds(k * tm * n_sub + s, tm, stride=n_sub), :]
            acc = term if acc is None else acc + term
        cols.append(acc)
    y_routed = jnp.concatenate(cols, axis=1)
    x2 = x1_ref[0] + gt2_ref[0] * (y_routed + ysh_ref[0])
    o_ref[0] = _rms(x2, gf_ref[...])


def _combine_call(dst_rows, y_rows, ew, x1, ysh, gt2, g_final, tm):
    bx, tx, d = x1.shape
    nt = tx // tm
    n_sub = d // LANES
    n_steps = bx * nt
    kern = functools.partial(_combine_kernel, tm=tm, n_sub=n_sub, nt=nt, n_steps=n_steps)
    row = pl.BlockSpec((1, tm, d), lambda b, i: (b, i, 0))
    cur = pl.BlockSpec((TOP_K, tm), lambda b, i: (0, b * nt + i), memory_space=pltpu.SMEM)
    nxt = pl.BlockSpec((TOP_K, tm), lambda b, i: (0, jnp.minimum(b * nt + i + 1, n_steps - 1)),
                       memory_space=pltpu.SMEM)
    return pl.pallas_call(
        kern,
        grid=(bx, nt),
        in_specs=[cur, nxt,
                  pl.BlockSpec(memory_space=pl.ANY),
                  pl.BlockSpec((tm, TOP_K), lambda b, i: (b * nt + i, 0)),
                  row, row, _mod_spec(gt2, tm, d),
                  pl.BlockSpec((1, d), lambda b, i: (0, 0))],
        out_specs=row,
        out_shape=jax.ShapeDtypeStruct((bx, tx, d), F32),
        scratch_shapes=[pltpu.VMEM((2, TOP_K * tm * n_sub, LANES), F32),
                        pltpu.SemaphoreType.DMA((2,))],
        compiler_params=_cparams(2),
        name="combine_final",
    )(dst_rows, dst_rows, y_rows, ew, x1, ysh, gt2, g_final.reshape(1, d))


def _compress_sample_kernel(pt_ref, cache_hbm, w1bd_ref, fs_ref, raw, kbuf, vbuf, sem,
                            *, pp, page_rows, n_steps):
    step = pl.program_id(0) * pl.num_programs(1) + pl.program_id(1)
    slot = lax.rem(step, 2)

    def fetch(of_step, to_slot):
        def issue(p, carry):
            page = pt_ref[of_step * pp + p]
            pltpu.make_async_copy(cache_hbm.at[page, pl.ds(0, 2)], raw.at[to_slot, p],
                                  sem.at[to_slot]).start()
            return carry
        lax.fori_loop(0, pp, issue, 0)

    @pl.when(step == 0)
    def _():
        fetch(0, 0)

    @pl.when(step + 1 < n_steps)
    def _():
        fetch(step + 1, 1 - slot)

    _wait_slot(raw, sem, slot)

    def to_rows(p, carry):
        rows = pl.ds(pl.multiple_of(p * page_rows, page_rows), page_rows)
        kbuf[rows, :] = raw[slot, p, 0].reshape(LANES, page_rows).T
        vbuf[rows, :] = raw[slot, p, 1].reshape(LANES, page_rows).T
        return carry

    lax.fori_loop(0, pp, to_rows, 0)
    n_chunks = pp * page_rows // CMP_STRIDE
    fs_ref[0] = _cmp_first_layer(kbuf, vbuf, w1bd_ref, n_chunks)


def _compress_sample_call(page_table, cache5, w1bd, pp):
    bs, n_pages = page_table.shape
    _, _, n_g, hd, page_rows = cache5.shape
    assert n_g * hd == LANES
    n_chunks = pp * page_rows // CMP_STRIDE
    spb = n_pages // pp
    kern = functools.partial(_compress_sample_kernel, pp=pp, page_rows=page_rows, n_steps=bs * spb)
    buf = pltpu.VMEM((pp * page_rows, LANES), F32)
    grid_spec = pltpu.PrefetchScalarGridSpec(
        num_scalar_prefetch=1,
        grid=(bs, spb),
        in_specs=[pl.BlockSpec(memory_space=pl.ANY),
                  pl.BlockSpec(w1bd.shape, lambda b, i, pt: (0, 0, 0))],
        out_specs=pl.BlockSpec((1, n_chunks, 8 * CMP_HID), lambda b, i, pt: (b, i, 0)),
        scratch_shapes=[pltpu.VMEM((2, pp, 2, n_g, hd, page_rows), F32), buf, buf,
                        pltpu.SemaphoreType.DMA((2,))],
    )
    return pl.pallas_call(
        kern,
        grid_spec=grid_spec,
        out_shape=jax.ShapeDtypeStruct((bs, spb * n_chunks, 8 * CMP_HID), F32),
        compiler_params=_cparams(2),
        name="compress_sample",
    )(page_table.reshape(-1), cache5, w1bd)


def _head_slopes(n_rows):
    head = lax.broadcasted_iota(I32, (n_rows, 1), 0)
    slopes = jnp.zeros((n_rows, 1), F32)
    for h in range(N_HEADS):
        slopes = jnp.where(head == h, _slope(h), slopes)
    return slopes


def _attn_sample_a_kernel(q_ref, fs_ref, pe_ref, w1pe_ref, w2bd_ref, ocmp_ref, idx_ref,
                          *, t_pos, n_blk, n_blk_pad):
    hd = HEAD_DIM
    f = fs_ref[0]
    n_c = f.shape[0]
    half = 4 * CMP_HID
    second_next = pltpu.roll(f[:, half:], n_c - 1, 0)
    kcvc = _cmp_second_layer(f[:, :half], second_next, pe_ref, w1pe_ref, w2bd_ref)
    q = q_ref[0]
    c_idx = lax.broadcasted_iota(I32, (1, n_c), 1)
    dist_c = t_pos - (c_idx * CMP_STRIDE + (CMP_BLOCK - 1))
    mask_c = dist_c >= 0
    slopes = _head_slopes(N_HEADS)
    head_grp = lax.div(lax.broadcasted_iota(I32, (N_HEADS, 1), 0), Q_PER_KV)
    per = SEL_BLOCK // CMP_STRIDE
    c_row = lax.broadcasted_iota(I32, (n_c, n_blk_pad), 0)
    lo = lax.broadcasted_iota(I32, (n_c, n_blk_pad), 1) * per
    spread = jnp.where((c_row >= lo) & (c_row < lo + per - 1), 1.0, 0.0) \
        + jnp.where((c_row == lo - 1) | (c_row == lo + per - 1), 0.5, 0.0)
    blk_lane = lax.broadcasted_iota(I32, (1, n_blk_pad), 1)
    cur = t_pos // SEL_BLOCK
    forced = (blk_lane == 0) | (blk_lane == cur) | (blk_lane == cur - 1)
    in_range = (blk_lane * SEL_BLOCK <= t_pos) & (blk_lane < n_blk)
    n_sq = (n_blk_pad, n_blk_pad)
    sub_id = lax.broadcasted_iota(I32, n_sq, 0)
    lane_id = lax.broadcasted_iota(I32, n_sq, 1)
    top_n = min(TOP_N, n_blk)
    o_cmp = jnp.zeros((N_HEADS, hd), F32)
    idx_rows = []
    for g in range(KV_HEADS):
        k_cmp = kcvc[:, g * hd:(g + 1) * hd]
        v_cmp = kcvc[:, (2 + g) * hd:(3 + g) * hd]
        s = _mm_nt(q, k_cmp) * ATTN_SCALE - slopes * dist_c.astype(F32)
        m = jnp.max(jnp.where(mask_c, s, NEG_INF), axis=1, keepdims=True)
        p = jnp.where(mask_c, jnp.exp(s - m), 0.0)
        l = jnp.sum(p, axis=1, keepdims=True)
        p = p / jnp.where(l > 0.0, l, 1.0)
        o_cmp = jnp.where(head_grp == g, _mm(p, v_cmp), o_cmp)
        p_sum = jnp.sum(jnp.where(head_grp == g, p, 0.0), axis=0, keepdims=True)
        imp = jnp.dot(jnp.broadcast_to(p_sum, (SUBLANES, n_c)), spread,
                      precision=lax.Precision.HIGHEST, preferred_element_type=F32)[0:1]
        imp = jnp.where(forced, FORCE_SCORE, imp)
        imp = jnp.where(in_range, imp, -jnp.inf)
        imp_col = jnp.transpose(jnp.broadcast_to(imp, (LANES, n_blk_pad)))[:, 0:1]
        beats = (imp > imp_col) | ((imp == imp_col) & (lane_id < sub_id))
        rank_col = jnp.sum(jnp.where(beats, 1.0, 0.0), axis=1, keepdims=True)
        sel_col = jnp.where(rank_col < float(top_n), 1.0, 0.0)
        before = jnp.dot(jnp.where(lane_id < sub_id, 1.0, 0.0).astype(BF16),
                         jnp.broadcast_to(sel_col, (n_blk_pad, LANES)).astype(BF16),
                         preferred_element_type=F32)
        slot_lane = lax.broadcasted_iota(I32, (n_blk_pad, LANES), 1).astype(F32)
        blk_sub = lax.broadcasted_iota(I32, (n_blk_pad, LANES), 0).astype(F32)
        hit = (sel_col > 0.5) & (before == slot_lane)
        idx_rows.append(jnp.sum(jnp.where(hit, blk_sub, 0.0), axis=0, keepdims=True))
    ocmp_ref[0] = o_cmp
    pad = jnp.zeros((SUBLANES - KV_HEADS, LANES), F32)
    idx_ref[0] = jnp.concatenate(idx_rows + [pad], axis=0).astype(I32)


def _attn_sample_a_call(q_heads, fs, cw, t_pos, n_blk):
    bs = q_heads.shape[0]
    n_blk_pad = -(-n_blk // LANES) * LANES
    kern = functools.partial(_attn_sample_a_kernel, t_pos=t_pos, n_blk=n_blk, n_blk_pad=n_blk_pad)
    full = lambda a: pl.BlockSpec(a.shape, lambda b: (0,) * a.ndim)
    per_b = lambda a: pl.BlockSpec((1,) + a.shape[1:], lambda b: (b,) + (0,) * (a.ndim - 1))
    return pl.pallas_call(
        kern,
        grid=(bs,),
        in_specs=[per_b(q_heads), per_b(fs), full(cw["pe"]), full(cw["w1pe"]), full(cw["w2bd"])],
        out_specs=[pl.BlockSpec((1, N_HEADS, HEAD_DIM), lambda b: (b, 0, 0)),
                   pl.BlockSpec((1, SUBLANES, LANES), lambda b: (b, 0, 0))],
        out_shape=[jax.ShapeDtypeStruct((bs, N_HEADS, HEAD_DIM), F32),
                   jax.ShapeDtypeStruct((bs, SUBLANES, LANES), I32)],
        compiler_params=_cparams(1),
        name="attn_sample_select",
    )(q_heads, fs, cw["pe"], cw["w1pe"], cw["w2bd"])


def _attn_sample_b_kernel(sel_ref, pt_ref, q_ref, cache_hbm, kvn_ref, win_ref, gate_ref, ocmp_ref,
                          o_ref, kbuf, vbuf, sem,
                          *, t_pos, top_n, n_past_blk, n_pages, per_page, n_steps):
    b = pl.program_id(0)
    slot = lax.rem(b, 2)
    hd = HEAD_DIM
    page_rows = kbuf.shape[-1]

    def fetch(of_b, to_slot):
        for g in range(KV_HEADS):
            def issue(i, carry, g=g):
                blk = jnp.minimum(sel_ref[(of_b * KV_HEADS + g) * top_n + i], n_past_blk - 1)
                page = pt_ref[of_b * n_pages + blk // per_page]
                pltpu.make_async_copy(cache_hbm.at[page, 2, g], kbuf.at[to_slot, g, i],
                                      sem.at[to_slot]).start()
                pltpu.make_async_copy(cache_hbm.at[page, 3, g], vbuf.at[to_slot, g, i],
                                      sem.at[to_slot]).start()
                return carry
            lax.fori_loop(0, top_n, issue, 0)

    @pl.when(b == 0)
    def _():
        fetch(0, 0)

    @pl.when(b + 1 < n_steps)
    def _():
        fetch(b + 1, 1 - slot)

    _wait_slot(kbuf, sem, slot)
    _wait_slot(vbuf, sem, slot)
    q = q_ref[0]
    qf = q.astype(F32)
    slopes = _head_slopes(N_HEADS)
    kvn = kvn_ref[0]
    rnd = lambda a: a.astype(MXU_DTYPE).astype(F32)
    new_col = lambda c: rnd(kvn[:, c * hd:(c + 1) * hd])
    gates = gate_ref[0]
    head_grp = lax.div(lax.broadcasted_iota(I32, (N_HEADS, 1), 0), Q_PER_KV)
    lane = lax.broadcasted_iota(I32, (1, page_rows), 1)
    out = jnp.zeros((N_HEADS, hd), F32)
    for g in range(KV_HEADS):
        scores = []
        for i in range(top_n):
            blk = sel_ref[(b * KV_HEADS + g) * top_n + i]
            k_pos = (blk // per_page) * page_rows + lane
            picked = (lax.div(lane, SEL_BLOCK) == lax.rem(blk, per_page)) & (blk < n_past_blk)
            s = _mm(q, kbuf[slot, g, i]) * ATTN_SCALE - slopes * (t_pos - k_pos).astype(F32)
            scores.append(jnp.where(picked, s, NEG_INF))
        s_all = jnp.concatenate(scores, axis=1)
        s_new = jnp.sum(qf * new_col(4 + g), axis=1, keepdims=True) * ATTN_SCALE
        m = jnp.maximum(jnp.max(s_all, axis=1, keepdims=True), s_new)
        p_all = jnp.exp(s_all - m)
        p_new = jnp.exp(s_new - m)
        acc = p_new * new_col(6 + g)
        for i in range(top_n):
            acc = acc + _mm_nt(p_all[:, i * page_rows:(i + 1) * page_rows], vbuf[slot, g, i])
        o_sel = acc / (jnp.sum(p_all, axis=1, keepdims=True) + p_new)
        w_rows = win_ref.shape[-1]
        dist_w = (w_rows - lax.broadcasted_iota(I32, (1, w_rows), 1)).astype(F32)
        s_w = _mm(q, win_ref[0, 0, g]) * ATTN_SCALE - slopes * dist_w
        s_w = jnp.where(dist_w < float(WINDOW), s_w, NEG_INF)
        sw_new = jnp.sum(qf * new_col(8 + g), axis=1, keepdims=True) * ATTN_SCALE
        m_w = jnp.maximum(jnp.max(s_w, axis=1, keepdims=True), sw_new)
        p_w = jnp.exp(s_w - m_w)
        pw_new = jnp.exp(sw_new - m_w)
        o_win = (_mm_nt(p_w, win_ref[0, 1, g]) + pw_new * new_col(10 + g)) \
            / (jnp.sum(p_w, axis=1, keepdims=True) + pw_new)
        o = gates[:, 0:1] * ocmp_ref[0] + gates[:, 1:2] * o_sel + gates[:, 2:3] * o_win
        out = jnp.where(head_grp == g, o, out)
    o_ref[0] = out.astype(o_ref.dtype)


def _attn_sample_b_call(sel_idx, page_table, q_heads, cache5, kv_new, win5, gates3, o_cmp,
                        t_pos, top_n):
    bs, n_pages = page_table.shape
    _, _, n_g, hd, page_rows = cache5.shape
    per_page = page_rows // SEL_BLOCK
    n_past_blk = n_pages * per_page
    kern = functools.partial(_attn_sample_b_kernel, t_pos=t_pos, top_n=top_n, n_past_blk=n_past_blk,
                             n_pages=n_pages, per_page=per_page, n_steps=bs)
    per_b = lambda a: pl.BlockSpec((1,) + a.shape[1:], lambda b, sel, pt: (b,) + (0,) * (a.ndim - 1))
    tiles = pltpu.VMEM((2, n_g, top_n, hd, page_rows), F32)
    grid_spec = pltpu.PrefetchScalarGridSpec(
        num_scalar_prefetch=2,
        grid=(bs,),
        in_specs=[per_b(q_heads), pl.BlockSpec(memory_space=pl.ANY),
                  per_b(kv_new), per_b(win5), per_b(gates3), per_b(o_cmp)],
        out_specs=pl.BlockSpec((1, N_HEADS, HEAD_DIM), lambda b, sel, pt: (b, 0, 0)),
        scratch_shapes=[tiles, tiles, pltpu.SemaphoreType.DMA((2,))],
    )
    return pl.pallas_call(
        kern,
        grid_spec=grid_spec,
        out_shape=jax.ShapeDtypeStruct((bs, N_HEADS, HEAD_DIM), MXU_DTYPE),
        compiler_params=_cparams(1),
        name="attn_sample_gather",
    )(sel_idx, page_table.reshape(-1), q_heads, cache5, kv_new, win5, gates3, o_cmp)


def _prep_weights(w_in, w_a, w_b, w_out, router_w, router_b, sh_w1, sh_w3, sh_w2, g_norm1, g_norm2):
    d = w_in.shape[0]
    n_q = N_HEADS * HEAD_DIM
    n_kv = 6 * KV_HEADS * HEAD_DIM
    n_gate = 3 * N_HEADS
    o_gate = n_q + n_kv
    o_glu = o_gate + n_gate
    o_mrg = o_glu + d
    c = lambda a: a.astype(MXU_DTYPE)
    w_cat = jnp.concatenate(
        [w_in[:, :o_gate], w_in[:, o_glu:o_mrg],
         jnp.pad(w_in[:, o_gate:o_glu], ((0, 0), (0, LANES - n_gate)))], axis=1)
    mw = {"g1": g_norm1.reshape(1, d), "wm": c(w_in[:, o_mrg:]), "wa": c(w_a), "wb": c(w_b),
          "wo": c(w_out), "g2": g_norm2.reshape(1, d), "rwt": c(router_w.T),
          "rb": router_b.reshape(-1, 1), "s13": c(jnp.concatenate([sh_w1, sh_w3], axis=1)),
          "s2": c(sh_w2)}
    return c(w_cat), mw


def kernel(x_prompt, x_sample, cache_kv, cache_win, state_conv, page_table, c_prompt, c_sample, w_ada, b_ada, g_norm1, w_in, cmp_pe, cmp_w1, cmp_w2, conv_w, conv_b, conv_ln_g, conv_ln_b, w_a, w_b, w_out, g_norm2, router_w, router_b, exp_w1, exp_w3, exp_w2, sh_w1, sh_w3, sh_w2, g_final):
    l = 0
    assert w_ada.shape[0] == 1 and x_sample.shape[1] == 1 and cache_win.shape[2] == WINDOW
    bp, sp, d = x_prompt.shape
    bs = x_sample.shape[0]
    n_pool, page_rows = cache_kv.shape[1], cache_kv.shape[2]
    n_pages = page_table.shape[1]
    past_len = n_pages * page_rows
    hd = HEAD_DIM
    c_conv = d // 2
    n_sub = d // LANES
    n_kv_new = 4 * KV_HEADS * hd
    rb = 256

    mod = _ada_call(jnp.concatenate([c_prompt, c_sample], axis=0), w_ada[l], b_ada[l])
    mods_p = [mod[:bp, None, k * d:(k + 1) * d] for k in range(6)]
    mods_s = [mod[None, bp:, k * d:(k + 1) * d] for k in range(6)]
    w_cat, mw = _prep_weights(w_in[l], w_a[l], w_b[l], w_out[l], router_w[l], router_b[l],
                              sh_w1[l], sh_w3[l], sh_w2[l], g_norm1[l], g_norm2[l])
    cw = _compress_weights(cmp_pe[l], cmp_w1[l], cmp_w2[l])

    sh1, sc1, gt1, sh2, sc2, gt2 = mods_p
    q, kvw, glu, gates = _inproj_call(x_prompt, sc1, sh1, g_norm1[l], w_cat, min(sp, 512))
    kcvc = _compress_prompt_call(kvw, cw)
    o_attn = _attn_prompt_call(q, kvw, kcvc, gates, 128, 512 if sp % 512 == 0 else 256)
    z = _conv_prompt_call(glu, conv_w[l], conv_b[l], conv_ln_g[l], conv_ln_b[l], 64)
    x1_p, h2_p, ysh_p, eid_p, ew_p, rank_p, cnt_p = _merge_call(
        x_prompt, o_attn, z, (sc1, sh1, gt1, sc2, sh2), mw, min(sp, 256))

    sh1s, sc1s, gt1s, sh2s, sc2s, gt2s = mods_s
    xs = x_sample.reshape(1, bs, d)
    q_s, kvw_s, glu_s, gates_s = _inproj_call(xs, sc1s, sh1s, g_norm1[l], w_cat, bs)
    cache5 = jnp.transpose(cache_kv[l], (0, 2, 3, 4, 1))
    win5 = jnp.transpose(cache_win[l], (0, 2, 3, 4, 1))
    fs = _compress_sample_call(page_table, cache5, cw["w1bd"], min(n_pages, 32))
    q_heads = q_s.reshape(bs, N_HEADS, hd)
    n_blk = -(-(past_len + 1) // SEL_BLOCK)
    top_n = min(TOP_N, n_blk)
    o_cmp_s, sel = _attn_sample_a_call(q_heads, fs, cw, past_len, n_blk)
    o_s = _attn_sample_b_call(
        sel[:, :KV_HEADS, :top_n].reshape(-1), page_table, q_heads, cache5,
        kvw_s.reshape(bs, 1, -1), win5,
        gates_s[0, :, :3 * N_HEADS].reshape(bs, N_HEADS, 3), o_cmp_s, past_len, top_n)
    glu_new = glu_s.reshape(bs, 1, c_conv)
    z_s = _conv_sample_call(jnp.transpose(state_conv[l], (1, 0, 2)), glu_s[0],
                            conv_w[l], conv_b[l], conv_ln_g[l], conv_ln_b[l])
    x1_s, h2_s, ysh_s, eid_s, ew_s, rank_s, cnt_s = _merge_call(
        xs, o_s.reshape(1, bs, N_HEADS * hd), z_s.reshape(1, bs, c_conv),
        (sc1s, sh1s, gt1s, sc2s, sh2s), mw, bs)

    n_p = bp * sp
    cnt_p = cnt_p[:, 0].astype(I32)
    cnt_s = cnt_s[:, 0].astype(I32)
    counts = cnt_p + cnt_s
    rank_s = rank_s + cnt_p[eid_s]
    start, padded, blk_e, n_used, n_blk = _moe_plan(counts, (n_p + bs) * TOP_K, rb)
    x_rows, dst_p, dst_s = _dispatch_call(start, counts, padded, n_used, eid_p, rank_p, eid_s,
                                          rank_s, h2_p, h2_s, n_blk, rb, min(n_p, 512))
    y_rows = _experts_call(x_rows, blk_e, n_used, exp_w1[l], exp_w3[l], exp_w2[l], rb)
    y_prompt = _combine_call(dst_p, y_rows, ew_p.T, x1_p, ysh_p, gt2, g_final, min(sp, 128))
    y_sample = _combine_call(dst_s, y_rows, ew_s.T, x1_s, ysh_s, gt2s, g_final, bs)

    w_keep = min(WINDOW, sp)
    kv_prompt = kvw[:, :, :n_kv_new].reshape(1, bp, sp, 4, KV_HEADS, hd)
    kv_sample = kvw_s[0, :, :n_kv_new].reshape(1, bs, 1, 4, KV_HEADS, hd)
    win_prompt = kvw[:, sp - w_keep:, n_kv_new:].reshape(1, bp, w_keep, 2, KV_HEADS, hd)
    win_new = kvw_s[0, :, n_kv_new:].reshape(bs, 1, 2, KV_HEADS, hd)
    win_sample = jnp.concatenate([cache_win[l][:, 1:], win_new], axis=1)[None]
    conv_prompt = glu[:, sp - (CONV_W - 1):][None]
    conv_sample = jnp.concatenate([state_conv[l][:, 1:], glu_new], axis=1)[None]
    return (y_prompt, y_sample.reshape(bs, 1, d), kv_prompt, kv_sample, win_prompt, win_sample,
            conv_prompt, conv_sample)
```

```python
import functools

import jax
import jax.numpy as jnp
from jax import lax
from jax.experimental import pallas as pl
from jax.experimental.pallas import tpu as pltpu

F32 = jnp.float32
BF16 = jnp.bfloat16
I32 = jnp.int32
MXU_DTYPE = jnp.bfloat16

N_HEADS = 8
KV_HEADS = 2
Q_PER_KV = N_HEADS // KV_HEADS
HEAD_DIM = 64
CMP_BLOCK = 32
CMP_STRIDE = 16
CMP_HID = 2 * HEAD_DIM
SEL_BLOCK = 64
TOP_N = 16
WINDOW = 512
CONV_W = 31
N_EXPERTS = 256
N_GROUPS = 8
TOPK_GROUPS = 4
TOP_K = 8
ROUTED_SCALE = 2.5
EPS = 1e-6
NEG_INF = -1e30
FORCE_SCORE = 1e4
ATTN_SCALE = HEAD_DIM ** -0.5

LANES = 128
SUBLANES = 8
VMEM_LIMIT = 56 * 1024 * 1024


def _cparams(n_axes):
    return pltpu.CompilerParams(
        dimension_semantics=("arbitrary",) * n_axes, vmem_limit_bytes=VMEM_LIMIT)


def _mm(a, b):
    return jnp.dot(a.astype(MXU_DTYPE), b.astype(MXU_DTYPE), preferred_element_type=F32)


def _mm_nt(a, b):
    return lax.dot_general(a.astype(MXU_DTYPE), b.astype(MXU_DTYPE),
                           (((1,), (1,)), ((), ())), preferred_element_type=F32)


def _mm_tn(a, b):
    return lax.dot_general(a.astype(MXU_DTYPE), b.astype(MXU_DTYPE),
                           (((0,), (0,)), ((), ())), preferred_element_type=F32)


def _sigmoid(x):
    return 1.0 / (1.0 + jnp.exp(-x))


def _silu(x):
    return x * _sigmoid(x)


def _rms(x, g):
    return x * lax.rsqrt(jnp.mean(x * x, axis=-1, keepdims=True) + EPS) * g


def _slope(head):
    return 2.0 ** (-8.0 * (head + 1) / N_HEADS)


def _ada_kernel(c_ref, w_ref, b_ref, o_ref):
    o_ref[...] = _mm(_silu(c_ref[...]), w_ref[...]) + b_ref[...]


def _ada_call(c, w_ada, b_ada):
    n, d = c.shape
    n_out = w_ada.shape[1]
    tn = n_out // 6
    return pl.pallas_call(
        _ada_kernel,
        grid=(n_out // tn,),
        in_specs=[pl.BlockSpec((n, d), lambda j: (0, 0)),
                  pl.BlockSpec((d, tn), lambda j: (0, j)),
                  pl.BlockSpec((1, tn), lambda j: (0, j))],
        out_specs=pl.BlockSpec((n, tn), lambda j: (0, j)),
        out_shape=jax.ShapeDtypeStruct((n, n_out), F32),
        compiler_params=_cparams(1),
        name="ada_mod",
    )(c, w_ada, b_ada.reshape(1, n_out))


def _inproj_kernel(x_ref, sc_ref, sh_ref, g_ref, w_ref, q_ref, kv_ref, glu_ref, gate_ref,
                   *, n_q, n_kv, c_conv):
    x = x_ref[0]
    h = _rms(x, g_ref[...]) * (1.0 + sc_ref[0]) + sh_ref[0]
    y = _mm(h, w_ref[...])
    o = 0
    q_ref[0] = y[:, o:o + n_q].astype(q_ref.dtype)
    o += n_q
    kv_ref[0] = y[:, o:o + n_kv]
    o += n_kv
    u_a = y[:, o:o + c_conv]
    u_g = y[:, o + c_conv:o + 2 * c_conv]
    glu_ref[0] = u_a * _sigmoid(u_g)
    o += 2 * c_conv
    gate_ref[0] = _sigmoid(y[:, o:o + LANES])


def _mod_spec(mod, tm, d):
    if mod.shape[1] == 1:
        return pl.BlockSpec((1, 1, d), lambda b, i: (b, 0, 0))
    return pl.BlockSpec((1, tm, d), lambda b, i: (b, i, 0))


def _inproj_call(x, sc1, sh1, g1, w_cat, tm):
    bx, tx, d = x.shape
    n_q = N_HEADS * HEAD_DIM
    n_kv = 6 * KV_HEADS * HEAD_DIM
    c_conv = d // 2
    n_cat = w_cat.shape[1]
    kern = functools.partial(_inproj_kernel, n_q=n_q, n_kv=n_kv, c_conv=c_conv)
    row = lambda w: pl.BlockSpec((1, tm, w), lambda b, i: (b, i, 0))
    return pl.pallas_call(
        kern,
        grid=(bx, tx // tm),
        in_specs=[row(d), _mod_spec(sc1, tm, d), _mod_spec(sh1, tm, d),
                  pl.BlockSpec((1, d), lambda b, i: (0, 0)),
                  pl.BlockSpec((d, n_cat), lambda b, i: (0, 0))],
        out_specs=[row(n_q), row(n_kv), row(c_conv), row(LANES)],
        out_shape=[jax.ShapeDtypeStruct((bx, tx, n_q), MXU_DTYPE),
                   jax.ShapeDtypeStruct((bx, tx, n_kv), F32),
                   jax.ShapeDtypeStruct((bx, tx, c_conv), F32),
                   jax.ShapeDtypeStruct((bx, tx, LANES), F32)],
        compiler_params=_cparams(2),
        name="in_proj",
    )(x, sc1, sh1, g1.reshape(1, d), w_cat)


def _cmp_first_layer(k_of_l, v_of_l, w1k_ref, w1v_ref):
    half = KV_HEADS * CMP_HID
    f_k = _mm(jnp.concatenate([k_of_l(l) for l in range(CMP_STRIDE)], axis=1), w1k_ref[...])
    f_v = _mm(jnp.concatenate([v_of_l(l) for l in range(CMP_STRIDE)], axis=1), w1v_ref[...])
    return jnp.concatenate([f_k[:, :half], f_v[:, :half], f_k[:, half:], f_v[:, half:]], axis=1)


def _cmp_second_layer(first, second_next, pe_ref, w1pe_ref, w2bd_ref):
    pe_term = _mm(pe_ref[...], w1pe_ref[...])[0:1]
    return _mm(_silu(first + second_next + pe_term), w2bd_ref[...])


def _compress_prompt_kernel(k_ref, v_ref, w1k_ref, w1v_ref, pe_ref, w1pe_ref, w2bd_ref, o_ref, fs_ref,
                            *, n_chunks):
    half = 4 * CMP_HID
    rows_of = lambda ref: (lambda l: ref.at[0][pl.ds(l, n_chunks, stride=CMP_STRIDE), :])
    fs_ref[pl.ds(0, n_chunks), :] = _cmp_first_layer(rows_of(k_ref), rows_of(v_ref), w1k_ref, w1v_ref)
    fs_ref[pl.ds(n_chunks, SUBLANES), :] = jnp.zeros((SUBLANES, 2 * half), F32)
    first = fs_ref[pl.ds(0, n_chunks), pl.ds(0, half)]
    second_next = fs_ref[pl.ds(1, n_chunks), pl.ds(half, half)]
    o_ref[0] = _cmp_second_layer(first, second_next, pe_ref, w1pe_ref, w2bd_ref)


def _compress_prompt_call(kvw, cw):
    b, t, _ = kvw.shape
    n_chunks = t // CMP_STRIDE
    wid = 4 * HEAD_DIM
    kern = functools.partial(_compress_prompt_kernel, n_chunks=n_chunks)
    full = lambda a: pl.BlockSpec(a.shape, lambda i: (0,) * a.ndim)
    return pl.pallas_call(
        kern,
        grid=(b,),
        in_specs=[pl.BlockSpec((1, t, LANES), lambda i: (i, 0, 0)),
                  pl.BlockSpec((1, t, LANES), lambda i: (i, 0, 1)),
                  full(cw["w1k"]), full(cw["w1v"]), full(cw["pe"]), full(cw["w1pe"]), full(cw["w2bd"])],
        out_specs=pl.BlockSpec((1, n_chunks, wid), lambda i: (i, 0, 0)),
        out_shape=jax.ShapeDtypeStruct((b, n_chunks, wid), F32),
        scratch_shapes=[pltpu.VMEM((n_chunks + SUBLANES, 8 * CMP_HID), F32)],
        compiler_params=_cparams(1),
        name="compress_prompt",
    )(kvw, kvw, cw["w1k"], cw["w1v"], cw["pe"], cw["w1pe"], cw["w2bd"])


def _compress_weights(cmp_pe, cmp_w1, cmp_w2):
    hd, hid = HEAD_DIM, CMP_HID
    slab_kv = (0, 0, 1, 1)

    def first_layer(kv):
        w = jnp.zeros((CMP_STRIDE, KV_HEADS, hd, 2, KV_HEADS, hid), F32)
        for g in range(KV_HEADS):
            w = w.at[:, g, :, 0, g, :].set(cmp_w1[kv, :CMP_STRIDE])
            w = w.at[:, g, :, 1, g, :].set(cmp_w1[kv, CMP_STRIDE:])
        return w.reshape(CMP_STRIDE * KV_HEADS * hd, 2 * KV_HEADS * hid).astype(MXU_DTYPE)

    w2bd = jnp.zeros((4 * hid, 4 * hd), F32)
    for s, kv in enumerate(slab_kv):
        w2bd = w2bd.at[s * hid:(s + 1) * hid, s * hd:(s + 1) * hd].set(cmp_w2[kv])
    pe = jnp.broadcast_to(cmp_pe.reshape(1, -1), (SUBLANES, 2 * CMP_BLOCK * hd))
    w1pe = jnp.zeros((2 * CMP_BLOCK * hd, 4 * hid), F32)
    n_flat = CMP_BLOCK * hd
    for s, kv in enumerate(slab_kv):
        w1pe = w1pe.at[kv * n_flat:(kv + 1) * n_flat, s * hid:(s + 1) * hid].set(
            cmp_w1[kv].reshape(n_flat, hid))
    return {"w1k": first_layer(0), "w1v": first_layer(1), "w2bd": w2bd.astype(MXU_DTYPE),
            "pe": pe, "w1pe": w1pe.astype(MXU_DTYPE)}


def _softmax_cols(s, mask):
    m = jnp.max(jnp.where(mask, s, NEG_INF), axis=0, keepdims=True)
    p = jnp.where(mask, jnp.exp(s - m), 0.0)
    l = jnp.sum(p, axis=0, keepdims=True)
    return p / jnp.where(l > 0.0, l, 1.0)


def _top_n_rows(imp, top_n):
    n = imp.shape[0]
    row_id = lax.broadcasted_iota(I32, imp.shape, 0)
    beaten = jnp.zeros(imp.shape, F32)
    for m in range(n):
        other = imp[m:m + 1, :]
        wins = (other > imp) | ((other == imp) & (row_id > m))
        beaten = beaten + jnp.where(wins, 1.0, 0.0)
    return beaten < float(top_n)


SEL_FEAT0 = HEAD_DIM + SUBLANES
MASK_BIG = -2.0 ** 100


def _attn_prompt_kernel(q_ref, ksv_ref, kwv_ref, kc_ref, gate_ref, o_ref,
                        kaug, vsel, kwaug, vwin, psum_ref, *, tq, kc_len, t_len):
    qt = pl.program_id(1)
    q0 = qt * tq
    hd = HEAD_DIM
    n_cmp_rows = kc_ref.shape[1]
    n_blk = t_len // SEL_BLOCK
    top_n = min(TOP_N, n_blk)
    wk = min(WINDOW + tq, t_len)

    @pl.when(qt == 0)
    def _():
        lane = lax.broadcasted_iota(I32, (t_len, LANES), 1)
        pos = lax.broadcasted_iota(I32, (t_len, LANES), 0)
        pos_hi = lax.div(pos, SEL_BLOCK)
        alibi = jnp.where(lane < hd + 2, 1.0,
                          jnp.where(lane == hd + 2, pos_hi.astype(F32),
                                    jnp.where(lane == hd + 3, lax.rem(pos, SEL_BLOCK).astype(F32), 0.0)))
        in_blk = jnp.where(lane - SEL_FEAT0 == pos_hi, 1.0, 0.0)
        for g in range(KV_HEADS):
            k_s = ksv_ref[0, :, 0:LANES]
            k_w = kwv_ref[0, :, 0:LANES]
            if g:
                k_s = pltpu.roll(k_s, LANES - g * hd, 1)
                k_w = pltpu.roll(k_w, LANES - g * hd, 1)
            kaug[g] = jnp.where(lane < hd, k_s, alibi + in_blk).astype(kaug.dtype)
            kwaug[g] = jnp.where(lane < hd, k_w, alibi).astype(kwaug.dtype)
            v_s = ksv_ref[0, :, LANES:2 * LANES]
            v_w = kwv_ref[0, :, LANES:2 * LANES]
            if g:
                v_s = pltpu.roll(v_s, LANES - g * hd, 1)
                v_w = pltpu.roll(v_w, LANES - g * hd, 1)
            ones_lane = jnp.where(lane == hd, 1.0, 0.0)
            vsel[g] = jnp.where(lane < hd, v_s, ones_lane).astype(vsel.dtype)
            vwin[g] = jnp.where(lane < hd, v_w, ones_lane).astype(vwin.dtype)
        psum_ref[...] = jnp.zeros(psum_ref.shape, F32)

    q_blk = q_ref[0]
    gates = gate_ref[0]
    rows = Q_PER_KV * tq
    row_t = q0 + lax.rem(lax.broadcasted_iota(I32, (rows, 1), 0), tq)
    lane_q = lax.broadcasted_iota(I32, (tq, LANES), 1)
    t_q = q0 + lax.broadcasted_iota(I32, (tq, LANES), 0)
    t_hi = (lax.div(t_q, SEL_BLOCK) * SEL_BLOCK).astype(F32)
    t_lo = lax.rem(t_q, SEL_BLOCK).astype(F32)
    w0 = pl.multiple_of(jnp.maximum(q0 + tq - wk, 0), tq)
    dist_w = row_t - (w0 + lax.broadcasted_iota(I32, (rows, wk), 1))
    band = jnp.where((dist_w >= 0) & (dist_w < WINDOW), 0.0, NEG_INF)
    out_heads = []
    for g in range(KV_HEADS):
        q_heads = [q_blk[:, (g * Q_PER_KV + r) * hd:(g * Q_PER_KV + r + 1) * hd]
                   for r in range(Q_PER_KV)]
        k_cmp = kc_ref[0, :, g * hd:(g + 1) * hd]
        v_cmp = kc_ref[0, :, (2 + g) * hd:(3 + g) * hd]
        t_lane = q0 + lax.broadcasted_iota(I32, (n_cmp_rows, tq), 1)
        c_pos = lax.broadcasted_iota(I32, (n_cmp_rows, tq), 0) * CMP_STRIDE + (CMP_BLOCK - 1)
        dist_c = t_lane - c_pos
        mask_c = dist_c >= 0
        dist_cf = dist_c.astype(F32)
        o_cmp = []
        p_sum = None
        for r in range(Q_PER_KV):
            s = _mm_nt(k_cmp, q_heads[r]) * ATTN_SCALE - _slope(g * Q_PER_KV + r) * dist_cf
            p = _softmax_cols(s, mask_c)
            o_cmp.append(_mm_tn(p, v_cmp))
            p_sum = p if p_sum is None else p_sum + p
        psum_ref[pl.ds(SUBLANES, n_cmp_rows), :] = p_sum
        per = SEL_BLOCK // CMP_STRIDE
        taps = [psum_ref[pl.ds(SUBLANES - 1 + k, n_blk, stride=per), :] for k in range(per + 1)]
        imp = 0.5 * taps[0] + 0.5 * taps[per]
        for k in range(1, per):
            imp = imp + taps[k]
        blk = lax.broadcasted_iota(I32, (n_blk, tq), 0)
        t_blk = q0 + lax.broadcasted_iota(I32, (n_blk, tq), 1)
        cur = lax.div(t_blk, SEL_BLOCK)
        forced = (blk == 0) | (blk == cur) | (blk == cur - 1)
        imp = jnp.where(forced, FORCE_SCORE, imp)
        imp = jnp.where(blk * SEL_BLOCK <= t_blk, imp, -jnp.inf)
        sel_neg = jnp.where(_top_n_rows(imp, top_n), 0.0, MASK_BIG)
        sel_lanes = jnp.transpose(jnp.concatenate(
            [jnp.zeros((SEL_FEAT0, tq), F32), sel_neg,
             jnp.zeros((LANES - SEL_FEAT0 - n_blk, tq), F32)], axis=0))
        q_aug = []
        for r in range(Q_PER_KV):
            head = g * Q_PER_KV + r
            slope = _slope(head)
            pair = q_blk[:, (head // 2) * LANES:(head // 2 + 1) * LANES].astype(F32)
            if head % 2:
                pair = pltpu.roll(pair, LANES - hd, 1)
            feats = jnp.where(lane_q == hd, -slope * t_hi,
                              jnp.where(lane_q == hd + 1, -slope * t_lo,
                                        jnp.where(lane_q == hd + 2, slope * SEL_BLOCK,
                                                  jnp.where(lane_q == hd + 3, slope, sel_lanes))))
            q_aug.append(jnp.where(lane_q < hd, pair * ATTN_SCALE, feats).astype(MXU_DTYPE))
        q_aug = jnp.concatenate(q_aug, axis=0)

        def sel_chunk(j, carry, causal):
            m_run, acc = carry
            k0 = pl.multiple_of(j * kc_len, kc_len)
            s = _mm_nt(q_aug, kaug[g, pl.ds(k0, kc_len), :])
            if causal:
                k_pos = k0 + lax.broadcasted_iota(I32, (rows, kc_len), 1)
                s = jnp.where(k_pos <= row_t, s, NEG_INF)
            m_new = jnp.maximum(m_run, jnp.max(s, axis=1, keepdims=True))
            p = jnp.exp((s - m_new).astype(MXU_DTYPE))
            acc_new = jnp.exp(m_run - m_new) * acc + _mm(p, vsel[g, pl.ds(k0, kc_len), :])
            return m_new, acc_new

        n_full = lax.div(q0, kc_len)
        init = (jnp.full((rows, 1), NEG_INF, F32), jnp.zeros((rows, LANES), F32))
        carry = lax.fori_loop(0, n_full, functools.partial(sel_chunk, causal=False), init)
        _, acc_sel = sel_chunk(n_full, carry, causal=True)
        o_sel = acc_sel[:, :hd] / acc_sel[:, hd:hd + 1]
        s = _mm_nt(q_aug, kwaug[g, pl.ds(w0, wk), :]) + band
        p = jnp.exp((s - jnp.max(s, axis=1, keepdims=True)).astype(MXU_DTYPE))
        acc_win = _mm(p, vwin[g, pl.ds(w0, wk), :])
        o_win = acc_win[:, :hd] / acc_win[:, hd:hd + 1]
        for r in range(Q_PER_KV):
            c = (g * Q_PER_KV + r) * 3
            out_heads.append(gates[:, c:c + 1] * o_cmp[r]
                             + gates[:, c + 1:c + 2] * o_sel[r * tq:(r + 1) * tq]
                             + gates[:, c + 2:c + 3] * o_win[r * tq:(r + 1) * tq])
    o_ref[0] = jnp.concatenate(out_heads, axis=1).astype(o_ref.dtype)


def _attn_prompt_call(q, kvw, kcvc, gates, tq, kc_len):
    b, t, n_q = q.shape
    wid = 4 * HEAD_DIM
    kern = functools.partial(_attn_prompt_kernel, tq=tq, kc_len=kc_len, t_len=t)
    assert SEL_FEAT0 + t // SEL_BLOCK <= LANES and kc_len % tq == 0
    k_scratch = pltpu.VMEM((KV_HEADS, t, LANES), MXU_DTYPE)
    v_scratch = pltpu.VMEM((KV_HEADS, t, LANES), MXU_DTYPE)
    n_cmp_rows = kcvc.shape[1]
    return pl.pallas_call(
        kern,
        grid=(b, t // tq),
        in_specs=[pl.BlockSpec((1, tq, n_q), lambda i, j: (i, j, 0)),
                  pl.BlockSpec((1, t, wid), lambda i, j: (i, 0, 1)),
                  pl.BlockSpec((1, t, wid), lambda i, j: (i, 0, 2)),
                  pl.BlockSpec((1, n_cmp_rows, wid), lambda i, j: (i, 0, 0)),
                  pl.BlockSpec((1, tq, LANES), lambda i, j: (i, j, 0))],
        out_specs=pl.BlockSpec((1, tq, n_q), lambda i, j: (i, j, 0)),
        out_shape=jax.ShapeDtypeStruct((b, t, n_q), MXU_DTYPE),
        scratch_shapes=[k_scratch, v_scratch, k_scratch, v_scratch,
                        pltpu.VMEM((n_cmp_rows + 2 * SUBLANES, tq), F32)],
        compiler_params=_cparams(2),
        name="attn_prompt",
    )(q, kvw, kvw, kcvc, gates)


CONV_PAD = 32


def _ln_silu(z, lg, lb):
    mu = jnp.mean(z, axis=-1, keepdims=True)
    zc = z - mu
    var = jnp.mean(zc * zc, axis=-1, keepdims=True)
    return _silu(zc * lax.rsqrt(var + EPS) * lg + lb)


def _conv_prompt_kernel(glu_ref, cw_ref, cb_ref, lg_ref, lb_ref, z_ref, full_ref, *, t_len, tt):
    full_ref[pl.ds(0, CONV_PAD), :] = jnp.zeros((CONV_PAD, full_ref.shape[1]), F32)
    full_ref[pl.ds(CONV_PAD, t_len), :] = glu_ref[0]
    first = CONV_PAD - (CONV_W - 1)

    def tile(i, carry):
        r0 = pl.multiple_of(i * tt, tt)
        acc = jnp.zeros((tt, full_ref.shape[1]), F32) + cb_ref[...]
        win = full_ref[pl.ds(r0, tt + CONV_PAD), :]
        for phase in range(SUBLANES):
            offs = [o for o in range(first, first + CONV_W) if o % SUBLANES == phase]
            if not offs:
                continue
            shifted = win[phase:max(offs) + tt]
            for o in offs:
                acc = acc + shifted[o - phase:o - phase + tt] * cw_ref[o - first:o - first + 1, :]
        z_ref[0, pl.ds(r0, tt), :] = _ln_silu(acc, lg_ref[...], lb_ref[...]).astype(z_ref.dtype)
        return carry

    lax.fori_loop(0, t_len // tt, tile, 0)


def _conv_prompt_call(glu, conv_w, conv_b, ln_g, ln_b, tt):
    b, t, c = glu.shape
    kern = functools.partial(_conv_prompt_kernel, t_len=t, tt=tt)
    vec = pl.BlockSpec((1, c), lambda i: (0, 0))
    return pl.pallas_call(
        kern,
        grid=(b,),
        in_specs=[pl.BlockSpec((1, t, c), lambda i: (i, 0, 0)),
                  pl.BlockSpec((CONV_PAD, c), lambda i: (0, 0)), vec, vec, vec],
        out_specs=pl.BlockSpec((1, t, c), lambda i: (i, 0, 0)),
        out_shape=jax.ShapeDtypeStruct((b, t, c), MXU_DTYPE),
        scratch_shapes=[pltpu.VMEM((CONV_PAD + t, c), F32)],
        compiler_params=_cparams(1),
        name="conv_prompt",
    )(glu, jnp.pad(conv_w, ((0, CONV_PAD - CONV_W), (0, 0))), conv_b.reshape(1, c),
      ln_g.reshape(1, c), ln_b.reshape(1, c))


def _conv_sample_kernel(hist_ref, new_ref, cw_ref, cb_ref, lg_ref, lb_ref, z_ref):
    z = new_ref[...] * cw_ref[CONV_W - 1:CONV_W, :] + cb_ref[...]
    for w in range(CONV_W - 1):
        z = z + hist_ref[w] * cw_ref[w:w + 1, :]
    z_ref[...] = _ln_silu(z, lg_ref[...], lb_ref[...]).astype(z_ref.dtype)


def _conv_sample_call(hist, glu_new, conv_w, conv_b, ln_g, ln_b):
    bs, c = glu_new.shape
    return pl.pallas_call(
        _conv_sample_kernel,
        out_shape=jax.ShapeDtypeStruct((bs, c), MXU_DTYPE),
        name="conv_sample",
    )(hist, glu_new, jnp.pad(conv_w, ((0, CONV_PAD - CONV_W), (0, 0))), conv_b.reshape(1, c),
      ln_g.reshape(1, c), ln_b.reshape(1, c))


def _route_tile(h2, rwt_ref, rb_ref, cnt_ref):
    n_e = N_EXPERTS
    per_grp = n_e // N_GROUPS
    tm = h2.shape[0]
    aff = _sigmoid(_mm_nt(rwt_ref[...], h2))
    biased = aff + rb_ref[...]
    neg = -jnp.inf
    g_rows = []
    for g in range(N_GROUPS):
        v = biased[g * per_grp:(g + 1) * per_grp]
        m1 = jnp.max(v, axis=0, keepdims=True)
        is_m1 = v == m1
        n_m1 = jnp.sum(jnp.where(is_m1, 1.0, 0.0), axis=0, keepdims=True)
        m2 = jnp.max(jnp.where(is_m1, neg, v), axis=0, keepdims=True)
        g_rows.append(m1 + jnp.where(n_m1 >= 2.0, m1, m2))
    g_keep = _top_n_rows(jnp.concatenate(g_rows, axis=0), TOPK_GROUPS)
    cur = jnp.concatenate(
        [jnp.where(g_keep[g:g + 1], biased[g * per_grp:(g + 1) * per_grp], neg)
         for g in range(N_GROUPS)], axis=0)
    row_id = lax.broadcasted_iota(I32, (n_e, tm), 0).astype(F32)
    ids, wts, hots = [], [], []
    for _ in range(TOP_K):
        m = jnp.max(cur, axis=0, keepdims=True)
        idx = jnp.min(jnp.where(cur == m, row_id, float(n_e)), axis=0, keepdims=True)
        hot = row_id == idx
        ids.append(idx)
        wts.append(jnp.sum(jnp.where(hot, aff, 0.0), axis=0, keepdims=True))
        hots.append(hot)
        cur = jnp.where(hot, neg, cur)
    w_sum = wts[0]
    for w in wts[1:]:
        w_sum = w_sum + w
    wts = [w / w_sum * ROUTED_SCALE for w in wts]
    hot_all = jnp.where(hots[0], 1.0, 0.0)
    for hot in hots[1:]:
        hot_all = hot_all + jnp.where(hot, 1.0, 0.0)
    earlier = jnp.where(lax.broadcasted_iota(I32, (tm, tm), 0) < lax.broadcasted_iota(I32, (tm, tm), 1),
                        1.0, 0.0)
    before = cnt_ref[:, 0:1] + jnp.dot(hot_all.astype(BF16), earlier.astype(BF16),
                                       preferred_element_type=F32)
    ranks = [jnp.sum(jnp.where(hot, before, 0.0), axis=0, keepdims=True) for hot in hots]
    cnt_ref[...] = cnt_ref[...] + jnp.sum(hot_all, axis=1, keepdims=True)
    cat = lambda rows: jnp.concatenate(rows, axis=0)
    return cat(ids).astype(I32), cat(wts), cat(ranks).astype(I32)


def _merge_kernel(x_ref, oa_ref, z_ref, sc1_ref, sh1_ref, gt1_ref, sc2_ref, sh2_ref,
                  g1_ref, wm_ref, wa_ref, wb_ref, wo_ref, g2_ref, rwt_ref, rb_ref,
                  s13_ref, s2_ref,
                  x1_ref, h2_ref, ysh_ref, eid_ref, ew_ref, rk_ref, cnt_out_ref, cnt_ref, *, d_exp):
    first_step = (pl.program_id(0) == 0) & (pl.program_id(1) == 0)

    @pl.when(first_step)
    def _():
        cnt_ref[...] = jnp.zeros(cnt_ref.shape, F32)

    x = x_ref[0]
    d = x.shape[1]
    tm = x.shape[0]
    h = _rms(x, g1_ref[...]) * (1.0 + sc1_ref[0]) + sh1_ref[0]
    g_mrg = _sigmoid(_mm(h, wm_ref[...]))
    mixed = g_mrg[:, :d] * _mm(oa_ref[0], wa_ref[...]) + g_mrg[:, d:] * _mm(z_ref[0], wb_ref[...])
    x1 = x + gt1_ref[0] * _mm(mixed, wo_ref[...])
    x1_ref[0] = x1
    h2 = _rms(x1, g2_ref[...]) * (1.0 + sc2_ref[0]) + sh2_ref[0]
    for s in range(d // LANES):
        h2_ref[pl.ds(s, tm, stride=d // LANES), :] = h2[:, s * LANES:(s + 1) * LANES]
    hs = _mm(h2, s13_ref[...])
    ysh_ref[0] = _mm(_silu(hs[:, :d_exp]) * hs[:, d_exp:], s2_ref[...])
    ids, wts, ranks = _route_tile(h2, rwt_ref, rb_ref, cnt_ref)
    eid_ref[...] = ids
    ew_ref[...] = wts
    rk_ref[...] = ranks
    cnt_out_ref[...] = cnt_ref[...]


def _merge_call(x, o_attn, z, mods, mw, tm):
    bx, tx, d = x.shape
    n = bx * tx
    nt = tx // tm
    d_exp = mw["s2"].shape[0]
    kern = functools.partial(_merge_kernel, d_exp=d_exp)
    row = lambda w: pl.BlockSpec((1, tm, w), lambda b, i: (b, i, 0))
    full = lambda a: pl.BlockSpec(a.shape, lambda b, i: (0,) * a.ndim)
    tok = pl.BlockSpec((TOP_K, tm), lambda b, i: (0, b * nt + i))
    wnames = ("g1", "wm", "wa", "wb", "wo", "g2", "rwt", "rb", "s13", "s2")
    return pl.pallas_call(
        kern,
        grid=(bx, nt),
        in_specs=[row(d), row(o_attn.shape[2]), row(z.shape[2])]
                 + [_mod_spec(m, tm, d) for m in mods]
                 + [full(mw[k]) for k in wnames],
        out_specs=[row(d),
                   pl.BlockSpec((tm * (d // LANES), LANES), lambda b, i: (b * nt + i, 0)),
                   row(d), tok, tok, tok,
                   pl.BlockSpec((N_EXPERTS, LANES), lambda b, i: (0, 0))],
        out_shape=[jax.ShapeDtypeStruct((bx, tx, d), F32),
                   jax.ShapeDtypeStruct((n * (d // LANES), LANES), F32),
                   jax.ShapeDtypeStruct((bx, tx, d), F32),
                   jax.ShapeDtypeStruct((TOP_K, n), I32),
                   jax.ShapeDtypeStruct((TOP_K, n), F32),
                   jax.ShapeDtypeStruct((TOP_K, n), I32),
                   jax.ShapeDtypeStruct((N_EXPERTS, LANES), F32)],
        scratch_shapes=[pltpu.VMEM((N_EXPERTS, LANES), F32)],
        compiler_params=_cparams(2),
        name="merge_route",
    )(x, o_attn, z, *mods, *[mw[k] for k in wnames])


def _start_tile_copy(src_hbm, src_row, buf, slot, dst_row, sem, n_sub):
    pltpu.make_async_copy(src_hbm.at[pl.ds(pl.multiple_of(src_row, n_sub), n_sub), :],
                          buf.at[slot, pl.ds(pl.multiple_of(dst_row, n_sub), n_sub), :],
                          sem.at[slot]).start()


def _wait_slot(buf, sem, slot):
    pltpu.make_async_copy(buf.at[slot], buf.at[slot], sem.at[slot]).wait()


def _moe_plan(counts, n_asg, rb):
    n_blk = -(-(n_asg + N_EXPERTS * (rb - 1)) // rb)
    padded = (counts + rb - 1) // rb * rb
    pad_end = jnp.cumsum(padded)
    start = pad_end - padded
    blk_row = jnp.arange(n_blk, dtype=I32)[:, None] * rb
    blk_e = jnp.minimum(jnp.sum((pad_end[None, :] <= blk_row).astype(I32), axis=1), N_EXPERTS - 1)
    n_used = (pad_end[-1] // rb).astype(I32).reshape(1)
    return start.astype(I32), padded.astype(I32), blk_e.astype(I32), n_used, n_blk


def _dispatch_kernel(start_ref, cnt_ref, pad_ref, n_used_ref,
                     eid_ref, rk_ref, eid_s_ref, rk_s_ref, h2p_ref, h2s_ref,
                     xs_hbm, dst_ref, dst_s_ref, zeros, sem,
                     *, tmd, n_p_steps, bs, n_sub, rb, n_blk):
    i = pl.program_id(0)

    def scatter_tokens(src_ref, e_ref, r_ref, d_ref, n_tok):
        def issue(t, carry):
            src = src_ref.at[pl.ds(pl.multiple_of(t * n_sub, n_sub), n_sub), :]
            for k in range(TOP_K):
                dst_row = (start_ref[e_ref[k, t]] + r_ref[k, t]) * n_sub
                d_ref[k, t] = dst_row
                pltpu.make_async_copy(
                    src, xs_hbm.at[pl.ds(pl.multiple_of(dst_row, n_sub), n_sub), :], sem).start()
            return carry
        lax.fori_loop(0, n_tok, issue, 0)
        done = xs_hbm.at[pl.ds(0, n_tok * TOP_K * n_sub), :]
        pltpu.make_async_copy(done, done, sem).wait()

    @pl.when(i < n_p_steps)
    def _():
        scatter_tokens(h2p_ref, eid_ref, rk_ref, dst_ref, tmd)

    @pl.when(i == n_p_steps)
    def _():
        scatter_tokens(h2s_ref, eid_s_ref, rk_s_ref, dst_s_ref, bs)
        zeros[...] = jnp.zeros(zeros.shape, F32)

        def zero_rows(first_row, n_rows):
            pltpu.make_async_copy(
                zeros.at[pl.ds(0, n_rows * n_sub), :],
                xs_hbm.at[pl.ds(pl.multiple_of(first_row * n_sub, n_sub), n_rows * n_sub), :],
                sem).start()

        def pad_expert(e, carry):
            n_pad = pad_ref[e] - cnt_ref[e]
            row = start_ref[e] + cnt_ref[e]
            piece = rb // 2
            while piece >= 1:
                take = (n_pad & piece) != 0

                @pl.when(take)
                def _(row=row, piece=piece):
                    zero_rows(row, piece)

                row = row + jnp.where(take, piece, 0)
                piece //= 2
            return carry

        lax.fori_loop(0, N_EXPERTS, pad_expert, 0)

        def pad_block(blk, carry):
            zero_rows(blk * rb, rb)
            return carry

        lax.fori_loop(n_used_ref[0], n_blk, pad_block, 0)
        n_zero = n_blk * rb - (n_p_steps * tmd + bs) * TOP_K
        done = xs_hbm.at[pl.ds(0, n_zero * n_sub), :]
        pltpu.make_async_copy(done, done, sem).wait()


def _dispatch_call(start, counts, padded, n_used, eid_p, rank_p, eid_s, rank_s, h2_p, h2_s,
                   n_blk, rb, tmd):
    n_p = eid_p.shape[1]
    bs = eid_s.shape[1]
    n_sub = h2_p.shape[0] // n_p
    n_p_steps = n_p // tmd
    kern = functools.partial(_dispatch_kernel, tmd=tmd, n_p_steps=n_p_steps, bs=bs, n_sub=n_sub,
                             rb=rb, n_blk=n_blk)
    last = n_p_steps - 1
    tile = pl.BlockSpec((TOP_K, tmd), lambda i, *_: (0, jnp.minimum(i, last)), memory_space=pltpu.SMEM)
    whole = pl.BlockSpec((TOP_K, bs), lambda i, *_: (0, 0), memory_space=pltpu.SMEM)
    grid_spec = pltpu.PrefetchScalarGridSpec(
        num_scalar_prefetch=4,
        grid=(n_p_steps + 1,),
        in_specs=[tile, tile, whole, whole,
                  pl.BlockSpec((tmd * n_sub, LANES), lambda i, *_: (jnp.minimum(i, last), 0)),
                  pl.BlockSpec((bs * n_sub, LANES), lambda i, *_: (0, 0))],
        out_specs=[pl.BlockSpec(memory_space=pl.ANY), tile, whole],
        scratch_shapes=[pltpu.VMEM((rb * n_sub, LANES), F32), pltpu.SemaphoreType.DMA],
    )
    return pl.pallas_call(
        kern,
        grid_spec=grid_spec,
        out_shape=[jax.ShapeDtypeStruct((n_blk * rb * n_sub, LANES), F32),
                   jax.ShapeDtypeStruct((TOP_K, n_p), I32),
                   jax.ShapeDtypeStruct((TOP_K, bs), I32)],
        compiler_params=_cparams(1),
        name="moe_dispatch",
    )(start, counts, padded, n_used, eid_p, rank_p, eid_s, rank_s, h2_p, h2_s)


def _experts_kernel(blk_e_ref, n_used_ref, x_ref, w1_ref, w3_ref, w2_ref, y_ref, *, rb, n_sub):
    i = pl.program_id(0)

    @pl.when(i < n_used_ref[0])
    def _():
        x = jnp.concatenate([x_ref[pl.ds(s, rb, stride=n_sub), :] for s in range(n_sub)], axis=1)
        hid = _silu(_mm(x, w1_ref[0])) * _mm(x, w3_ref[0])
        y = _mm(hid, w2_ref[0])
        for s in range(n_sub):
            y_ref[pl.ds(s, rb, stride=n_sub), :] = y[:, s * LANES:(s + 1) * LANES]

    @pl.when(i >= n_used_ref[0])
    def _():
        y_ref[...] = jnp.zeros(y_ref.shape, F32)


def _experts_call(x_rows, blk_e, n_used, w1, w3, w2, rb):
    _, d, d_exp = w1.shape
    n_sub = d // LANES
    n_blk = x_rows.shape[0] // (rb * n_sub)
    kern = functools.partial(_experts_kernel, rb=rb, n_sub=n_sub)
    grid_spec = pltpu.PrefetchScalarGridSpec(
        num_scalar_prefetch=2,
        grid=(n_blk,),
        in_specs=[pl.BlockSpec((rb * n_sub, LANES), lambda i, be, nu: (jnp.minimum(i, nu[0] - 1), 0)),
                  pl.BlockSpec((1, d, d_exp), lambda i, be, nu: (be[i], 0, 0)),
                  pl.BlockSpec((1, d, d_exp), lambda i, be, nu: (be[i], 0, 0)),
                  pl.BlockSpec((1, d_exp, d), lambda i, be, nu: (be[i], 0, 0))],
        out_specs=pl.BlockSpec((rb * n_sub, LANES), lambda i, be, nu: (i, 0)),
    )
    return pl.pallas_call(
        kern,
        grid_spec=grid_spec,
        out_shape=jax.ShapeDtypeStruct((n_blk * rb * n_sub, LANES), F32),
        compiler_params=_cparams(1),
        name="routed_experts",
    )(blk_e, n_used, x_rows, w1, w3, w2)


def _combine_kernel(rows_ref, rows_next_ref, y_hbm, ew_ref, x1_ref,
                    ysh_ref, gt2_ref, gf_ref, o_ref, buf, sem, *, tm, n_sub, nt, n_steps):
    step = pl.program_id(0) * nt + pl.program_id(1)
    slot = lax.rem(step, 2)

    def gather(r_ref, to_slot):
        def issue(t, carry):
            for k in range(TOP_K):
                _start_tile_copy(y_hbm, r_ref[k, t], buf, to_slot, (k * tm + t) * n_sub, sem, n_sub)
            return carry
        lax.fori_loop(0, tm, issue, 0)

    @pl.when(step == 0)
    def _():
        gather(rows_ref, 0)

    @pl.when(step + 1 < n_steps)
    def _():
        gather(rows_next_ref, 1 - slot)

    _wait_slot(buf, sem, slot)
    ew = ew_ref[...]
    cols = []
    for s in range(n_sub):
        acc = None
        for k in range(TOP_K):
            term = ew[:, k:k + 1] * buf[slot, pl.ds(k * tm * n_sub + s, tm, stride=n_sub), :]
            acc = term if acc is None else acc + term
        cols.append(acc)
    y_routed = jnp.concatenate(cols, axis=1)
    x2 = x1_ref[0] + gt2_ref[0] * (y_routed + ysh_ref[0])
    o_ref[0] = _rms(x2, gf_ref[...])


def _combine_call(dst_rows, y_rows, ew, x1, ysh, gt2, g_final, tm):
    bx, tx, d = x1.shape
    nt = tx // tm
    n_sub = d // LANES
    n_steps = bx * nt
    kern = functools.partial(_combine_kernel, tm=tm, n_sub=n_sub, nt=nt, n_steps=n_steps)
    row = pl.BlockSpec((1, tm, d), lambda b, i: (b, i, 0))
    cur = pl.BlockSpec((TOP_K, tm), lambda b, i: (0, b * nt + i), memory_space=pltpu.SMEM)
    nxt = pl.BlockSpec((TOP_K, tm), lambda b, i: (0, jnp.minimum(b * nt + i + 1, n_steps - 1)),
                       memory_space=pltpu.SMEM)
    return pl.pallas_call(
        kern,
        grid=(bx, nt),
        in_specs=[cur, nxt,
                  pl.BlockSpec(memory_space=pl.ANY),
                  pl.BlockSpec((tm, TOP_K), lambda b, i: (b * nt + i, 0)),
                  row, row, _mod_spec(gt2, tm, d),
                  pl.BlockSpec((1, d), lambda b, i: (0, 0))],
        out_specs=row,
        out_shape=jax.ShapeDtypeStruct((bx, tx, d), F32),
        scratch_shapes=[pltpu.VMEM((2, TOP_K * tm * n_sub, LANES), F32),
                        pltpu.SemaphoreType.DMA((2,))],
        compiler_params=_cparams(2),
        name="combine_final",
    )(dst_rows, dst_rows, y_rows, ew, x1, ysh, gt2, g_final.reshape(1, d))


def _compress_sample_kernel(pt_ref, cache_hbm, w1k_ref, w1v_ref, fs_ref, raw, kbuf, vbuf, sem,
                            *, pp, page_rows, n_steps):
    step = pl.program_id(0) * pl.num_programs(1) + pl.program_id(1)
    slot = lax.rem(step, 2)

    def fetch(of_step, to_slot):
        def issue(p, carry):
            page = pt_ref[of_step * pp + p]
            pltpu.make_async_copy(cache_hbm.at[page, pl.ds(0, 2)], raw.at[to_slot, p],
                                  sem.at[to_slot]).start()
            return carry
        lax.fori_loop(0, pp, issue, 0)

    @pl.when(step == 0)
    def _():
        fetch(0, 0)

    @pl.when(step + 1 < n_steps)
    def _():
        fetch(step + 1, 1 - slot)

    _wait_slot(raw, sem, slot)
    per_page = page_rows // CMP_STRIDE
    n_chunks = pp * per_page
    out_row = lax.broadcasted_iota(I32, (page_rows, page_rows), 0)
    in_row = lax.broadcasted_iota(I32, (page_rows, page_rows), 1)
    regroup = jnp.where(in_row == lax.rem(out_row, per_page) * CMP_STRIDE + lax.div(out_row, per_page),
                        1.0, 0.0).astype(MXU_DTYPE)

    def to_rows(p, carry):
        for which, buf in ((0, kbuf), (1, vbuf)):
            rows = _mm_nt(regroup, raw[slot, p, which].reshape(LANES, page_rows))
            for l in range(CMP_STRIDE):
                dst = pl.multiple_of(l * n_chunks + p * per_page, per_page)
                buf[pl.ds(dst, per_page), :] = rows[l * per_page:(l + 1) * per_page]
        return carry

    lax.fori_loop(0, pp, to_rows, 0, unroll=4)
    rows_of = lambda buf: (lambda l: buf[pl.ds(l * n_chunks, n_chunks), :])
    fs_ref[0] = _cmp_first_layer(rows_of(kbuf), rows_of(vbuf), w1k_ref, w1v_ref)


def _compress_sample_call(page_table, cache5, w1k, w1v, pp):
    bs, n_pages = page_table.shape
    _, _, n_g, hd, page_rows = cache5.shape
    assert n_g * hd == LANES
    n_chunks = pp * page_rows // CMP_STRIDE
    spb = n_pages // pp
    kern = functools.partial(_compress_sample_kernel, pp=pp, page_rows=page_rows, n_steps=bs * spb)
    buf = pltpu.VMEM((pp * page_rows, LANES), F32)
    grid_spec = pltpu.PrefetchScalarGridSpec(
        num_scalar_prefetch=1,
        grid=(bs, spb),
        in_specs=[pl.BlockSpec(memory_space=pl.ANY),
                  pl.BlockSpec(w1k.shape, lambda b, i, pt: (0, 0)),
                  pl.BlockSpec(w1v.shape, lambda b, i, pt: (0, 0))],
        out_specs=pl.BlockSpec((1, n_chunks, 8 * CMP_HID), lambda b, i, pt: (b, i, 0)),
        scratch_shapes=[pltpu.VMEM((2, pp, 2, n_g, hd, page_rows), F32), buf, buf,
                        pltpu.SemaphoreType.DMA((2,))],
    )
    return pl.pallas_call(
        kern,
        grid_spec=grid_spec,
        out_shape=jax.ShapeDtypeStruct((bs, spb * n_chunks, 8 * CMP_HID), F32),
        compiler_params=_cparams(2),
        name="compress_sample",
    )(page_table.reshape(-1), cache5, w1k, w1v)


def _head_slopes(n_rows):
    head = lax.broadcasted_iota(I32, (n_rows, 1), 0)
    slopes = jnp.zeros((n_rows, 1), F32)
    for h in range(N_HEADS):
        slopes = jnp.where(head == h, _slope(h), slopes)
    return slopes


def _attn_sample_a_kernel(q_ref, fs_ref, pe_ref, w1pe_ref, w2bd_ref, ocmp_ref, idx_ref,
                          *, t_pos, n_blk, n_blk_pad):
    hd = HEAD_DIM
    f = fs_ref[0]
    n_c = f.shape[0]
    half = 4 * CMP_HID
    second_next = pltpu.roll(f[:, half:], n_c - 1, 0)
    kcvc = _cmp_second_layer(f[:, :half], second_next, pe_ref, w1pe_ref, w2bd_ref)
    q = q_ref[0]
    c_idx = lax.broadcasted_iota(I32, (1, n_c), 1)
    dist_c = t_pos - (c_idx * CMP_STRIDE + (CMP_BLOCK - 1))
    mask_c = dist_c >= 0
    slopes = _head_slopes(N_HEADS)
    head_grp = lax.div(lax.broadcasted_iota(I32, (N_HEADS, 1), 0), Q_PER_KV)
    per = SEL_BLOCK // CMP_STRIDE
    c_row = lax.broadcasted_iota(I32, (n_c, n_blk_pad), 0)
    lo = lax.broadcasted_iota(I32, (n_c, n_blk_pad), 1) * per
    spread = jnp.where((c_row >= lo) & (c_row < lo + per - 1), 1.0, 0.0) \
        + jnp.where((c_row == lo - 1) | (c_row == lo + per - 1), 0.5, 0.0)
    blk_lane = lax.broadcasted_iota(I32, (1, n_blk_pad), 1)
    cur = t_pos // SEL_BLOCK
    forced = (blk_lane == 0) | (blk_lane == cur) | (blk_lane == cur - 1)
    in_range = (blk_lane * SEL_BLOCK <= t_pos) & (blk_lane < n_blk)
    n_sq = (n_blk_pad, n_blk_pad)
    sub_id = lax.broadcasted_iota(I32, n_sq, 0)
    lane_id = lax.broadcasted_iota(I32, n_sq, 1)
    top_n = min(TOP_N, n_blk)
    o_cmp = jnp.zeros((N_HEADS, hd), F32)
    idx_rows = []
    for g in range(KV_HEADS):
        k_cmp = kcvc[:, g * hd:(g + 1) * hd]
        v_cmp = kcvc[:, (2 + g) * hd:(3 + g) * hd]
        s = _mm_nt(q, k_cmp) * ATTN_SCALE - slopes * dist_c.astype(F32)
        m = jnp.max(jnp.where(mask_c, s, NEG_INF), axis=1, keepdims=True)
        p = jnp.where(mask_c, jnp.exp(s - m), 0.0)
        l = jnp.sum(p, axis=1, keepdims=True)
        p = p / jnp.where(l > 0.0, l, 1.0)
        o_cmp = jnp.where(head_grp == g, _mm(p, v_cmp), o_cmp)
        p_sum = jnp.sum(jnp.where(head_grp == g, p, 0.0), axis=0, keepdims=True)
        imp = jnp.dot(jnp.broadcast_to(p_sum, (SUBLANES, n_c)), spread,
                      precision=lax.Precision.HIGHEST, preferred_element_type=F32)[0:1]
        imp = jnp.where(forced, FORCE_SCORE, imp)
        imp = jnp.where(in_range, imp, -jnp.inf)
        imp_col = jnp.transpose(jnp.broadcast_to(imp, (LANES, n_blk_pad)))[:, 0:1]
        beats = (imp > imp_col) | ((imp == imp_col) & (lane_id < sub_id))
        rank_col = jnp.sum(jnp.where(beats, 1.0, 0.0), axis=1, keepdims=True)
        sel_col = jnp.where(rank_col < float(top_n), 1.0, 0.0)
        before = jnp.dot(jnp.where(lane_id < sub_id, 1.0, 0.0).astype(BF16),
                         jnp.broadcast_to(sel_col, (n_blk_pad, LANES)).astype(BF16),
                         preferred_element_type=F32)
        slot_lane = lax.broadcasted_iota(I32, (n_blk_pad, LANES), 1).astype(F32)
        blk_sub = lax.broadcasted_iota(I32, (n_blk_pad, LANES), 0).astype(F32)
        hit = (sel_col > 0.5) & (before == slot_lane)
        idx_rows.append(jnp.sum(jnp.where(hit, blk_sub, 0.0), axis=0, keepdims=True))
    ocmp_ref[0] = o_cmp
    pad = jnp.zeros((SUBLANES - KV_HEADS, LANES), F32)
    idx_ref[0] = jnp.concatenate(idx_rows + [pad], axis=0).astype(I32)


def _attn_sample_a_call(q_heads, fs, cw, t_pos, n_blk):
    bs = q_heads.shape[0]
    n_blk_pad = -(-n_blk // LANES) * LANES
    kern = functools.partial(_attn_sample_a_kernel, t_pos=t_pos, n_blk=n_blk, n_blk_pad=n_blk_pad)
    full = lambda a: pl.BlockSpec(a.shape, lambda b: (0,) * a.ndim)
    per_b = lambda a: pl.BlockSpec((1,) + a.shape[1:], lambda b: (b,) + (0,) * (a.ndim - 1))
    return pl.pallas_call(
        kern,
        grid=(bs,),
        in_specs=[per_b(q_heads), per_b(fs), full(cw["pe"]), full(cw["w1pe"]), full(cw["w2bd"])],
        out_specs=[pl.BlockSpec((1, N_HEADS, HEAD_DIM), lambda b: (b, 0, 0)),
                   pl.BlockSpec((1, SUBLANES, LANES), lambda b: (b, 0, 0))],
        out_shape=[jax.ShapeDtypeStruct((bs, N_HEADS, HEAD_DIM), F32),
                   jax.ShapeDtypeStruct((bs, SUBLANES, LANES), I32)],
        compiler_params=_cparams(1),
        name="attn_sample_select",
    )(q_heads, fs, cw["pe"], cw["w1pe"], cw["w2bd"])


def _attn_sample_b_kernel(sel_ref, pt_ref, q_ref, cache_hbm, kvn_ref, win_ref, gate_ref, ocmp_ref,
                          o_ref, kbuf, vbuf, sem,
                          *, t_pos, top_n, n_past_blk, n_pages, per_page, n_steps):
    b = pl.program_id(0)
    slot = lax.rem(b, 2)
    hd = HEAD_DIM
    page_rows = kbuf.shape[-1]

    def fetch(of_b, to_slot):
        for g in range(KV_HEADS):
            def issue(i, carry, g=g):
                blk = jnp.minimum(sel_ref[(of_b * KV_HEADS + g) * top_n + i], n_past_blk - 1)
                page = pt_ref[of_b * n_pages + blk // per_page]
                pltpu.make_async_copy(cache_hbm.at[page, 2, g], kbuf.at[to_slot, g, i],
                                      sem.at[to_slot]).start()
                pltpu.make_async_copy(cache_hbm.at[page, 3, g], vbuf.at[to_slot, g, i],
                                      sem.at[to_slot]).start()
                return carry
            lax.fori_loop(0, top_n, issue, 0)

    @pl.when(b == 0)
    def _():
        fetch(0, 0)

    @pl.when(b + 1 < n_steps)
    def _():
        fetch(b + 1, 1 - slot)

    _wait_slot(kbuf, sem, slot)
    _wait_slot(vbuf, sem, slot)
    q = q_ref[0]
    qf = q.astype(F32)
    slopes = _head_slopes(N_HEADS)
    kvn = kvn_ref[0]
    rnd = lambda a: a.astype(MXU_DTYPE).astype(F32)
    new_col = lambda c: rnd(kvn[:, c * hd:(c + 1) * hd])
    gates = gate_ref[0]
    head_grp = lax.div(lax.broadcasted_iota(I32, (N_HEADS, 1), 0), Q_PER_KV)
    lane = lax.broadcasted_iota(I32, (1, page_rows), 1)
    out = jnp.zeros((N_HEADS, hd), F32)
    for g in range(KV_HEADS):
        scores = []
        for i in range(top_n):
            blk = sel_ref[(b * KV_HEADS + g) * top_n + i]
            k_pos = (blk // per_page) * page_rows + lane
            picked = (lax.div(lane, SEL_BLOCK) == lax.rem(blk, per_page)) & (blk < n_past_blk)
            s = _mm(q, kbuf[slot, g, i]) * ATTN_SCALE - slopes * (t_pos - k_pos).astype(F32)
            scores.append(jnp.where(picked, s, NEG_INF))
        s_all = jnp.concatenate(scores, axis=1)
        s_new = jnp.sum(qf * new_col(4 + g), axis=1, keepdims=True) * ATTN_SCALE
        m = jnp.maximum(jnp.max(s_all, axis=1, keepdims=True), s_new)
        p_all = jnp.exp(s_all - m)
        p_new = jnp.exp(s_new - m)
        acc = p_new * new_col(6 + g)
        for i in range(top_n):
            acc = acc + _mm_nt(p_all[:, i * page_rows:(i + 1) * page_rows], vbuf[slot, g, i])
        o_sel = acc / (jnp.sum(p_all, axis=1, keepdims=True) + p_new)
        w_rows = win_ref.shape[-1]
        dist_w = (w_rows - lax.broadcasted_iota(I32, (1, w_rows), 1)).astype(F32)
        s_w = _mm(q, win_ref[0, 0, g]) * ATTN_SCALE - slopes * dist_w
        s_w = jnp.where(dist_w < float(WINDOW), s_w, NEG_INF)
        sw_new = jnp.sum(qf * new_col(8 + g), axis=1, keepdims=True) * ATTN_SCALE
        m_w = jnp.maximum(jnp.max(s_w, axis=1, keepdims=True), sw_new)
        p_w = jnp.exp(s_w - m_w)
        pw_new = jnp.exp(sw_new - m_w)
        o_win = (_mm_nt(p_w, win_ref[0, 1, g]) + pw_new * new_col(10 + g)) \
            / (jnp.sum(p_w, axis=1, keepdims=True) + pw_new)
        o = gates[:, 0:1] * ocmp_ref[0] + gates[:, 1:2] * o_sel + gates[:, 2:3] * o_win
        out = jnp.where(head_grp == g, o, out)
    o_ref[0] = out.astype(o_ref.dtype)


def _attn_sample_b_call(sel_idx, page_table, q_heads, cache5, kv_new, win5, gates3, o_cmp,
                        t_pos, top_n):
    bs, n_pages = page_table.shape
    _, _, n_g, hd, page_rows = cache5.shape
    per_page = page_rows // SEL_BLOCK
    n_past_blk = n_pages * per_page
    kern = functools.partial(_attn_sample_b_kernel, t_pos=t_pos, top_n=top_n, n_past_blk=n_past_blk,
                             n_pages=n_pages, per_page=per_page, n_steps=bs)
    per_b = lambda a: pl.BlockSpec((1,) + a.shape[1:], lambda b, sel, pt: (b,) + (0,) * (a.ndim - 1))
    tiles = pltpu.VMEM((2, n_g, top_n, hd, page_rows), F32)
    grid_spec = pltpu.PrefetchScalarGridSpec(
        num_scalar_prefetch=2,
        grid=(bs,),
        in_specs=[per_b(q_heads), pl.BlockSpec(memory_space=pl.ANY),
                  per_b(kv_new), per_b(win5), per_b(gates3), per_b(o_cmp)],
        out_specs=pl.BlockSpec((1, N_HEADS, HEAD_DIM), lambda b, sel, pt: (b, 0, 0)),
        scratch_shapes=[tiles, tiles, pltpu.SemaphoreType.DMA((2,))],
    )
    return pl.pallas_call(
        kern,
        grid_spec=grid_spec,
        out_shape=jax.ShapeDtypeStruct((bs, N_HEADS, HEAD_DIM), MXU_DTYPE),
        compiler_params=_cparams(1),
        name="attn_sample_gather",
    )(sel_idx, page_table.reshape(-1), q_heads, cache5, kv_new, win5, gates3, o_cmp)


def _prep_weights(w_in, w_a, w_b, w_out, router_w, router_b, sh_w1, sh_w3, sh_w2, g_norm1, g_norm2):
    d = w_in.shape[0]
    n_q = N_HEADS * HEAD_DIM
    n_kv = 6 * KV_HEADS * HEAD_DIM
    n_gate = 3 * N_HEADS
    o_gate = n_q + n_kv
    o_glu = o_gate + n_gate
    o_mrg = o_glu + d
    c = lambda a: a.astype(MXU_DTYPE)
    w_cat = jnp.concatenate(
        [w_in[:, :o_gate], w_in[:, o_glu:o_mrg],
         jnp.pad(w_in[:, o_gate:o_glu], ((0, 0), (0, LANES - n_gate)))], axis=1)
    mw = {"g1": g_norm1.reshape(1, d), "wm": c(w_in[:, o_mrg:]), "wa": c(w_a), "wb": c(w_b),
          "wo": c(w_out), "g2": g_norm2.reshape(1, d), "rwt": c(router_w.T),
          "rb": router_b.reshape(-1, 1), "s13": c(jnp.concatenate([sh_w1, sh_w3], axis=1)),
          "s2": c(sh_w2)}
    return c(w_cat), mw


def kernel(x_prompt, x_sample, cache_kv, cache_win, state_conv, page_table, c_prompt, c_sample, w_ada, b_ada, g_norm1, w_in, cmp_pe, cmp_w1, cmp_w2, conv_w, conv_b, conv_ln_g, conv_ln_b, w_a, w_b, w_out, g_norm2, router_w, router_b, exp_w1, exp_w3, exp_w2, sh_w1, sh_w3, sh_w2, g_final):
    l = 0
    assert w_ada.shape[0] == 1 and x_sample.shape[1] == 1 and cache_win.shape[2] == WINDOW
    bp, sp, d = x_prompt.shape
    bs = x_sample.shape[0]
    n_pool, page_rows = cache_kv.shape[1], cache_kv.shape[2]
    n_pages = page_table.shape[1]
    past_len = n_pages * page_rows
    hd = HEAD_DIM
    c_conv = d // 2
    n_sub = d // LANES
    n_kv_new = 4 * KV_HEADS * hd
    rb = 512

    mod = _ada_call(jnp.concatenate([c_prompt, c_sample], axis=0), w_ada[l], b_ada[l])
    mods_p = [mod[:bp, None, k * d:(k + 1) * d] for k in range(6)]
    mods_s = [mod[None, bp:, k * d:(k + 1) * d] for k in range(6)]
    w_cat, mw = _prep_weights(w_in[l], w_a[l], w_b[l], w_out[l], router_w[l], router_b[l],
                              sh_w1[l], sh_w3[l], sh_w2[l], g_norm1[l], g_norm2[l])
    cw = _compress_weights(cmp_pe[l], cmp_w1[l], cmp_w2[l])

    sh1, sc1, gt1, sh2, sc2, gt2 = mods_p
    q, kvw, glu, gates = _inproj_call(x_prompt, sc1, sh1, g_norm1[l], w_cat, min(sp, 512))
    kcvc = _compress_prompt_call(kvw, cw)
    o_attn = _attn_prompt_call(q, kvw, kcvc, gates, 128, 512 if sp % 512 == 0 else 256)
    z = _conv_prompt_call(glu, conv_w[l], conv_b[l], conv_ln_g[l], conv_ln_b[l], 64)
    x1_p, h2_p, ysh_p, eid_p, ew_p, rank_p, cnt_p = _merge_call(
        x_prompt, o_attn, z, (sc1, sh1, gt1, sc2, sh2), mw, min(sp, 256))

    sh1s, sc1s, gt1s, sh2s, sc2s, gt2s = mods_s
    xs = x_sample.reshape(1, bs, d)
    q_s, kvw_s, glu_s, gates_s = _inproj_call(xs, sc1s, sh1s, g_norm1[l], w_cat, bs)
    cache5 = jnp.transpose(cache_kv[l], (0, 2, 3, 4, 1))
    win5 = jnp.transpose(cache_win[l], (0, 2, 3, 4, 1))
    fs = _compress_sample_call(page_table, cache5, cw["w1k"], cw["w1v"], min(n_pages, 32))
    q_heads = q_s.reshape(bs, N_HEADS, hd)
    n_blk = -(-(past_len + 1) // SEL_BLOCK)
    top_n = min(TOP_N, n_blk)
    o_cmp_s, sel = _attn_sample_a_call(q_heads, fs, cw, past_len, n_blk)
    o_s = _attn_sample_b_call(
        sel[:, :KV_HEADS, :top_n].reshape(-1), page_table, q_heads, cache5,
        kvw_s.reshape(bs, 1, -1), win5,
        gates_s[0, :, :3 * N_HEADS].reshape(bs, N_HEADS, 3), o_cmp_s, past_len, top_n)
    glu_new = glu_s.reshape(bs, 1, c_conv)
    z_s = _conv_sample_call(jnp.transpose(state_conv[l], (1, 0, 2)), glu_s[0],
                            conv_w[l], conv_b[l], conv_ln_g[l], conv_ln_b[l])
    x1_s, h2_s, ysh_s, eid_s, ew_s, rank_s, cnt_s = _merge_call(
        xs, o_s.reshape(1, bs, N_HEADS * hd), z_s.reshape(1, bs, c_conv),
        (sc1s, sh1s, gt1s, sc2s, sh2s), mw, bs)

    n_p = bp * sp
    cnt_p = cnt_p[:, 0].astype(I32)
    cnt_s = cnt_s[:, 0].astype(I32)
    counts = cnt_p + cnt_s
    rank_s = rank_s + cnt_p[eid_s]
    start, padded, blk_e, n_used, n_blk = _moe_plan(counts, (n_p + bs) * TOP_K, rb)
    x_rows, dst_p, dst_s = _dispatch_call(start, counts, padded, n_used, eid_p, rank_p, eid_s,
                                          rank_s, h2_p, h2_s, n_blk, rb, min(n_p, 512))
    y_rows = _experts_call(x_rows, blk_e, n_used, exp_w1[l], exp_w3[l], exp_w2[l], rb)
    y_prompt = _combine_call(dst_p, y_rows, ew_p.T, x1_p, ysh_p, gt2, g_final, min(sp, 128))
    y_sample = _combine_call(dst_s, y_rows, ew_s.T, x1_s, ysh_s, gt2s, g_final, bs)

    w_keep = min(WINDOW, sp)
    kv_prompt = kvw[:, :, :n_kv_new].reshape(1, bp, sp, 4, KV_HEADS, hd)
    kv_sample = kvw_s[0, :, :n_kv_new].reshape(1, bs, 1, 4, KV_HEADS, hd)
    win_prompt = kvw[:, sp - w_keep:, n_kv_new:].reshape(1, bp, w_keep, 2, KV_HEADS, hd)
    win_new = kvw_s[0, :, n_kv_new:].reshape(bs, 1, 2, KV_HEADS, hd)
    win_sample = jnp.concatenate([cache_win[l][:, 1:], win_new], axis=1)[None]
    conv_prompt = glu[:, sp - (CONV_W - 1):][None]
    conv_sample = jnp.concatenate([state_conv[l][:, 1:], glu_new], axis=1)[None]
    return (y_prompt, y_sample.reshape(bs, 1, d), kv_prompt, kv_sample, win_prompt, win_sample,
            conv_prompt, conv_sample)
```

```python
import functools

import jax
import jax.numpy as jnp
from jax import lax
from jax.experimental import pallas as pl
from jax.experimental.pallas import tpu as pltpu

F32 = jnp.float32
BF16 = jnp.bfloat16
I32 = jnp.int32
MXU_DTYPE = jnp.bfloat16

N_HEADS = 8
KV_HEADS = 2
Q_PER_KV = N_HEADS // KV_HEADS
HEAD_DIM = 64
CMP_BLOCK = 32
CMP_STRIDE = 16
CMP_HID = 2 * HEAD_DIM
SEL_BLOCK = 64
TOP_N = 16
WINDOW = 512
CONV_W = 31
N_EXPERTS = 256
N_GROUPS = 8
TOPK_GROUPS = 4
TOP_K = 8
ROUTED_SCALE = 2.5
EPS = 1e-6
NEG_INF = -1e30
FORCE_SCORE = 1e4
ATTN_SCALE = HEAD_DIM ** -0.5

LANES = 128
SUBLANES = 8
VMEM_LIMIT = 56 * 1024 * 1024


def _cparams(n_axes):
    return pltpu.CompilerParams(
        dimension_semantics=("arbitrary",) * n_axes, vmem_limit_bytes=VMEM_LIMIT)


def _mm(a, b):
    return jnp.dot(a.astype(MXU_DTYPE), b.astype(MXU_DTYPE), preferred_element_type=F32)


def _mm_nt(a, b):
    return lax.dot_general(a.astype(MXU_DTYPE), b.astype(MXU_DTYPE),
                           (((1,), (1,)), ((), ())), preferred_element_type=F32)


def _mm_tn(a, b):
    return lax.dot_general(a.astype(MXU_DTYPE), b.astype(MXU_DTYPE),
                           (((0,), (0,)), ((), ())), preferred_element_type=F32)


def _sigmoid(x):
    return 1.0 / (1.0 + jnp.exp(-x))


def _silu(x):
    return x * _sigmoid(x)


def _rms(x, g):
    return x * lax.rsqrt(jnp.mean(x * x, axis=-1, keepdims=True) + EPS) * g


def _slope(head):
    return 2.0 ** (-8.0 * (head + 1) / N_HEADS)


def _ada_kernel(c_ref, w_ref, b_ref, o_ref):
    o_ref[...] = _mm(_silu(c_ref[...]), w_ref[...]) + b_ref[...]


def _ada_call(c, w_ada, b_ada):
    n, d = c.shape
    n_out = w_ada.shape[1]
    tn = n_out // 6
    return pl.pallas_call(
        _ada_kernel,
        grid=(n_out // tn,),
        in_specs=[pl.BlockSpec((n, d), lambda j: (0, 0)),
                  pl.BlockSpec((d, tn), lambda j: (0, j)),
                  pl.BlockSpec((1, tn), lambda j: (0, j))],
        out_specs=pl.BlockSpec((n, tn), lambda j: (0, j)),
        out_shape=jax.ShapeDtypeStruct((n, n_out), F32),
        compiler_params=_cparams(1),
        name="ada_mod",
    )(c, w_ada, b_ada.reshape(1, n_out))


def _inproj_kernel(x_ref, sc_ref, sh_ref, g_ref, w_ref, q_ref, kv_ref, glu_ref, gate_ref,
                   *, n_q, n_kv, c_conv):
    x = x_ref[0]
    h = _rms(x, g_ref[...]) * (1.0 + sc_ref[0]) + sh_ref[0]
    y = _mm(h, w_ref[...])
    o = 0
    q_ref[0] = y[:, o:o + n_q].astype(q_ref.dtype)
    o += n_q
    kv_ref[0] = y[:, o:o + n_kv]
    o += n_kv
    u_a = y[:, o:o + c_conv]
    u_g = y[:, o + c_conv:o + 2 * c_conv]
    glu_ref[0] = u_a * _sigmoid(u_g)
    o += 2 * c_conv
    gate_ref[0] = _sigmoid(y[:, o:o + LANES])


def _mod_spec(mod, tm, d):
    if mod.shape[1] == 1:
        return pl.BlockSpec((1, 1, d), lambda b, i: (b, 0, 0))
    return pl.BlockSpec((1, tm, d), lambda b, i: (b, i, 0))


def _inproj_call(x, sc1, sh1, g1, w_cat, tm):
    bx, tx, d = x.shape
    n_q = N_HEADS * HEAD_DIM
    n_kv = 6 * KV_HEADS * HEAD_DIM
    c_conv = d // 2
    n_cat = w_cat.shape[1]
    kern = functools.partial(_inproj_kernel, n_q=n_q, n_kv=n_kv, c_conv=c_conv)
    row = lambda w: pl.BlockSpec((1, tm, w), lambda b, i: (b, i, 0))
    return pl.pallas_call(
        kern,
        grid=(bx, tx // tm),
        in_specs=[row(d), _mod_spec(sc1, tm, d), _mod_spec(sh1, tm, d),
                  pl.BlockSpec((1, d), lambda b, i: (0, 0)),
                  pl.BlockSpec((d, n_cat), lambda b, i: (0, 0))],
        out_specs=[row(n_q), row(n_kv), row(c_conv), row(LANES)],
        out_shape=[jax.ShapeDtypeStruct((bx, tx, n_q), MXU_DTYPE),
                   jax.ShapeDtypeStruct((bx, tx, n_kv), F32),
                   jax.ShapeDtypeStruct((bx, tx, c_conv), F32),
                   jax.ShapeDtypeStruct((bx, tx, LANES), F32)],
        compiler_params=_cparams(2),
        name="in_proj",
    )(x, sc1, sh1, g1.reshape(1, d), w_cat)


def _cmp_first_layer(k_of_l, v_of_l, w1k_ref, w1v_ref):
    half = KV_HEADS * CMP_HID
    f_k = _mm(jnp.concatenate([k_of_l(l) for l in range(CMP_STRIDE)], axis=1), w1k_ref[...])
    f_v = _mm(jnp.concatenate([v_of_l(l) for l in range(CMP_STRIDE)], axis=1), w1v_ref[...])
    return jnp.concatenate([f_k[:, :half], f_v[:, :half], f_k[:, half:], f_v[:, half:]], axis=1)


def _cmp_second_layer(first, second_next, pe_ref, w1pe_ref, w2bd_ref):
    pe_term = _mm(pe_ref[...], w1pe_ref[...])[0:1]
    return _mm(_silu(first + second_next + pe_term), w2bd_ref[...])


def _compress_prompt_kernel(k_ref, v_ref, w1k_ref, w1v_ref, pe_ref, w1pe_ref, w2bd_ref, o_ref, fs_ref,
                            *, n_chunks):
    half = 4 * CMP_HID
    rows_of = lambda ref: (lambda l: ref.at[0][pl.ds(l, n_chunks, stride=CMP_STRIDE), :])
    fs_ref[pl.ds(0, n_chunks), :] = _cmp_first_layer(rows_of(k_ref), rows_of(v_ref), w1k_ref, w1v_ref)
    fs_ref[pl.ds(n_chunks, SUBLANES), :] = jnp.zeros((SUBLANES, 2 * half), F32)
    first = fs_ref[pl.ds(0, n_chunks), pl.ds(0, half)]
    second_next = fs_ref[pl.ds(1, n_chunks), pl.ds(half, half)]
    o_ref[0] = _cmp_second_layer(first, second_next, pe_ref, w1pe_ref, w2bd_ref)


def _compress_prompt_call(kvw, cw):
    b, t, _ = kvw.shape
    n_chunks = t // CMP_STRIDE
    wid = 4 * HEAD_DIM
    kern = functools.partial(_compress_prompt_kernel, n_chunks=n_chunks)
    full = lambda a: pl.BlockSpec(a.shape, lambda i: (0,) * a.ndim)
    return pl.pallas_call(
        kern,
        grid=(b,),
        in_specs=[pl.BlockSpec((1, t, LANES), lambda i: (i, 0, 0)),
                  pl.BlockSpec((1, t, LANES), lambda i: (i, 0, 1)),
                  full(cw["w1k"]), full(cw["w1v"]), full(cw["pe"]), full(cw["w1pe"]), full(cw["w2bd"])],
        out_specs=pl.BlockSpec((1, n_chunks, wid), lambda i: (i, 0, 0)),
        out_shape=jax.ShapeDtypeStruct((b, n_chunks, wid), F32),
        scratch_shapes=[pltpu.VMEM((n_chunks + SUBLANES, 8 * CMP_HID), F32)],
        compiler_params=_cparams(1),
        name="compress_prompt",
    )(kvw, kvw, cw["w1k"], cw["w1v"], cw["pe"], cw["w1pe"], cw["w2bd"])


def _compress_weights(cmp_pe, cmp_w1, cmp_w2):
    hd, hid = HEAD_DIM, CMP_HID
    slab_kv = (0, 0, 1, 1)

    def first_layer(kv):
        w = jnp.zeros((CMP_STRIDE, KV_HEADS, hd, 2, KV_HEADS, hid), F32)
        for g in range(KV_HEADS):
            w = w.at[:, g, :, 0, g, :].set(cmp_w1[kv, :CMP_STRIDE])
            w = w.at[:, g, :, 1, g, :].set(cmp_w1[kv, CMP_STRIDE:])
        return w.reshape(CMP_STRIDE * KV_HEADS * hd, 2 * KV_HEADS * hid).astype(MXU_DTYPE)

    w2bd = jnp.zeros((4 * hid, 4 * hd), F32)
    for s, kv in enumerate(slab_kv):
        w2bd = w2bd.at[s * hid:(s + 1) * hid, s * hd:(s + 1) * hd].set(cmp_w2[kv])
    pe = jnp.broadcast_to(cmp_pe.reshape(1, -1), (SUBLANES, 2 * CMP_BLOCK * hd))
    w1pe = jnp.zeros((2 * CMP_BLOCK * hd, 4 * hid), F32)
    n_flat = CMP_BLOCK * hd
    for s, kv in enumerate(slab_kv):
        w1pe = w1pe.at[kv * n_flat:(kv + 1) * n_flat, s * hid:(s + 1) * hid].set(
            cmp_w1[kv].reshape(n_flat, hid))
    return {"w1k": first_layer(0), "w1v": first_layer(1), "w2bd": w2bd.astype(MXU_DTYPE),
            "pe": pe, "w1pe": w1pe.astype(MXU_DTYPE)}


def _softmax_cols(s, mask):
    m = jnp.max(jnp.where(mask, s, NEG_INF), axis=0, keepdims=True)
    p = jnp.where(mask, jnp.exp(s - m), 0.0)
    l = jnp.sum(p, axis=0, keepdims=True)
    return p / jnp.where(l > 0.0, l, 1.0)


def _top_n_rows(imp, top_n):
    n = imp.shape[0]
    row_id = lax.broadcasted_iota(I32, imp.shape, 0)
    beaten = jnp.zeros(imp.shape, F32)
    for m in range(n):
        other = imp[m:m + 1, :]
        wins = (other > imp) | ((other == imp) & (row_id > m))
        beaten = beaten + jnp.where(wins, 1.0, 0.0)
    return beaten < float(top_n)


SEL_FEAT0 = HEAD_DIM + SUBLANES
MASK_BIG = -2.0 ** 100


def _attn_prompt_kernel(q_ref, ksv_ref, kwv_ref, kc_ref, gate_ref, o_ref,
                        kaug, vsel, kwaug, vwin, psum_ref, *, tq, kc_len, t_len):
    qt = pl.program_id(1)
    q0 = qt * tq
    hd = HEAD_DIM
    n_cmp_rows = kc_ref.shape[1]
    n_blk = t_len // SEL_BLOCK
    top_n = min(TOP_N, n_blk)
    wk = min(WINDOW + tq, t_len)

    @pl.when(qt == 0)
    def _():
        lane = lax.broadcasted_iota(I32, (t_len, LANES), 1)
        pos = lax.broadcasted_iota(I32, (t_len, LANES), 0)
        pos_hi = lax.div(pos, SEL_BLOCK)
        alibi = jnp.where(lane < hd + 2, 1.0,
                          jnp.where(lane == hd + 2, pos_hi.astype(F32),
                                    jnp.where(lane == hd + 3, lax.rem(pos, SEL_BLOCK).astype(F32), 0.0)))
        in_blk = jnp.where(lane - SEL_FEAT0 == pos_hi, 1.0, 0.0)
        for g in range(KV_HEADS):
            k_s = ksv_ref[0, :, 0:LANES]
            k_w = kwv_ref[0, :, 0:LANES]
            if g:
                k_s = pltpu.roll(k_s, LANES - g * hd, 1)
                k_w = pltpu.roll(k_w, LANES - g * hd, 1)
            kaug[g] = jnp.where(lane < hd, k_s, alibi + in_blk).astype(kaug.dtype)
            kwaug[g] = jnp.where(lane < hd, k_w, alibi).astype(kwaug.dtype)
            v_s = ksv_ref[0, :, LANES:2 * LANES]
            v_w = kwv_ref[0, :, LANES:2 * LANES]
            if g:
                v_s = pltpu.roll(v_s, LANES - g * hd, 1)
                v_w = pltpu.roll(v_w, LANES - g * hd, 1)
            ones_lane = jnp.where(lane == hd, 1.0, 0.0)
            vsel[g] = jnp.where(lane < hd, v_s, ones_lane).astype(vsel.dtype)
            vwin[g] = jnp.where(lane < hd, v_w, ones_lane).astype(vwin.dtype)
        psum_ref[...] = jnp.zeros(psum_ref.shape, F32)

    q_blk = q_ref[0]
    gates = gate_ref[0]
    rows = Q_PER_KV * tq
    row_t = q0 + lax.rem(lax.broadcasted_iota(I32, (rows, 1), 0), tq)
    lane_q = lax.broadcasted_iota(I32, (tq, LANES), 1)
    t_q = q0 + lax.broadcasted_iota(I32, (tq, LANES), 0)
    t_hi = (lax.div(t_q, SEL_BLOCK) * SEL_BLOCK).astype(F32)
    t_lo = lax.rem(t_q, SEL_BLOCK).astype(F32)
    w0 = pl.multiple_of(jnp.maximum(q0 + tq - wk, 0), tq)
    dist_w = row_t - (w0 + lax.broadcasted_iota(I32, (rows, wk), 1))
    band = jnp.where((dist_w >= 0) & (dist_w < WINDOW), 0.0, NEG_INF)
    out_heads = []
    for g in range(KV_HEADS):
        q_heads = [q_blk[:, (g * Q_PER_KV + r) * hd:(g * Q_PER_KV + r + 1) * hd]
                   for r in range(Q_PER_KV)]
        k_cmp = kc_ref[0, :, g * hd:(g + 1) * hd]
        v_cmp = kc_ref[0, :, (2 + g) * hd:(3 + g) * hd]
        t_lane = q0 + lax.broadcasted_iota(I32, (n_cmp_rows, tq), 1)
        c_pos = lax.broadcasted_iota(I32, (n_cmp_rows, tq), 0) * CMP_STRIDE + (CMP_BLOCK - 1)
        dist_c = t_lane - c_pos
        mask_c = dist_c >= 0
        dist_cf = dist_c.astype(F32)
        o_cmp = []
        p_sum = None
        for r in range(Q_PER_KV):
            s = _mm_nt(k_cmp, q_heads[r]) * ATTN_SCALE - _slope(g * Q_PER_KV + r) * dist_cf
            p = _softmax_cols(s, mask_c)
            o_cmp.append(_mm_tn(p, v_cmp))
            p_sum = p if p_sum is None else p_sum + p
        psum_ref[pl.ds(SUBLANES, n_cmp_rows), :] = p_sum
        per = SEL_BLOCK // CMP_STRIDE
        taps = [psum_ref[pl.ds(SUBLANES - 1 + k, n_blk, stride=per), :] for k in range(per + 1)]
        imp = 0.5 * taps[0] + 0.5 * taps[per]
        for k in range(1, per):
            imp = imp + taps[k]
        blk = lax.broadcasted_iota(I32, (n_blk, tq), 0)
        t_blk = q0 + lax.broadcasted_iota(I32, (n_blk, tq), 1)
        cur = lax.div(t_blk, SEL_BLOCK)
        forced = (blk == 0) | (blk == cur) | (blk == cur - 1)
        imp = jnp.where(forced, FORCE_SCORE, imp)
        imp = jnp.where(blk * SEL_BLOCK <= t_blk, imp, -jnp.inf)
        sel_neg = jnp.where(_top_n_rows(imp, top_n), 0.0, MASK_BIG)
        sel_lanes = jnp.transpose(jnp.concatenate(
            [jnp.zeros((SEL_FEAT0, tq), F32), sel_neg,
             jnp.zeros((LANES - SEL_FEAT0 - n_blk, tq), F32)], axis=0))
        q_aug = []
        for r in range(Q_PER_KV):
            head = g * Q_PER_KV + r
            slope = _slope(head)
            pair = q_blk[:, (head // 2) * LANES:(head // 2 + 1) * LANES].astype(F32)
            if head % 2:
                pair = pltpu.roll(pair, LANES - hd, 1)
            feats = jnp.where(lane_q == hd, -slope * t_hi,
                              jnp.where(lane_q == hd + 1, -slope * t_lo,
                                        jnp.where(lane_q == hd + 2, slope * SEL_BLOCK,
                                                  jnp.where(lane_q == hd + 3, slope, sel_lanes))))
            q_aug.append(jnp.where(lane_q < hd, pair * ATTN_SCALE, feats).astype(MXU_DTYPE))
        q_aug = jnp.concatenate(q_aug, axis=0)

        def sel_chunk(j, carry, causal):
            m_run, acc = carry
            k0 = pl.multiple_of(j * kc_len, kc_len)
            s = _mm_nt(q_aug, kaug[g, pl.ds(k0, kc_len), :])
            if causal:
                k_pos = k0 + lax.broadcasted_iota(I32, (rows, kc_len), 1)
                s = jnp.where(k_pos <= row_t, s, NEG_INF)
            m_new = jnp.maximum(m_run, jnp.max(s, axis=1, keepdims=True))
            p = jnp.exp((s - m_new).astype(MXU_DTYPE))
            acc_new = jnp.exp(m_run - m_new) * acc + _mm(p, vsel[g, pl.ds(k0, kc_len), :])
            return m_new, acc_new

        n_full = lax.div(q0, kc_len)
        init = (jnp.full((rows, 1), NEG_INF, F32), jnp.zeros((rows, LANES), F32))
        carry = lax.fori_loop(0, n_full, functools.partial(sel_chunk, causal=False), init)
        _, acc_sel = sel_chunk(n_full, carry, causal=True)
        o_sel = acc_sel[:, :hd] / acc_sel[:, hd:hd + 1]
        s = _mm_nt(q_aug, kwaug[g, pl.ds(w0, wk), :]) + band
        p = jnp.exp((s - jnp.max(s, axis=1, keepdims=True)).astype(MXU_DTYPE))
        acc_win = _mm(p, vwin[g, pl.ds(w0, wk), :])
        o_win = acc_win[:, :hd] / acc_win[:, hd:hd + 1]
        for r in range(Q_PER_KV):
            c = (g * Q_PER_KV + r) * 3
            out_heads.append(gates[:, c:c + 1] * o_cmp[r]
                             + gates[:, c + 1:c + 2] * o_sel[r * tq:(r + 1) * tq]
                             + gates[:, c + 2:c + 3] * o_win[r * tq:(r + 1) * tq])
    o_ref[0] = jnp.concatenate(out_heads, axis=1).astype(o_ref.dtype)


def _attn_prompt_call(q, kvw, kcvc, gates, tq, kc_len):
    b, t, n_q = q.shape
    wid = 4 * HEAD_DIM
    kern = functools.partial(_attn_prompt_kernel, tq=tq, kc_len=kc_len, t_len=t)
    assert SEL_FEAT0 + t // SEL_BLOCK <= LANES and kc_len % tq == 0
    k_scratch = pltpu.VMEM((KV_HEADS, t, LANES), MXU_DTYPE)
    v_scratch = pltpu.VMEM((KV_HEADS, t, LANES), MXU_DTYPE)
    n_cmp_rows = kcvc.shape[1]
    return pl.pallas_call(
        kern,
        grid=(b, t // tq),
        in_specs=[pl.BlockSpec((1, tq, n_q), lambda i, j: (i, j, 0)),
                  pl.BlockSpec((1, t, wid), lambda i, j: (i, 0, 1)),
                  pl.BlockSpec((1, t, wid), lambda i, j: (i, 0, 2)),
                  pl.BlockSpec((1, n_cmp_rows, wid), lambda i, j: (i, 0, 0)),
                  pl.BlockSpec((1, tq, LANES), lambda i, j: (i, j, 0))],
        out_specs=pl.BlockSpec((1, tq, n_q), lambda i, j: (i, j, 0)),
        out_shape=jax.ShapeDtypeStruct((b, t, n_q), MXU_DTYPE),
        scratch_shapes=[k_scratch, v_scratch, k_scratch, v_scratch,
                        pltpu.VMEM((n_cmp_rows + 2 * SUBLANES, tq), F32)],
        compiler_params=_cparams(2),
        name="attn_prompt",
    )(q, kvw, kvw, kcvc, gates)


CONV_PAD = 32


def _ln_silu(z, lg, lb):
    mu = jnp.mean(z, axis=-1, keepdims=True)
    zc = z - mu
    var = jnp.mean(zc * zc, axis=-1, keepdims=True)
    return _silu(zc * lax.rsqrt(var + EPS) * lg + lb)


def _conv_prompt_kernel(glu_ref, cw_ref, cb_ref, lg_ref, lb_ref, z_ref, full_ref, *, t_len, tt):
    full_ref[pl.ds(0, CONV_PAD), :] = jnp.zeros((CONV_PAD, full_ref.shape[1]), F32)
    full_ref[pl.ds(CONV_PAD, t_len), :] = glu_ref[0]
    first = CONV_PAD - (CONV_W - 1)

    def tile(i, carry):
        r0 = pl.multiple_of(i * tt, tt)
        acc = jnp.zeros((tt, full_ref.shape[1]), F32) + cb_ref[...]
        win = full_ref[pl.ds(r0, tt + CONV_PAD), :]
        for phase in range(SUBLANES):
            offs = [o for o in range(first, first + CONV_W) if o % SUBLANES == phase]
            if not offs:
                continue
            shifted = win[phase:max(offs) + tt]
            for o in offs:
                acc = acc + shifted[o - phase:o - phase + tt] * cw_ref[o - first:o - first + 1, :]
        z_ref[0, pl.ds(r0, tt), :] = _ln_silu(acc, lg_ref[...], lb_ref[...]).astype(z_ref.dtype)
        return carry

    lax.fori_loop(0, t_len // tt, tile, 0)


def _conv_prompt_call(glu, conv_w, conv_b, ln_g, ln_b, tt):
    b, t, c = glu.shape
    kern = functools.partial(_conv_prompt_kernel, t_len=t, tt=tt)
    vec = pl.BlockSpec((1, c), lambda i: (0, 0))
    return pl.pallas_call(
        kern,
        grid=(b,),
        in_specs=[pl.BlockSpec((1, t, c), lambda i: (i, 0, 0)),
                  pl.BlockSpec((CONV_PAD, c), lambda i: (0, 0)), vec, vec, vec],
        out_specs=pl.BlockSpec((1, t, c), lambda i: (i, 0, 0)),
        out_shape=jax.ShapeDtypeStruct((b, t, c), MXU_DTYPE),
        scratch_shapes=[pltpu.VMEM((CONV_PAD + t, c), F32)],
        compiler_params=_cparams(1),
        name="conv_prompt",
    )(glu, jnp.pad(conv_w, ((0, CONV_PAD - CONV_W), (0, 0))), conv_b.reshape(1, c),
      ln_g.reshape(1, c), ln_b.reshape(1, c))


def _conv_sample_kernel(hist_ref, new_ref, cw_ref, cb_ref, lg_ref, lb_ref, z_ref):
    z = new_ref[...] * cw_ref[CONV_W - 1:CONV_W, :] + cb_ref[...]
    for w in range(CONV_W - 1):
        z = z + hist_ref[w] * cw_ref[w:w + 1, :]
    z_ref[...] = _ln_silu(z, lg_ref[...], lb_ref[...]).astype(z_ref.dtype)


def _conv_sample_call(hist, glu_new, conv_w, conv_b, ln_g, ln_b):
    bs, c = glu_new.shape
    return pl.pallas_call(
        _conv_sample_kernel,
        out_shape=jax.ShapeDtypeStruct((bs, c), MXU_DTYPE),
        name="conv_sample",
    )(hist, glu_new, jnp.pad(conv_w, ((0, CONV_PAD - CONV_W), (0, 0))), conv_b.reshape(1, c),
      ln_g.reshape(1, c), ln_b.reshape(1, c))


def _route_tile(h2, rwt_ref, rb_ref, cnt_ref):
    n_e = N_EXPERTS
    per_grp = n_e // N_GROUPS
    tm = h2.shape[0]
    aff = _sigmoid(_mm_nt(rwt_ref[...], h2))
    biased = aff + rb_ref[...]
    neg = -jnp.inf
    g_rows = []
    for g in range(N_GROUPS):
        v = biased[g * per_grp:(g + 1) * per_grp]
        m1 = jnp.max(v, axis=0, keepdims=True)
        is_m1 = v == m1
        n_m1 = jnp.sum(jnp.where(is_m1, 1.0, 0.0), axis=0, keepdims=True)
        m2 = jnp.max(jnp.where(is_m1, neg, v), axis=0, keepdims=True)
        g_rows.append(m1 + jnp.where(n_m1 >= 2.0, m1, m2))
    g_keep = _top_n_rows(jnp.concatenate(g_rows, axis=0), TOPK_GROUPS)
    cur = jnp.concatenate(
        [jnp.where(g_keep[g:g + 1], biased[g * per_grp:(g + 1) * per_grp], neg)
         for g in range(N_GROUPS)], axis=0)
    row_id = lax.broadcasted_iota(I32, (n_e, tm), 0).astype(F32)
    ids, wts, hots = [], [], []
    for _ in range(TOP_K):
        m = jnp.max(cur, axis=0, keepdims=True)
        idx = jnp.min(jnp.where(cur == m, row_id, float(n_e)), axis=0, keepdims=True)
        hot = row_id == idx
        ids.append(idx)
        wts.append(jnp.sum(jnp.where(hot, aff, 0.0), axis=0, keepdims=True))
        hots.append(hot)
        cur = jnp.where(hot, neg, cur)
    w_sum = wts[0]
    for w in wts[1:]:
        w_sum = w_sum + w
    wts = [w / w_sum * ROUTED_SCALE for w in wts]
    hot_all = jnp.where(hots[0], 1.0, 0.0)
    for hot in hots[1:]:
        hot_all = hot_all + jnp.where(hot, 1.0, 0.0)
    earlier = jnp.where(lax.broadcasted_iota(I32, (tm, tm), 0) < lax.broadcasted_iota(I32, (tm, tm), 1),
                        1.0, 0.0)
    before = cnt_ref[:, 0:1] + jnp.dot(hot_all.astype(BF16), earlier.astype(BF16),
                                       preferred_element_type=F32)
    ranks = [jnp.sum(jnp.where(hot, before, 0.0), axis=0, keepdims=True) for hot in hots]
    cnt_ref[...] = cnt_ref[...] + jnp.sum(hot_all, axis=1, keepdims=True)
    cat = lambda rows: jnp.concatenate(rows, axis=0)
    return cat(ids).astype(I32), cat(wts), cat(ranks).astype(I32)


def _merge_kernel(x_ref, oa_ref, z_ref, sc1_ref, sh1_ref, gt1_ref, sc2_ref, sh2_ref,
                  g1_ref, wm_ref, wa_ref, wb_ref, wo_ref, g2_ref, rwt_ref, rb_ref,
                  s13_ref, s2_ref,
                  x1_ref, h2_ref, ysh_ref, eid_ref, ew_ref, rk_ref, cnt_out_ref, cnt_ref, *, d_exp):
    first_step = (pl.program_id(0) == 0) & (pl.program_id(1) == 0)

    @pl.when(first_step)
    def _():
        cnt_ref[...] = jnp.zeros(cnt_ref.shape, F32)

    x = x_ref[0]
    d = x.shape[1]
    tm = x.shape[0]
    h = _rms(x, g1_ref[...]) * (1.0 + sc1_ref[0]) + sh1_ref[0]
    g_mrg = _sigmoid(_mm(h, wm_ref[...]))
    mixed = g_mrg[:, :d] * _mm(oa_ref[0], wa_ref[...]) + g_mrg[:, d:] * _mm(z_ref[0], wb_ref[...])
    x1 = x + gt1_ref[0] * _mm(mixed, wo_ref[...])
    x1_ref[0] = x1
    h2 = _rms(x1, g2_ref[...]) * (1.0 + sc2_ref[0]) + sh2_ref[0]
    for s in range(d // LANES):
        h2_ref[pl.ds(s, tm, stride=d // LANES), :] = h2[:, s * LANES:(s + 1) * LANES]
    hs = _mm(h2, s13_ref[...])
    ysh_ref[0] = _mm(_silu(hs[:, :d_exp]) * hs[:, d_exp:], s2_ref[...])
    ids, wts, ranks = _route_tile(h2, rwt_ref, rb_ref, cnt_ref)
    eid_ref[...] = ids
    ew_ref[...] = wts
    rk_ref[...] = ranks
    cnt_out_ref[...] = cnt_ref[...]


def _merge_call(x, o_attn, z, mods, mw, tm):
    bx, tx, d = x.shape
    n = bx * tx
    nt = tx // tm
    d_exp = mw["s2"].shape[0]
    kern = functools.partial(_merge_kernel, d_exp=d_exp)
    row = lambda w: pl.BlockSpec((1, tm, w), lambda b, i: (b, i, 0))
    full = lambda a: pl.BlockSpec(a.shape, lambda b, i: (0,) * a.ndim)
    tok = pl.BlockSpec((TOP_K, tm), lambda b, i: (0, b * nt + i))
    wnames = ("g1", "wm", "wa", "wb", "wo", "g2", "rwt", "rb", "s13", "s2")
    return pl.pallas_call(
        kern,
        grid=(bx, nt),
        in_specs=[row(d), row(o_attn.shape[2]), row(z.shape[2])]
                 + [_mod_spec(m, tm, d) for m in mods]
                 + [full(mw[k]) for k in wnames],
        out_specs=[row(d),
                   pl.BlockSpec((tm * (d // LANES), LANES), lambda b, i: (b * nt + i, 0)),
                   row(d), tok, tok, tok,
                   pl.BlockSpec((N_EXPERTS, LANES), lambda b, i: (0, 0))],
        out_shape=[jax.ShapeDtypeStruct((bx, tx, d), F32),
                   jax.ShapeDtypeStruct((n * (d // LANES), LANES), F32),
                   jax.ShapeDtypeStruct((bx, tx, d), F32),
                   jax.ShapeDtypeStruct((TOP_K, n), I32),
                   jax.ShapeDtypeStruct((TOP_K, n), F32),
                   jax.ShapeDtypeStruct((TOP_K, n), I32),
                   jax.ShapeDtypeStruct((N_EXPERTS, LANES), F32)],
        scratch_shapes=[pltpu.VMEM((N_EXPERTS, LANES), F32)],
        compiler_params=_cparams(2),
        name="merge_route",
    )(x, o_attn, z, *mods, *[mw[k] for k in wnames])


def _start_tile_copy(src_hbm, src_row, buf, slot, dst_row, sem, n_sub):
    pltpu.make_async_copy(src_hbm.at[pl.ds(pl.multiple_of(src_row, n_sub), n_sub), :],
                          buf.at[slot, pl.ds(pl.multiple_of(dst_row, n_sub), n_sub), :],
                          sem.at[slot]).start()


def _wait_slot(buf, sem, slot):
    pltpu.make_async_copy(buf.at[slot], buf.at[slot], sem.at[slot]).wait()


def _moe_plan(counts, n_asg, rb):
    n_blk = -(-(n_asg + N_EXPERTS * (rb - 1)) // rb)
    padded = (counts + rb - 1) // rb * rb
    pad_end = jnp.cumsum(padded)
    start = pad_end - padded
    blk_row = jnp.arange(n_blk, dtype=I32)[:, None] * rb
    blk_e = jnp.minimum(jnp.sum((pad_end[None, :] <= blk_row).astype(I32), axis=1), N_EXPERTS - 1)
    n_used = (pad_end[-1] // rb).astype(I32).reshape(1)
    return start.astype(I32), padded.astype(I32), blk_e.astype(I32), n_used, n_blk


def _dispatch_kernel(start_ref, cnt_ref, pad_ref, n_used_ref,
                     eid_ref, rk_ref, eid_s_ref, rk_s_ref, h2p_ref, h2s_ref,
                     xs_hbm, dst_ref, dst_s_ref, zeros, sem,
                     *, tmd, n_p_steps, bs, n_sub, rb, n_blk):
    i = pl.program_id(0)

    def scatter_tokens(src_ref, e_ref, r_ref, d_ref, n_tok):
        def issue(t, carry):
            src = src_ref.at[pl.ds(pl.multiple_of(t * n_sub, n_sub), n_sub), :]
            for k in range(TOP_K):
                dst_row = (start_ref[e_ref[k, t]] + r_ref[k, t]) * n_sub
                d_ref[k, t] = dst_row
                pltpu.make_async_copy(
                    src, xs_hbm.at[pl.ds(pl.multiple_of(dst_row, n_sub), n_sub), :], sem).start()
            return carry
        lax.fori_loop(0, n_tok, issue, 0)
        done = xs_hbm.at[pl.ds(0, n_tok * TOP_K * n_sub), :]
        pltpu.make_async_copy(done, done, sem).wait()

    @pl.when(i < n_p_steps)
    def _():
        scatter_tokens(h2p_ref, eid_ref, rk_ref, dst_ref, tmd)

    @pl.when(i == n_p_steps)
    def _():
        scatter_tokens(h2s_ref, eid_s_ref, rk_s_ref, dst_s_ref, bs)
        zeros[...] = jnp.zeros(zeros.shape, F32)

        def zero_rows(first_row, n_rows):
            pltpu.make_async_copy(
                zeros.at[pl.ds(0, n_rows * n_sub), :],
                xs_hbm.at[pl.ds(pl.multiple_of(first_row * n_sub, n_sub), n_rows * n_sub), :],
                sem).start()

        def pad_expert(e, carry):
            n_pad = pad_ref[e] - cnt_ref[e]
            row = start_ref[e] + cnt_ref[e]
            piece = rb // 2
            while piece >= 1:
                take = (n_pad & piece) != 0

                @pl.when(take)
                def _(row=row, piece=piece):
                    zero_rows(row, piece)

                row = row + jnp.where(take, piece, 0)
                piece //= 2
            return carry

        lax.fori_loop(0, N_EXPERTS, pad_expert, 0)

        def pad_block(blk, carry):
            zero_rows(blk * rb, rb)
            return carry

        lax.fori_loop(n_used_ref[0], n_blk, pad_block, 0)
        n_zero = n_blk * rb - (n_p_steps * tmd + bs) * TOP_K
        done = xs_hbm.at[pl.ds(0, n_zero * n_sub), :]
        pltpu.make_async_copy(done, done, sem).wait()


def _dispatch_call(start, counts, padded, n_used, eid_p, rank_p, eid_s, rank_s, h2_p, h2_s,
                   n_blk, rb, tmd):
    n_p = eid_p.shape[1]
    bs = eid_s.shape[1]
    n_sub = h2_p.shape[0] // n_p
    n_p_steps = n_p // tmd
    kern = functools.partial(_dispatch_kernel, tmd=tmd, n_p_steps=n_p_steps, bs=bs, n_sub=n_sub,
                             rb=rb, n_blk=n_blk)
    last = n_p_steps - 1
    tile = pl.BlockSpec((TOP_K, tmd), lambda i, *_: (0, jnp.minimum(i, last)), memory_space=pltpu.SMEM)
    whole = pl.BlockSpec((TOP_K, bs), lambda i, *_: (0, 0), memory_space=pltpu.SMEM)
    grid_spec = pltpu.PrefetchScalarGridSpec(
        num_scalar_prefetch=4,
        grid=(n_p_steps + 1,),
        in_specs=[tile, tile, whole, whole,
                  pl.BlockSpec((tmd * n_sub, LANES), lambda i, *_: (jnp.minimum(i, last), 0)),
                  pl.BlockSpec((bs * n_sub, LANES), lambda i, *_: (0, 0))],
        out_specs=[pl.BlockSpec(memory_space=pl.ANY), tile, whole],
        scratch_shapes=[pltpu.VMEM((rb * n_sub, LANES), F32), pltpu.SemaphoreType.DMA],
    )
    return pl.pallas_call(
        kern,
        grid_spec=grid_spec,
        out_shape=[jax.ShapeDtypeStruct((n_blk * rb * n_sub, LANES), F32),
                   jax.ShapeDtypeStruct((TOP_K, n_p), I32),
                   jax.ShapeDtypeStruct((TOP_K, bs), I32)],
        compiler_params=_cparams(1),
        name="moe_dispatch",
    )(start, counts, padded, n_used, eid_p, rank_p, eid_s, rank_s, h2_p, h2_s)


def _experts_kernel(blk_e_ref, n_used_ref, x_ref, w1_ref, w3_ref, w2_ref, y_ref, *, rb, n_sub):
    i = pl.program_id(0)

    @pl.when(i < n_used_ref[0])
    def _():
        x = jnp.concatenate([x_ref[pl.ds(s, rb, stride=n_sub), :] for s in range(n_sub)], axis=1)
        hid = _silu(_mm(x, w1_ref[0])) * _mm(x, w3_ref[0])
        y = _mm(hid, w2_ref[0])
        for s in range(n_sub):
            y_ref[pl.ds(s, rb, stride=n_sub), :] = y[:, s * LANES:(s + 1) * LANES]

    @pl.when(i >= n_used_ref[0])
    def _():
        y_ref[...] = jnp.zeros(y_ref.shape, F32)


def _experts_call(x_rows, blk_e, n_used, w1, w3, w2, rb):
    _, d, d_exp = w1.shape
    n_sub = d // LANES
    n_blk = x_rows.shape[0] // (rb * n_sub)
    kern = functools.partial(_experts_kernel, rb=rb, n_sub=n_sub)
    grid_spec = pltpu.PrefetchScalarGridSpec(
        num_scalar_prefetch=2,
        grid=(n_blk,),
        in_specs=[pl.BlockSpec((rb * n_sub, LANES), lambda i, be, nu: (jnp.minimum(i, nu[0] - 1), 0)),
                  pl.BlockSpec((1, d, d_exp), lambda i, be, nu: (be[i], 0, 0)),
                  pl.BlockSpec((1, d, d_exp), lambda i, be, nu: (be[i], 0, 0)),
                  pl.BlockSpec((1, d_exp, d), lambda i, be, nu: (be[i], 0, 0))],
        out_specs=pl.BlockSpec((rb * n_sub, LANES), lambda i, be, nu: (i, 0)),
    )
    return pl.pallas_call(
        kern,
        grid_spec=grid_spec,
        out_shape=jax.ShapeDtypeStruct((n_blk * rb * n_sub, LANES), F32),
        compiler_params=_cparams(1),
        name="routed_experts",
    )(blk_e, n_used, x_rows, w1, w3, w2)


def _combine_kernel(rows_ref, rows_next_ref, y_hbm, ew_ref, x1_ref,
                    ysh_ref, gt2_ref, gf_ref, o_ref, buf, sem, *, tm, n_sub, nt, n_steps):
    step = pl.program_id(0) * nt + pl.program_id(1)
    slot = lax.rem(step, 2)

    def gather(r_ref, to_slot):
        def issue(t, carry):
            for k in range(TOP_K):
                _start_tile_copy(y_hbm, r_ref[k, t], buf, to_slot, (k * tm + t) * n_sub, sem, n_sub)
            return carry
        lax.fori_loop(0, tm, issue, 0)

    @pl.when(step == 0)
    def _():
        gather(rows_ref, 0)

    @pl.when(step + 1 < n_steps)
    def _():
        gather(rows_next_ref, 1 - slot)

    _wait_slot(buf, sem, slot)
    ew = ew_ref[...]
    cols = []
    for s in range(n_sub):
        acc = None
        for k in range(TOP_K):
            term = ew[:, k:k + 1] * buf[slot, pl.ds(k * tm * n_sub + s, tm, stride=n_sub), :]
            acc = term if acc is None else acc + term
        cols.append(acc)
    y_routed = jnp.concatenate(cols, axis=1)
    x2 = x1_ref[0] + gt2_ref[0] * (y_routed + ysh_ref[0])
    o_ref[0] = _rms(x2, gf_ref[...])


def _combine_call(dst_rows, y_rows, ew, x1, ysh, gt2, g_final, tm):
    bx, tx, d = x1.shape
    nt = tx // tm
    n_sub = d // LANES
    n_steps = bx * nt
    kern = functools.partial(_combine_kernel, tm=tm, n_sub=n_sub, nt=nt, n_steps=n_steps)
    row = pl.BlockSpec((1, tm, d), lambda b, i: (b, i, 0))
    cur = pl.BlockSpec((TOP_K, tm), lambda b, i: (0, b * nt + i), memory_space=pltpu.SMEM)
    nxt = pl.BlockSpec((TOP_K, tm), lambda b, i: (0, jnp.minimum(b * nt + i + 1, n_steps - 1)),
                       memory_space=pltpu.SMEM)
    return pl.pallas_call(
        kern,
        grid=(bx, nt),
        in_specs=[cur, nxt,
                  pl.BlockSpec(memory_space=pl.ANY),
                  pl.BlockSpec((tm, TOP_K), lambda b, i: (b * nt + i, 0)),
                  row, row, _mod_spec(gt2, tm, d),
                  pl.BlockSpec((1, d), lambda b, i: (0, 0))],
        out_specs=row,
        out_shape=jax.ShapeDtypeStruct((bx, tx, d), F32),
        scratch_shapes=[pltpu.VMEM((2, TOP_K * tm * n_sub, LANES), F32),
                        pltpu.SemaphoreType.DMA((2,))],
        compiler_params=_cparams(2),
        name="combine_final",
    )(dst_rows, dst_rows, y_rows, ew, x1, ysh, gt2, g_final.reshape(1, d))


def _compress_sample_kernel(pt_ref, cache_hbm, w1k_ref, w1v_ref, fs_ref, raw, kbuf, vbuf, sem,
                            *, pp, page_rows, n_steps):
    step = pl.program_id(0) * pl.num_programs(1) + pl.program_id(1)
    slot = lax.rem(step, 2)

    def fetch(of_step, to_slot):
        def issue(p, carry):
            page = pt_ref[of_step * pp + p]
            pltpu.make_async_copy(cache_hbm.at[page, pl.ds(0, 2)], raw.at[to_slot, p],
                                  sem.at[to_slot]).start()
            return carry
        lax.fori_loop(0, pp, issue, 0)

    @pl.when(step == 0)
    def _():
        fetch(0, 0)

    @pl.when(step + 1 < n_steps)
    def _():
        fetch(step + 1, 1 - slot)

    _wait_slot(raw, sem, slot)
    per_page = page_rows // CMP_STRIDE
    n_chunks = pp * per_page
    out_row = lax.broadcasted_iota(I32, (page_rows, page_rows), 0)
    in_row = lax.broadcasted_iota(I32, (page_rows, page_rows), 1)
    regroup = jnp.where(in_row == lax.rem(out_row, per_page) * CMP_STRIDE + lax.div(out_row, per_page),
                        1.0, 0.0).astype(MXU_DTYPE)

    def to_rows(p, carry):
        for which, buf in ((0, kbuf), (1, vbuf)):
            rows = _mm_nt(regroup, raw[slot, p, which].reshape(LANES, page_rows))
            for l in range(CMP_STRIDE):
                dst = pl.multiple_of(l * n_chunks + p * per_page, per_page)
                buf[pl.ds(dst, per_page), :] = rows[l * per_page:(l + 1) * per_page]
        return carry

    lax.fori_loop(0, pp, to_rows, 0, unroll=4)
    rows_of = lambda buf: (lambda l: buf[pl.ds(l * n_chunks, n_chunks), :])
    fs_ref[0] = _cmp_first_layer(rows_of(kbuf), rows_of(vbuf), w1k_ref, w1v_ref)


def _compress_sample_call(page_table, cache5, w1k, w1v, pp):
    bs, n_pages = page_table.shape
    _, _, n_g, hd, page_rows = cache5.shape
    assert n_g * hd == LANES
    n_chunks = pp * page_rows // CMP_STRIDE
    spb = n_pages // pp
    kern = functools.partial(_compress_sample_kernel, pp=pp, page_rows=page_rows, n_steps=bs * spb)
    buf = pltpu.VMEM((pp * page_rows, LANES), F32)
    grid_spec = pltpu.PrefetchScalarGridSpec(
        num_scalar_prefetch=1,
        grid=(bs, spb),
        in_specs=[pl.BlockSpec(memory_space=pl.ANY),
                  pl.BlockSpec(w1k.shape, lambda b, i, pt: (0, 0)),
                  pl.BlockSpec(w1v.shape, lambda b, i, pt: (0, 0))],
        out_specs=pl.BlockSpec((1, n_chunks, 8 * CMP_HID), lambda b, i, pt: (b, i, 0)),
        scratch_shapes=[pltpu.VMEM((2, pp, 2, n_g, hd, page_rows), F32), buf, buf,
                        pltpu.SemaphoreType.DMA((2,))],
    )
    return pl.pallas_call(
        kern,
        grid_spec=grid_spec,
        out_shape=jax.ShapeDtypeStruct((bs, spb * n_chunks, 8 * CMP_HID), F32),
        compiler_params=_cparams(2),
        name="compress_sample",
    )(page_table.reshape(-1), cache5, w1k, w1v)


def _head_slopes(n_rows):
    head = lax.broadcasted_iota(I32, (n_rows, 1), 0)
    slopes = jnp.zeros((n_rows, 1), F32)
    for h in range(N_HEADS):
        slopes = jnp.where(head == h, _slope(h), slopes)
    return slopes


def _attn_sample_a_kernel(q_ref, fs_ref, pe_ref, w1pe_ref, w2bd_ref, ocmp_ref, idx_ref,
                          *, t_pos, n_blk, n_blk_pad):
    hd = HEAD_DIM
    f = fs_ref[0]
    n_c = f.shape[0]
    half = 4 * CMP_HID
    second_next = pltpu.roll(f[:, half:], n_c - 1, 0)
    kcvc = _cmp_second_layer(f[:, :half], second_next, pe_ref, w1pe_ref, w2bd_ref)
    q = q_ref[0]
    c_idx = lax.broadcasted_iota(I32, (1, n_c), 1)
    dist_c = t_pos - (c_idx * CMP_STRIDE + (CMP_BLOCK - 1))
    mask_c = dist_c >= 0
    slopes = _head_slopes(N_HEADS)
    head_grp = lax.div(lax.broadcasted_iota(I32, (N_HEADS, 1), 0), Q_PER_KV)
    per = SEL_BLOCK // CMP_STRIDE
    c_row = lax.broadcasted_iota(I32, (n_c, n_blk_pad), 0)
    lo = lax.broadcasted_iota(I32, (n_c, n_blk_pad), 1) * per
    spread = jnp.where((c_row >= lo) & (c_row < lo + per - 1), 1.0, 0.0) \
        + jnp.where((c_row == lo - 1) | (c_row == lo + per - 1), 0.5, 0.0)
    blk_lane = lax.broadcasted_iota(I32, (1, n_blk_pad), 1)
    cur = t_pos // SEL_BLOCK
    forced = (blk_lane == 0) | (blk_lane == cur) | (blk_lane == cur - 1)
    in_range = (blk_lane * SEL_BLOCK <= t_pos) & (blk_lane < n_blk)
    n_sq = (n_blk_pad, n_blk_pad)
    sub_id = lax.broadcasted_iota(I32, n_sq, 0)
    lane_id = lax.broadcasted_iota(I32, n_sq, 1)
    top_n = min(TOP_N, n_blk)
    o_cmp = jnp.zeros((N_HEADS, hd), F32)
    idx_rows = []
    for g in range(KV_HEADS):
        k_cmp = kcvc[:, g * hd:(g + 1) * hd]
        v_cmp = kcvc[:, (2 + g) * hd:(3 + g) * hd]
        s = _mm_nt(q, k_cmp) * ATTN_SCALE - slopes * dist_c.astype(F32)
        m = jnp.max(jnp.where(mask_c, s, NEG_INF), axis=1, keepdims=True)
        p = jnp.where(mask_c, jnp.exp(s - m), 0.0)
        l = jnp.sum(p, axis=1, keepdims=True)
        p = p / jnp.where(l > 0.0, l, 1.0)
        o_cmp = jnp.where(head_grp == g, _mm(p, v_cmp), o_cmp)
        p_sum = jnp.sum(jnp.where(head_grp == g, p, 0.0), axis=0, keepdims=True)
        imp = jnp.dot(jnp.broadcast_to(p_sum, (SUBLANES, n_c)), spread,
                      precision=lax.Precision.HIGHEST, preferred_element_type=F32)[0:1]
        imp = jnp.where(forced, FORCE_SCORE, imp)
        imp = jnp.where(in_range, imp, -jnp.inf)
        imp_col = jnp.transpose(jnp.broadcast_to(imp, (LANES, n_blk_pad)))[:, 0:1]
        beats = (imp > imp_col) | ((imp == imp_col) & (lane_id < sub_id))
        rank_col = jnp.sum(jnp.where(beats, 1.0, 0.0), axis=1, keepdims=True)
        sel_col = jnp.where(rank_col < float(top_n), 1.0, 0.0)
        before = jnp.dot(jnp.where(lane_id < sub_id, 1.0, 0.0).astype(BF16),
                         jnp.broadcast_to(sel_col, (n_blk_pad, LANES)).astype(BF16),
                         preferred_element_type=F32)
        slot_lane = lax.broadcasted_iota(I32, (n_blk_pad, LANES), 1).astype(F32)
        blk_sub = lax.broadcasted_iota(I32, (n_blk_pad, LANES), 0).astype(F32)
        hit = (sel_col > 0.5) & (before == slot_lane)
        idx_rows.append(jnp.sum(jnp.where(hit, blk_sub, 0.0), axis=0, keepdims=True))
    ocmp_ref[0] = o_cmp
    pad = jnp.zeros((SUBLANES - KV_HEADS, LANES), F32)
    idx_ref[0] = jnp.concatenate(idx_rows + [pad], axis=0).astype(I32)


def _attn_sample_a_call(q_heads, fs, cw, t_pos, n_blk):
    bs = q_heads.shape[0]
    n_blk_pad = -(-n_blk // LANES) * LANES
    kern = functools.partial(_attn_sample_a_kernel, t_pos=t_pos, n_blk=n_blk, n_blk_pad=n_blk_pad)
    full = lambda a: pl.BlockSpec(a.shape, lambda b: (0,) * a.ndim)
    per_b = lambda a: pl.BlockSpec((1,) + a.shape[1:], lambda b: (b,) + (0,) * (a.ndim - 1))
    return pl.pallas_call(
        kern,
        grid=(bs,),
        in_specs=[per_b(q_heads), per_b(fs), full(cw["pe"]), full(cw["w1pe"]), full(cw["w2bd"])],
        out_specs=[pl.BlockSpec((1, N_HEADS, HEAD_DIM), lambda b: (b, 0, 0)),
                   pl.BlockSpec((1, SUBLANES, LANES), lambda b: (b, 0, 0))],
        out_shape=[jax.ShapeDtypeStruct((bs, N_HEADS, HEAD_DIM), F32),
                   jax.ShapeDtypeStruct((bs, SUBLANES, LANES), I32)],
        compiler_params=_cparams(1),
        name="attn_sample_select",
    )(q_heads, fs, cw["pe"], cw["w1pe"], cw["w2bd"])


def _attn_sample_b_kernel(sel_ref, pt_ref, q_ref, cache_hbm, kvn_ref, win_ref, gate_ref, ocmp_ref,
                          o_ref, kbuf, vbuf, sem,
                          *, t_pos, top_n, n_past_blk, n_pages, per_page, n_steps):
    b = pl.program_id(0)
    slot = lax.rem(b, 2)
    hd = HEAD_DIM
    page_rows = kbuf.shape[-1]

    def fetch(of_b, to_slot):
        for g in range(KV_HEADS):
            def issue(i, carry, g=g):
                blk = jnp.minimum(sel_ref[(of_b * KV_HEADS + g) * top_n + i], n_past_blk - 1)
                page = pt_ref[of_b * n_pages + blk // per_page]
                pltpu.make_async_copy(cache_hbm.at[page, 2, g], kbuf.at[to_slot, g, i],
                                      sem.at[to_slot]).start()
                pltpu.make_async_copy(cache_hbm.at[page, 3, g], vbuf.at[to_slot, g, i],
                                      sem.at[to_slot]).start()
                return carry
            lax.fori_loop(0, top_n, issue, 0)

    @pl.when(b == 0)
    def _():
        fetch(0, 0)

    @pl.when(b + 1 < n_steps)
    def _():
        fetch(b + 1, 1 - slot)

    _wait_slot(kbuf, sem, slot)
    _wait_slot(vbuf, sem, slot)
    q = q_ref[0]
    qf = q.astype(F32)
    slopes = _head_slopes(N_HEADS)
    kvn = kvn_ref[0]
    rnd = lambda a: a.astype(MXU_DTYPE).astype(F32)
    new_col = lambda c: rnd(kvn[:, c * hd:(c + 1) * hd])
    gates = gate_ref[0]
    head_grp = lax.div(lax.broadcasted_iota(I32, (N_HEADS, 1), 0), Q_PER_KV)
    lane = lax.broadcasted_iota(I32, (1, page_rows), 1)
    out = jnp.zeros((N_HEADS, hd), F32)
    for g in range(KV_HEADS):
        scores = []
        for i in range(top_n):
            blk = sel_ref[(b * KV_HEADS + g) * top_n + i]
            k_pos = (blk // per_page) * page_rows + lane
            picked = (lax.div(lane, SEL_BLOCK) == lax.rem(blk, per_page)) & (blk < n_past_blk)
            s = _mm(q, kbuf[slot, g, i]) * ATTN_SCALE - slopes * (t_pos - k_pos).astype(F32)
            scores.append(jnp.where(picked, s, NEG_INF))
        s_all = jnp.concatenate(scores, axis=1)
        s_new = jnp.sum(qf * new_col(4 + g), axis=1, keepdims=True) * ATTN_SCALE
        m = jnp.maximum(jnp.max(s_all, axis=1, keepdims=True), s_new)
        p_all = jnp.exp(s_all - m)
        p_new = jnp.exp(s_new - m)
        acc = p_new * new_col(6 + g)
        for i in range(top_n):
            acc = acc + _mm_nt(p_all[:, i * page_rows:(i + 1) * page_rows], vbuf[slot, g, i])
        o_sel = acc / (jnp.sum(p_all, axis=1, keepdims=True) + p_new)
        w_rows = win_ref.shape[-1]
        dist_w = (w_rows - lax.broadcasted_iota(I32, (1, w_rows), 1)).astype(F32)
        s_w = _mm(q, win_ref[0, 0, g]) * ATTN_SCALE - slopes * dist_w
        s_w = jnp.where(dist_w < float(WINDOW), s_w, NEG_INF)
        sw_new = jnp.sum(qf * new_col(8 + g), axis=1, keepdims=True) * ATTN_SCALE
        m_w = jnp.maximum(jnp.max(s_w, axis=1, keepdims=True), sw_new)
        p_w = jnp.exp(s_w - m_w)
        pw_new = jnp.exp(sw_new - m_w)
        o_win = (_mm_nt(p_w, win_ref[0, 1, g]) + pw_new * new_col(10 + g)) \
            / (jnp.sum(p_w, axis=1, keepdims=True) + pw_new)
        o = gates[:, 0:1] * ocmp_ref[0] + gates[:, 1:2] * o_sel + gates[:, 2:3] * o_win
        out = jnp.where(head_grp == g, o, out)
    o_ref[0] = out.astype(o_ref.dtype)


def _attn_sample_b_call(sel_idx, page_table, q_heads, cache5, kv_new, win5, gates3, o_cmp,
                        t_pos, top_n):
    bs, n_pages = page_table.shape
    _, _, n_g, hd, page_rows = cache5.shape
    per_page = page_rows // SEL_BLOCK
    n_past_blk = n_pages * per_page
    kern = functools.partial(_attn_sample_b_kernel, t_pos=t_pos, top_n=top_n, n_past_blk=n_past_blk,
                             n_pages=n_pages, per_page=per_page, n_steps=bs)
    per_b = lambda a: pl.BlockSpec((1,) + a.shape[1:], lambda b, sel, pt: (b,) + (0,) * (a.ndim - 1))
    tiles = pltpu.VMEM((2, n_g, top_n, hd, page_rows), F32)
    grid_spec = pltpu.PrefetchScalarGridSpec(
        num_scalar_prefetch=2,
        grid=(bs,),
        in_specs=[per_b(q_heads), pl.BlockSpec(memory_space=pl.ANY),
                  per_b(kv_new), per_b(win5), per_b(gates3), per_b(o_cmp)],
        out_specs=pl.BlockSpec((1, N_HEADS, HEAD_DIM), lambda b, sel, pt: (b, 0, 0)),
        scratch_shapes=[tiles, tiles, pltpu.SemaphoreType.DMA((2,))],
    )
    return pl.pallas_call(
        kern,
        grid_spec=grid_spec,
        out_shape=jax.ShapeDtypeStruct((bs, N_HEADS, HEAD_DIM), MXU_DTYPE),
        compiler_params=_cparams(1),
        name="attn_sample_gather",
    )(sel_idx, page_table.reshape(-1), q_heads, cache5, kv_new, win5, gates3, o_cmp)


def _tiles(sp, n_prompt_tokens, n_pages):
    return {
        "in_proj_rows": min(sp, 1024),
        "attn_queries": LANES,
        "attn_keys": 512 if sp % 512 == 0 else 256,
        "conv_rows": 64,
        "merge_rows": min(sp, 512),
        "compress_pages": min(n_pages, 64),
        "expert_rows": 512,
        "dispatch_tokens": min(n_prompt_tokens, 1024),
        "combine_tokens": min(sp, LANES),
    }


def _prep_weights(w_in, w_a, w_b, w_out, router_w, router_b, sh_w1, sh_w3, sh_w2, g_norm1, g_norm2):
    d = w_in.shape[0]
    n_q = N_HEADS * HEAD_DIM
    n_kv = 6 * KV_HEADS * HEAD_DIM
    n_gate = 3 * N_HEADS
    o_gate = n_q + n_kv
    o_glu = o_gate + n_gate
    o_mrg = o_glu + d
    c = lambda a: a.astype(MXU_DTYPE)
    w_cat = jnp.concatenate(
        [w_in[:, :o_gate], w_in[:, o_glu:o_mrg],
         jnp.pad(w_in[:, o_gate:o_glu], ((0, 0), (0, LANES - n_gate)))], axis=1)
    mw = {"g1": g_norm1.reshape(1, d), "wm": c(w_in[:, o_mrg:]), "wa": c(w_a), "wb": c(w_b),
          "wo": c(w_out), "g2": g_norm2.reshape(1, d), "rwt": c(router_w.T),
          "rb": router_b.reshape(-1, 1), "s13": c(jnp.concatenate([sh_w1, sh_w3], axis=1)),
          "s2": c(sh_w2)}
    return c(w_cat), mw


def kernel(x_prompt, x_sample, cache_kv, cache_win, state_conv, page_table, c_prompt, c_sample, w_ada, b_ada, g_norm1, w_in, cmp_pe, cmp_w1, cmp_w2, conv_w, conv_b, conv_ln_g, conv_ln_b, w_a, w_b, w_out, g_norm2, router_w, router_b, exp_w1, exp_w3, exp_w2, sh_w1, sh_w3, sh_w2, g_final):
    l = 0
    assert w_ada.shape[0] == 1 and x_sample.shape[1] == 1 and cache_win.shape[2] == WINDOW
    bp, sp, d = x_prompt.shape
    bs = x_sample.shape[0]
    n_pool, page_rows = cache_kv.shape[1], cache_kv.shape[2]
    n_pages = page_table.shape[1]
    past_len = n_pages * page_rows
    hd = HEAD_DIM
    c_conv = d // 2
    n_sub = d // LANES
    n_kv_new = 4 * KV_HEADS * hd
    tiles = _tiles(sp, bp * sp, n_pages)
    rb = tiles["expert_rows"]

    mod = _ada_call(jnp.concatenate([c_prompt, c_sample], axis=0), w_ada[l], b_ada[l])
    mods_p = [mod[:bp, None, k * d:(k + 1) * d] for k in range(6)]
    mods_s = [mod[None, bp:, k * d:(k + 1) * d] for k in range(6)]
    w_cat, mw = _prep_weights(w_in[l], w_a[l], w_b[l], w_out[l], router_w[l], router_b[l],
                              sh_w1[l], sh_w3[l], sh_w2[l], g_norm1[l], g_norm2[l])
    cw = _compress_weights(cmp_pe[l], cmp_w1[l], cmp_w2[l])

    sh1, sc1, gt1, sh2, sc2, gt2 = mods_p
    q, kvw, glu, gates = _inproj_call(x_prompt, sc1, sh1, g_norm1[l], w_cat, tiles["in_proj_rows"])
    kcvc = _compress_prompt_call(kvw, cw)
    o_attn = _attn_prompt_call(q, kvw, kcvc, gates, tiles["attn_queries"], tiles["attn_keys"])
    z = _conv_prompt_call(glu, conv_w[l], conv_b[l], conv_ln_g[l], conv_ln_b[l], tiles["conv_rows"])
    x1_p, h2_p, ysh_p, eid_p, ew_p, rank_p, cnt_p = _merge_call(
        x_prompt, o_attn, z, (sc1, sh1, gt1, sc2, sh2), mw, tiles["merge_rows"])

    sh1s, sc1s, gt1s, sh2s, sc2s, gt2s = mods_s
    xs = x_sample.reshape(1, bs, d)
    q_s, kvw_s, glu_s, gates_s = _inproj_call(xs, sc1s, sh1s, g_norm1[l], w_cat, bs)
    cache5 = jnp.transpose(cache_kv[l], (0, 2, 3, 4, 1))
    win5 = jnp.transpose(cache_win[l], (0, 2, 3, 4, 1))
    fs = _compress_sample_call(page_table, cache5, cw["w1k"], cw["w1v"], tiles["compress_pages"])
    q_heads = q_s.reshape(bs, N_HEADS, hd)
    n_blk = -(-(past_len + 1) // SEL_BLOCK)
    top_n = min(TOP_N, n_blk)
    o_cmp_s, sel = _attn_sample_a_call(q_heads, fs, cw, past_len, n_blk)
    o_s = _attn_sample_b_call(
        sel[:, :KV_HEADS, :top_n].reshape(-1), page_table, q_heads, cache5,
        kvw_s.reshape(bs, 1, -1), win5,
        gates_s[0, :, :3 * N_HEADS].reshape(bs, N_HEADS, 3), o_cmp_s, past_len, top_n)
    glu_new = glu_s.reshape(bs, 1, c_conv)
    z_s = _conv_sample_call(jnp.transpose(state_conv[l], (1, 0, 2)), glu_s[0],
                            conv_w[l], conv_b[l], conv_ln_g[l], conv_ln_b[l])
    x1_s, h2_s, ysh_s, eid_s, ew_s, rank_s, cnt_s = _merge_call(
        xs, o_s.reshape(1, bs, N_HEADS * hd), z_s.reshape(1, bs, c_conv),
        (sc1s, sh1s, gt1s, sc2s, sh2s), mw, bs)

    n_p = bp * sp
    cnt_p = cnt_p[:, 0].astype(I32)
    cnt_s = cnt_s[:, 0].astype(I32)
    counts = cnt_p + cnt_s
    rank_s = rank_s + cnt_p[eid_s]
    start, padded, blk_e, n_used, n_blk = _moe_plan(counts, (n_p + bs) * TOP_K, rb)
    x_rows, dst_p, dst_s = _dispatch_call(start, counts, padded, n_used, eid_p, rank_p, eid_s,
                                          rank_s, h2_p, h2_s, n_blk, rb, tiles["dispatch_tokens"])
    y_rows = _experts_call(x_rows, blk_e, n_used, exp_w1[l], exp_w3[l], exp_w2[l], rb)
    y_prompt = _combine_call(dst_p, y_rows, ew_p.T, x1_p, ysh_p, gt2, g_final,
                             tiles["combine_tokens"])
    y_sample = _combine_call(dst_s, y_rows, ew_s.T, x1_s, ysh_s, gt2s, g_final, bs)

    w_keep = min(WINDOW, sp)
    kv_prompt = kvw[:, :, :n_kv_new].reshape(1, bp, sp, 4, KV_HEADS, hd)
    kv_sample = kvw_s[0, :, :n_kv_new].reshape(1, bs, 1, 4, KV_HEADS, hd)
    win_prompt = kvw[:, sp - w_keep:, n_kv_new:].reshape(1, bp, w_keep, 2, KV_HEADS, hd)
    win_new = kvw_s[0, :, n_kv_new:].reshape(bs, 1, 2, KV_HEADS, hd)
    win_sample = jnp.concatenate([cache_win[l][:, 1:], win_new], axis=1)[None]
    conv_prompt = glu[:, sp - (CONV_W - 1):][None]
    conv_sample = jnp.concatenate([state_conv[l][:, 1:], glu_new], axis=1)[None]
    return (y_prompt, y_sample.reshape(bs, 1, d), kv_prompt, kv_sample, win_prompt, win_sample,
            conv_prompt, conv_sample)
```

```python
import functools

import jax
import jax.numpy as jnp
from jax import lax
from jax.experimental import pallas as pl
from jax.experimental.pallas import tpu as pltpu

F32 = jnp.float32
BF16 = jnp.bfloat16
I32 = jnp.int32
MXU_DTYPE = jnp.bfloat16

N_HEADS = 8
KV_HEADS = 2
Q_PER_KV = N_HEADS // KV_HEADS
HEAD_DIM = 64
CMP_BLOCK = 32
CMP_STRIDE = 16
CMP_HID = 2 * HEAD_DIM
SEL_BLOCK = 64
TOP_N = 16
WINDOW = 512
CONV_W = 31
N_EXPERTS = 256
N_GROUPS = 8
TOPK_GROUPS = 4
TOP_K = 8
ROUTED_SCALE = 2.5
EPS = 1e-6
NEG_INF = -1e30
FORCE_SCORE = 1e4
ATTN_SCALE = HEAD_DIM ** -0.5

LANES = 128
SUBLANES = 8
VMEM_LIMIT = 56 * 1024 * 1024


def _cparams(n_axes):
    return pltpu.CompilerParams(
        dimension_semantics=("arbitrary",) * n_axes, vmem_limit_bytes=VMEM_LIMIT)


def _mm(a, b):
    return jnp.dot(a.astype(MXU_DTYPE), b.astype(MXU_DTYPE), preferred_element_type=F32)


def _mm_nt(a, b):
    return lax.dot_general(a.astype(MXU_DTYPE), b.astype(MXU_DTYPE),
                           (((1,), (1,)), ((), ())), preferred_element_type=F32)


def _mm_tn(a, b):
    return lax.dot_general(a.astype(MXU_DTYPE), b.astype(MXU_DTYPE),
                           (((0,), (0,)), ((), ())), preferred_element_type=F32)


def _sigmoid(x):
    return 1.0 / (1.0 + jnp.exp(-x))


def _silu(x):
    return x * _sigmoid(x)


def _rms(x, g):
    return x * lax.rsqrt(jnp.mean(x * x, axis=-1, keepdims=True) + EPS) * g


def _slope(head):
    return 2.0 ** (-8.0 * (head + 1) / N_HEADS)


def _ada_kernel(c_ref, w_ref, b_ref, o_ref):
    o_ref[...] = _mm(_silu(c_ref[...]), w_ref[...]) + b_ref[...]


def _ada_call(c, w_ada, b_ada):
    n, d = c.shape
    n_out = w_ada.shape[1]
    tn = n_out // 6
    return pl.pallas_call(
        _ada_kernel,
        grid=(n_out // tn,),
        in_specs=[pl.BlockSpec((n, d), lambda j: (0, 0)),
                  pl.BlockSpec((d, tn), lambda j: (0, j)),
                  pl.BlockSpec((1, tn), lambda j: (0, j))],
        out_specs=pl.BlockSpec((n, tn), lambda j: (0, j)),
        out_shape=jax.ShapeDtypeStruct((n, n_out), F32),
        compiler_params=_cparams(1),
        name="ada_mod",
    )(c, w_ada, b_ada.reshape(1, n_out))


def _inproj_kernel(x_ref, sc_ref, sh_ref, g_ref, w_ref, q_ref, kv_ref, glu_ref, gate_ref,
                   *, n_q, n_kv, c_conv):
    x = x_ref[0]
    h = _rms(x, g_ref[...]) * (1.0 + sc_ref[0]) + sh_ref[0]
    y = _mm(h, w_ref[...])
    o = 0
    q_ref[0] = y[:, o:o + n_q].astype(q_ref.dtype)
    o += n_q
    kv_ref[0] = y[:, o:o + n_kv]
    o += n_kv
    u_a = y[:, o:o + c_conv]
    u_g = y[:, o + c_conv:o + 2 * c_conv]
    glu_ref[0] = u_a * _sigmoid(u_g)
    o += 2 * c_conv
    gate_ref[0] = _sigmoid(y[:, o:o + LANES])


def _mod_spec(mod, tm, d):
    if mod.shape[1] == 1:
        return pl.BlockSpec((1, 1, d), lambda b, i: (b, 0, 0))
    return pl.BlockSpec((1, tm, d), lambda b, i: (b, i, 0))


def _inproj_call(x, sc1, sh1, g1, w_cat, tm):
    bx, tx, d = x.shape
    n_q = N_HEADS * HEAD_DIM
    n_kv = 6 * KV_HEADS * HEAD_DIM
    c_conv = d // 2
    n_cat = w_cat.shape[1]
    kern = functools.partial(_inproj_kernel, n_q=n_q, n_kv=n_kv, c_conv=c_conv)
    row = lambda w: pl.BlockSpec((1, tm, w), lambda b, i: (b, i, 0))
    return pl.pallas_call(
        kern,
        grid=(bx, tx // tm),
        in_specs=[row(d), _mod_spec(sc1, tm, d), _mod_spec(sh1, tm, d),
                  pl.BlockSpec((1, d), lambda b, i: (0, 0)),
                  pl.BlockSpec((d, n_cat), lambda b, i: (0, 0))],
        out_specs=[row(n_q), row(n_kv), row(c_conv), row(LANES)],
        out_shape=[jax.ShapeDtypeStruct((bx, tx, n_q), MXU_DTYPE),
                   jax.ShapeDtypeStruct((bx, tx, n_kv), F32),
                   jax.ShapeDtypeStruct((bx, tx, c_conv), F32),
                   jax.ShapeDtypeStruct((bx, tx, LANES), F32)],
        compiler_params=_cparams(2),
        name="in_proj",
    )(x, sc1, sh1, g1.reshape(1, d), w_cat)


def _cmp_first_layer(k_of_l, v_of_l, w1k_ref, w1v_ref):
    half = KV_HEADS * CMP_HID
    f_k = _mm(jnp.concatenate([k_of_l(l) for l in range(CMP_STRIDE)], axis=1), w1k_ref[...])
    f_v = _mm(jnp.concatenate([v_of_l(l) for l in range(CMP_STRIDE)], axis=1), w1v_ref[...])
    return jnp.concatenate([f_k[:, :half], f_v[:, :half], f_k[:, half:], f_v[:, half:]], axis=1)


def _cmp_second_layer(first, second_next, pe_ref, w1pe_ref, w2bd_ref):
    pe_term = _mm(pe_ref[...], w1pe_ref[...])[0:1]
    return _mm(_silu(first + second_next + pe_term), w2bd_ref[...])


def _compress_prompt_kernel(k_ref, v_ref, w1k_ref, w1v_ref, pe_ref, w1pe_ref, w2bd_ref, o_ref, fs_ref,
                            *, n_chunks):
    half = 4 * CMP_HID
    rows_of = lambda ref: (lambda l: ref.at[0][pl.ds(l, n_chunks, stride=CMP_STRIDE), :])
    fs_ref[pl.ds(0, n_chunks), :] = _cmp_first_layer(rows_of(k_ref), rows_of(v_ref), w1k_ref, w1v_ref)
    fs_ref[pl.ds(n_chunks, SUBLANES), :] = jnp.zeros((SUBLANES, 2 * half), F32)
    first = fs_ref[pl.ds(0, n_chunks), pl.ds(0, half)]
    second_next = fs_ref[pl.ds(1, n_chunks), pl.ds(half, half)]
    o_ref[0] = _cmp_second_layer(first, second_next, pe_ref, w1pe_ref, w2bd_ref)


def _compress_prompt_call(kvw, cw):
    b, t, _ = kvw.shape
    n_chunks = t // CMP_STRIDE
    wid = 4 * HEAD_DIM
    kern = functools.partial(_compress_prompt_kernel, n_chunks=n_chunks)
    full = lambda a: pl.BlockSpec(a.shape, lambda i: (0,) * a.ndim)
    return pl.pallas_call(
        kern,
        grid=(b,),
        in_specs=[pl.BlockSpec((1, t, LANES), lambda i: (i, 0, 0)),
                  pl.BlockSpec((1, t, LANES), lambda i: (i, 0, 1)),
                  full(cw["w1k"]), full(cw["w1v"]), full(cw["pe"]), full(cw["w1pe"]), full(cw["w2bd"])],
        out_specs=pl.BlockSpec((1, n_chunks, wid), lambda i: (i, 0, 0)),
        out_shape=jax.ShapeDtypeStruct((b, n_chunks, wid), F32),
        scratch_shapes=[pltpu.VMEM((n_chunks + SUBLANES, 8 * CMP_HID), F32)],
        compiler_params=_cparams(1),
        name="compress_prompt",
    )(kvw, kvw, cw["w1k"], cw["w1v"], cw["pe"], cw["w1pe"], cw["w2bd"])


def _compress_weights(cmp_pe, cmp_w1, cmp_w2):
    hd, hid = HEAD_DIM, CMP_HID
    slab_kv = (0, 0, 1, 1)

    def first_layer(kv):
        w = jnp.zeros((CMP_STRIDE, KV_HEADS, hd, 2, KV_HEADS, hid), F32)
        for g in range(KV_HEADS):
            w = w.at[:, g, :, 0, g, :].set(cmp_w1[kv, :CMP_STRIDE])
            w = w.at[:, g, :, 1, g, :].set(cmp_w1[kv, CMP_STRIDE:])
        return w.reshape(CMP_STRIDE * KV_HEADS * hd, 2 * KV_HEADS * hid).astype(MXU_DTYPE)

    w2bd = jnp.zeros((4 * hid, 4 * hd), F32)
    for s, kv in enumerate(slab_kv):
        w2bd = w2bd.at[s * hid:(s + 1) * hid, s * hd:(s + 1) * hd].set(cmp_w2[kv])
    pe = jnp.broadcast_to(cmp_pe.reshape(1, -1), (SUBLANES, 2 * CMP_BLOCK * hd))
    w1pe = jnp.zeros((2 * CMP_BLOCK * hd, 4 * hid), F32)
    n_flat = CMP_BLOCK * hd
    for s, kv in enumerate(slab_kv):
        w1pe = w1pe.at[kv * n_flat:(kv + 1) * n_flat, s * hid:(s + 1) * hid].set(
            cmp_w1[kv].reshape(n_flat, hid))
    return {"w1k": first_layer(0), "w1v": first_layer(1), "w2bd": w2bd.astype(MXU_DTYPE),
            "pe": pe, "w1pe": w1pe.astype(MXU_DTYPE)}


def _softmax_cols(s, mask):
    m = jnp.max(jnp.where(mask, s, NEG_INF), axis=0, keepdims=True)
    p = jnp.where(mask, jnp.exp(s - m), 0.0)
    l = jnp.sum(p, axis=0, keepdims=True)
    return p / jnp.where(l > 0.0, l, 1.0)


def _top_n_rows(imp, top_n):
    n = imp.shape[0]
    row_id = lax.broadcasted_iota(I32, imp.shape, 0)
    beaten = jnp.zeros(imp.shape, F32)
    for m in range(n):
        other = imp[m:m + 1, :]
        wins = (other > imp) | ((other == imp) & (row_id > m))
        beaten = beaten + jnp.where(wins, 1.0, 0.0)
    return beaten < float(top_n)


SEL_FEAT0 = HEAD_DIM + SUBLANES
MASK_BIG = -2.0 ** 100


def _attn_prompt_kernel(q_ref, ksv_ref, kwv_ref, kc_ref, gate_ref, o_ref,
                        kaug, vsel, kwaug, vwin, psum_ref, *, tq, kc_len, t_len):
    qt = pl.program_id(1)
    q0 = qt * tq
    hd = HEAD_DIM
    n_cmp_rows = kc_ref.shape[1]
    n_blk = t_len // SEL_BLOCK
    top_n = min(TOP_N, n_blk)
    wk = min(WINDOW + tq, t_len)

    @pl.when(qt == 0)
    def _():
        lane = lax.broadcasted_iota(I32, (t_len, LANES), 1)
        pos = lax.broadcasted_iota(I32, (t_len, LANES), 0)
        pos_hi = lax.div(pos, SEL_BLOCK)
        alibi = jnp.where(lane < hd + 2, 1.0,
                          jnp.where(lane == hd + 2, pos_hi.astype(F32),
                                    jnp.where(lane == hd + 3, lax.rem(pos, SEL_BLOCK).astype(F32), 0.0)))
        in_blk = jnp.where(lane - SEL_FEAT0 == pos_hi, 1.0, 0.0)
        for g in range(KV_HEADS):
            k_s = ksv_ref[0, :, 0:LANES]
            k_w = kwv_ref[0, :, 0:LANES]
            if g:
                k_s = pltpu.roll(k_s, LANES - g * hd, 1)
                k_w = pltpu.roll(k_w, LANES - g * hd, 1)
            kaug[g] = jnp.where(lane < hd, k_s, alibi + in_blk).astype(kaug.dtype)
            kwaug[g] = jnp.where(lane < hd, k_w, alibi).astype(kwaug.dtype)
            v_s = ksv_ref[0, :, LANES:2 * LANES]
            v_w = kwv_ref[0, :, LANES:2 * LANES]
            if g:
                v_s = pltpu.roll(v_s, LANES - g * hd, 1)
                v_w = pltpu.roll(v_w, LANES - g * hd, 1)
            ones_lane = jnp.where(lane == hd, 1.0, 0.0)
            vsel[g] = jnp.where(lane < hd, v_s, ones_lane).astype(vsel.dtype)
            vwin[g] = jnp.where(lane < hd, v_w, ones_lane).astype(vwin.dtype)
        psum_ref[...] = jnp.zeros(psum_ref.shape, F32)

    q_blk = q_ref[0]
    gates = gate_ref[0]
    rows = Q_PER_KV * tq
    row_t = q0 + lax.rem(lax.broadcasted_iota(I32, (rows, 1), 0), tq)
    lane_q = lax.broadcasted_iota(I32, (tq, LANES), 1)
    t_q = q0 + lax.broadcasted_iota(I32, (tq, LANES), 0)
    t_hi = (lax.div(t_q, SEL_BLOCK) * SEL_BLOCK).astype(F32)
    t_lo = lax.rem(t_q, SEL_BLOCK).astype(F32)
    w0 = pl.multiple_of(jnp.maximum(q0 + tq - wk, 0), tq)
    dist_w = row_t - (w0 + lax.broadcasted_iota(I32, (rows, wk), 1))
    band = jnp.where((dist_w >= 0) & (dist_w < WINDOW), 0.0, NEG_INF)
    out_heads = []
    for g in range(KV_HEADS):
        q_heads = [q_blk[:, (g * Q_PER_KV + r) * hd:(g * Q_PER_KV + r + 1) * hd]
                   for r in range(Q_PER_KV)]
        k_cmp = kc_ref[0, :, g * hd:(g + 1) * hd]
        v_cmp = kc_ref[0, :, (2 + g) * hd:(3 + g) * hd]
        t_lane = q0 + lax.broadcasted_iota(I32, (n_cmp_rows, tq), 1)
        c_pos = lax.broadcasted_iota(I32, (n_cmp_rows, tq), 0) * CMP_STRIDE + (CMP_BLOCK - 1)
        dist_c = t_lane - c_pos
        mask_c = dist_c >= 0
        dist_cf = dist_c.astype(F32)
        o_cmp = []
        p_sum = None
        for r in range(Q_PER_KV):
            s = _mm_nt(k_cmp, q_heads[r]) * ATTN_SCALE - _slope(g * Q_PER_KV + r) * dist_cf
            p = _softmax_cols(s, mask_c)
            o_cmp.append(_mm_tn(p, v_cmp))
            p_sum = p if p_sum is None else p_sum + p
        per = SEL_BLOCK // CMP_STRIDE
        for h in range(tq // LANES):
            psum_ref[h, pl.ds(SUBLANES, n_cmp_rows), :] = p_sum[:, h * LANES:(h + 1) * LANES]
        taps = [jnp.concatenate(
            [psum_ref.at[h][pl.ds(SUBLANES - 1 + k, n_blk, stride=per), :] for h in range(tq // LANES)],
            axis=1) for k in range(per + 1)]
        imp = 0.5 * taps[0] + 0.5 * taps[per]
        for k in range(1, per):
            imp = imp + taps[k]
        blk = lax.broadcasted_iota(I32, (n_blk, tq), 0)
        t_blk = q0 + lax.broadcasted_iota(I32, (n_blk, tq), 1)
        cur = lax.div(t_blk, SEL_BLOCK)
        forced = (blk == 0) | (blk == cur) | (blk == cur - 1)
        imp = jnp.where(forced, FORCE_SCORE, imp)
        imp = jnp.where(blk * SEL_BLOCK <= t_blk, imp, -jnp.inf)
        sel_neg = jnp.where(_top_n_rows(imp, top_n), 0.0, MASK_BIG)
        sel_lanes = jnp.transpose(jnp.concatenate(
            [jnp.zeros((SEL_FEAT0, tq), F32), sel_neg,
             jnp.zeros((LANES - SEL_FEAT0 - n_blk, tq), F32)], axis=0))
        q_aug = []
        for r in range(Q_PER_KV):
            head = g * Q_PER_KV + r
            slope = _slope(head)
            pair = q_blk[:, (head // 2) * LANES:(head // 2 + 1) * LANES].astype(F32)
            if head % 2:
                pair = pltpu.roll(pair, LANES - hd, 1)
            feats = jnp.where(lane_q == hd, -slope * t_hi,
                              jnp.where(lane_q == hd + 1, -slope * t_lo,
                                        jnp.where(lane_q == hd + 2, slope * SEL_BLOCK,
                                                  jnp.where(lane_q == hd + 3, slope, sel_lanes))))
            q_aug.append(jnp.where(lane_q < hd, pair * ATTN_SCALE, feats).astype(MXU_DTYPE))
        q_aug = jnp.concatenate(q_aug, axis=0)

        def sel_chunk(j, carry, causal):
            m_run, acc = carry
            k0 = pl.multiple_of(j * kc_len, kc_len)
            s = _mm_nt(q_aug, kaug[g, pl.ds(k0, kc_len), :])
            if causal:
                k_pos = k0 + lax.broadcasted_iota(I32, (rows, kc_len), 1)
                s = jnp.where(k_pos <= row_t, s, NEG_INF)
            m_new = jnp.maximum(m_run, jnp.max(s, axis=1, keepdims=True))
            p = jnp.exp((s - m_new).astype(MXU_DTYPE))
            acc_new = jnp.exp(m_run - m_new) * acc + _mm(p, vsel[g, pl.ds(k0, kc_len), :])
            return m_new, acc_new

        n_full = lax.div(q0, kc_len)
        init = (jnp.full((rows, 1), NEG_INF, F32), jnp.zeros((rows, LANES), F32))
        carry = lax.fori_loop(0, n_full, functools.partial(sel_chunk, causal=False), init)
        _, acc_sel = sel_chunk(n_full, carry, causal=True)
        o_sel = acc_sel[:, :hd] / acc_sel[:, hd:hd + 1]
        s = _mm_nt(q_aug, kwaug[g, pl.ds(w0, wk), :]) + band
        p = jnp.exp((s - jnp.max(s, axis=1, keepdims=True)).astype(MXU_DTYPE))
        acc_win = _mm(p, vwin[g, pl.ds(w0, wk), :])
        o_win = acc_win[:, :hd] / acc_win[:, hd:hd + 1]
        for r in range(Q_PER_KV):
            c = (g * Q_PER_KV + r) * 3
            out_heads.append(gates[:, c:c + 1] * o_cmp[r]
                             + gates[:, c + 1:c + 2] * o_sel[r * tq:(r + 1) * tq]
                             + gates[:, c + 2:c + 3] * o_win[r * tq:(r + 1) * tq])
    o_ref[0] = jnp.concatenate(out_heads, axis=1).astype(o_ref.dtype)


def _attn_prompt_call(q, kvw, kcvc, gates, tq, kc_len):
    b, t, n_q = q.shape
    wid = 4 * HEAD_DIM
    kern = functools.partial(_attn_prompt_kernel, tq=tq, kc_len=kc_len, t_len=t)
    assert SEL_FEAT0 + t // SEL_BLOCK <= LANES and kc_len % tq == 0
    k_scratch = pltpu.VMEM((KV_HEADS, t, LANES), MXU_DTYPE)
    v_scratch = pltpu.VMEM((KV_HEADS, t, LANES), MXU_DTYPE)
    n_cmp_rows = kcvc.shape[1]
    return pl.pallas_call(
        kern,
        grid=(b, t // tq),
        in_specs=[pl.BlockSpec((1, tq, n_q), lambda i, j: (i, j, 0)),
                  pl.BlockSpec((1, t, wid), lambda i, j: (i, 0, 1)),
                  pl.BlockSpec((1, t, wid), lambda i, j: (i, 0, 2)),
                  pl.BlockSpec((1, n_cmp_rows, wid), lambda i, j: (i, 0, 0)),
                  pl.BlockSpec((1, tq, LANES), lambda i, j: (i, j, 0))],
        out_specs=pl.BlockSpec((1, tq, n_q), lambda i, j: (i, j, 0)),
        out_shape=jax.ShapeDtypeStruct((b, t, n_q), MXU_DTYPE),
        scratch_shapes=[k_scratch, v_scratch, k_scratch, v_scratch,
                        pltpu.VMEM((tq // LANES, n_cmp_rows + 2 * SUBLANES, LANES), F32)],
        compiler_params=_cparams(2),
        name="attn_prompt",
    )(q, kvw, kvw, kcvc, gates)


CONV_PAD = 32


def _ln_silu(z, lg, lb):
    mu = jnp.mean(z, axis=-1, keepdims=True)
    zc = z - mu
    var = jnp.mean(zc * zc, axis=-1, keepdims=True)
    return _silu(zc * lax.rsqrt(var + EPS) * lg + lb)


def _conv_prompt_kernel(glu_ref, cw_ref, cb_ref, lg_ref, lb_ref, z_ref, full_ref, *, t_len, tt):
    full_ref[pl.ds(0, CONV_PAD), :] = jnp.zeros((CONV_PAD, full_ref.shape[1]), F32)
    full_ref[pl.ds(CONV_PAD, t_len), :] = glu_ref[0]
    first = CONV_PAD - (CONV_W - 1)

    def tile(i, carry):
        r0 = pl.multiple_of(i * tt, tt)
        acc = jnp.zeros((tt, full_ref.shape[1]), F32) + cb_ref[...]
        win = full_ref[pl.ds(r0, tt + CONV_PAD), :]
        for phase in range(SUBLANES):
            offs = [o for o in range(first, first + CONV_W) if o % SUBLANES == phase]
            if not offs:
                continue
            shifted = win[phase:max(offs) + tt]
            for o in offs:
                acc = acc + shifted[o - phase:o - phase + tt] * cw_ref[o - first:o - first + 1, :]
        z_ref[0, pl.ds(r0, tt), :] = _ln_silu(acc, lg_ref[...], lb_ref[...]).astype(z_ref.dtype)
        return carry

    lax.fori_loop(0, t_len // tt, tile, 0)


def _conv_prompt_call(glu, conv_w, conv_b, ln_g, ln_b, tt):
    b, t, c = glu.shape
    kern = functools.partial(_conv_prompt_kernel, t_len=t, tt=tt)
    vec = pl.BlockSpec((1, c), lambda i: (0, 0))
    return pl.pallas_call(
        kern,
        grid=(b,),
        in_specs=[pl.BlockSpec((1, t, c), lambda i: (i, 0, 0)),
                  pl.BlockSpec((CONV_PAD, c), lambda i: (0, 0)), vec, vec, vec],
        out_specs=pl.BlockSpec((1, t, c), lambda i: (i, 0, 0)),
        out_shape=jax.ShapeDtypeStruct((b, t, c), MXU_DTYPE),
        scratch_shapes=[pltpu.VMEM((CONV_PAD + t, c), F32)],
        compiler_params=_cparams(1),
        name="conv_prompt",
    )(glu, jnp.pad(conv_w, ((0, CONV_PAD - CONV_W), (0, 0))), conv_b.reshape(1, c),
      ln_g.reshape(1, c), ln_b.reshape(1, c))


def _conv_sample_kernel(hist_ref, new_ref, cw_ref, cb_ref, lg_ref, lb_ref, z_ref):
    z = new_ref[...] * cw_ref[CONV_W - 1:CONV_W, :] + cb_ref[...]
    for w in range(CONV_W - 1):
        z = z + hist_ref[w] * cw_ref[w:w + 1, :]
    z_ref[...] = _ln_silu(z, lg_ref[...], lb_ref[...]).astype(z_ref.dtype)


def _conv_sample_call(hist, glu_new, conv_w, conv_b, ln_g, ln_b):
    bs, c = glu_new.shape
    return pl.pallas_call(
        _conv_sample_kernel,
        out_shape=jax.ShapeDtypeStruct((bs, c), MXU_DTYPE),
        name="conv_sample",
    )(hist, glu_new, jnp.pad(conv_w, ((0, CONV_PAD - CONV_W), (0, 0))), conv_b.reshape(1, c),
      ln_g.reshape(1, c), ln_b.reshape(1, c))


def _route_tile(h2, rwt_ref, rb_ref, cnt_ref):
    n_e = N_EXPERTS
    per_grp = n_e // N_GROUPS
    tm = h2.shape[0]
    aff = _sigmoid(_mm_nt(rwt_ref[...], h2))
    biased = aff + rb_ref[...]
    neg = -jnp.inf
    g_rows = []
    for g in range(N_GROUPS):
        v = biased[g * per_grp:(g + 1) * per_grp]
        m1 = jnp.max(v, axis=0, keepdims=True)
        is_m1 = v == m1
        n_m1 = jnp.sum(jnp.where(is_m1, 1.0, 0.0), axis=0, keepdims=True)
        m2 = jnp.max(jnp.where(is_m1, neg, v), axis=0, keepdims=True)
        g_rows.append(m1 + jnp.where(n_m1 >= 2.0, m1, m2))
    g_keep = _top_n_rows(jnp.concatenate(g_rows, axis=0), TOPK_GROUPS)
    cur = jnp.concatenate(
        [jnp.where(g_keep[g:g + 1], biased[g * per_grp:(g + 1) * per_grp], neg)
         for g in range(N_GROUPS)], axis=0)
    row_id = lax.broadcasted_iota(I32, (n_e, tm), 0).astype(F32)
    ids, wts, hots = [], [], []
    for _ in range(TOP_K):
        m = jnp.max(cur, axis=0, keepdims=True)
        idx = jnp.min(jnp.where(cur == m, row_id, float(n_e)), axis=0, keepdims=True)
        hot = row_id == idx
        ids.append(idx)
        wts.append(jnp.sum(jnp.where(hot, aff, 0.0), axis=0, keepdims=True))
        hots.append(hot)
        cur = jnp.where(hot, neg, cur)
    w_sum = wts[0]
    for w in wts[1:]:
        w_sum = w_sum + w
    wts = [w / w_sum * ROUTED_SCALE for w in wts]
    hot_all = jnp.where(hots[0], 1.0, 0.0)
    for hot in hots[1:]:
        hot_all = hot_all + jnp.where(hot, 1.0, 0.0)
    earlier = jnp.where(lax.broadcasted_iota(I32, (tm, tm), 0) < lax.broadcasted_iota(I32, (tm, tm), 1),
                        1.0, 0.0)
    before = cnt_ref[:, 0:1] + jnp.dot(hot_all.astype(BF16), earlier.astype(BF16),
                                       preferred_element_type=F32)
    ranks = [jnp.sum(jnp.where(hot, before, 0.0), axis=0, keepdims=True) for hot in hots]
    cnt_ref[...] = cnt_ref[...] + jnp.sum(hot_all, axis=1, keepdims=True)
    cat = lambda rows: jnp.concatenate(rows, axis=0)
    return cat(ids).astype(I32), cat(wts), cat(ranks).astype(I32)


def _merge_kernel(x_ref, oa_ref, z_ref, sc1_ref, sh1_ref, gt1_ref, sc2_ref, sh2_ref,
                  g1_ref, wm_ref, wa_ref, wb_ref, wo_ref, g2_ref, rwt_ref, rb_ref,
                  s13_ref, s2_ref,
                  x1_ref, h2_ref, ysh_ref, eid_ref, ew_ref, rk_ref, cnt_out_ref, cnt_ref, *, d_exp):
    first_step = (pl.program_id(0) == 0) & (pl.program_id(1) == 0)

    @pl.when(first_step)
    def _():
        cnt_ref[...] = jnp.zeros(cnt_ref.shape, F32)

    x = x_ref[0]
    d = x.shape[1]
    tm = x.shape[0]
    h = _rms(x, g1_ref[...]) * (1.0 + sc1_ref[0]) + sh1_ref[0]
    g_mrg = _sigmoid(_mm(h, wm_ref[...]))
    mixed = g_mrg[:, :d] * _mm(oa_ref[0], wa_ref[...]) + g_mrg[:, d:] * _mm(z_ref[0], wb_ref[...])
    x1 = x + gt1_ref[0] * _mm(mixed, wo_ref[...])
    x1_ref[0] = x1
    h2 = _rms(x1, g2_ref[...]) * (1.0 + sc2_ref[0]) + sh2_ref[0]
    for s in range(d // LANES):
        h2_ref[pl.ds(s, tm, stride=d // LANES), :] = h2[:, s * LANES:(s + 1) * LANES]
    hs = _mm(h2, s13_ref[...])
    ysh_ref[0] = _mm(_silu(hs[:, :d_exp]) * hs[:, d_exp:], s2_ref[...])
    ids, wts, ranks = _route_tile(h2, rwt_ref, rb_ref, cnt_ref)
    eid_ref[...] = ids
    ew_ref[...] = wts
    rk_ref[...] = ranks
    cnt_out_ref[...] = cnt_ref[...]


def _merge_call(x, o_attn, z, mods, mw, tm):
    bx, tx, d = x.shape
    n = bx * tx
    nt = tx // tm
    d_exp = mw["s2"].shape[0]
    kern = functools.partial(_merge_kernel, d_exp=d_exp)
    row = lambda w: pl.BlockSpec((1, tm, w), lambda b, i: (b, i, 0))
    full = lambda a: pl.BlockSpec(a.shape, lambda b, i: (0,) * a.ndim)
    tok = pl.BlockSpec((TOP_K, tm), lambda b, i: (0, b * nt + i))
    wnames = ("g1", "wm", "wa", "wb", "wo", "g2", "rwt", "rb", "s13", "s2")
    return pl.pallas_call(
        kern,
        grid=(bx, nt),
        in_specs=[row(d), row(o_attn.shape[2]), row(z.shape[2])]
                 + [_mod_spec(m, tm, d) for m in mods]
                 + [full(mw[k]) for k in wnames],
        out_specs=[row(d),
                   pl.BlockSpec((tm * (d // LANES), LANES), lambda b, i: (b * nt + i, 0)),
                   row(d), tok, tok, tok,
                   pl.BlockSpec((N_EXPERTS, LANES), lambda b, i: (0, 0))],
        out_shape=[jax.ShapeDtypeStruct((bx, tx, d), F32),
                   jax.ShapeDtypeStruct((n * (d // LANES), LANES), F32),
                   jax.ShapeDtypeStruct((bx, tx, d), F32),
                   jax.ShapeDtypeStruct((TOP_K, n), I32),
                   jax.ShapeDtypeStruct((TOP_K, n), F32),
                   jax.ShapeDtypeStruct((TOP_K, n), I32),
                   jax.ShapeDtypeStruct((N_EXPERTS, LANES), F32)],
        scratch_shapes=[pltpu.VMEM((N_EXPERTS, LANES), F32)],
        compiler_params=_cparams(2),
        name="merge_route",
    )(x, o_attn, z, *mods, *[mw[k] for k in wnames])


def _start_tile_copy(src_hbm, src_row, buf, slot, dst_row, sem, n_sub):
    pltpu.make_async_copy(src_hbm.at[pl.ds(pl.multiple_of(src_row, n_sub), n_sub), :],
                          buf.at[slot, pl.ds(pl.multiple_of(dst_row, n_sub), n_sub), :],
                          sem.at[slot]).start()


def _wait_slot(buf, sem, slot):
    pltpu.make_async_copy(buf.at[slot], buf.at[slot], sem.at[slot]).wait()


def _moe_plan(counts, n_asg, rb):
    n_blk = -(-(n_asg + N_EXPERTS * (rb - 1)) // rb)
    padded = (counts + rb - 1) // rb * rb
    pad_end = jnp.cumsum(padded)
    start = pad_end - padded
    blk_row = jnp.arange(n_blk, dtype=I32)[:, None] * rb
    blk_e = jnp.minimum(jnp.sum((pad_end[None, :] <= blk_row).astype(I32), axis=1), N_EXPERTS - 1)
    n_used = (pad_end[-1] // rb).astype(I32).reshape(1)
    return start.astype(I32), padded.astype(I32), blk_e.astype(I32), n_used, n_blk


def _dispatch_kernel(start_ref, cnt_ref, pad_ref, n_used_ref,
                     eid_ref, rk_ref, eid_s_ref, rk_s_ref, h2p_ref, h2s_ref,
                     xs_hbm, dst_ref, dst_s_ref, zeros, sem,
                     *, tmd, n_p_steps, bs, n_sub, rb, n_blk):
    i = pl.program_id(0)

    def scatter_tokens(src_ref, e_ref, r_ref, d_ref, n_tok):
        def issue(t, carry):
            src = src_ref.at[pl.ds(pl.multiple_of(t * n_sub, n_sub), n_sub), :]
            for k in range(TOP_K):
                dst_row = (start_ref[e_ref[k, t]] + r_ref[k, t]) * n_sub
                d_ref[k, t] = dst_row
                pltpu.make_async_copy(
                    src, xs_hbm.at[pl.ds(pl.multiple_of(dst_row, n_sub), n_sub), :], sem).start()
            return carry
        lax.fori_loop(0, n_tok, issue, 0)
        done = xs_hbm.at[pl.ds(0, n_tok * TOP_K * n_sub), :]
        pltpu.make_async_copy(done, done, sem).wait()

    @pl.when(i < n_p_steps)
    def _():
        scatter_tokens(h2p_ref, eid_ref, rk_ref, dst_ref, tmd)

    @pl.when(i == n_p_steps)
    def _():
        scatter_tokens(h2s_ref, eid_s_ref, rk_s_ref, dst_s_ref, bs)
        zeros[...] = jnp.zeros(zeros.shape, F32)

        def zero_rows(first_row, n_rows):
            pltpu.make_async_copy(
                zeros.at[pl.ds(0, n_rows * n_sub), :],
                xs_hbm.at[pl.ds(pl.multiple_of(first_row * n_sub, n_sub), n_rows * n_sub), :],
                sem).start()

        def pad_expert(e, carry):
            n_pad = pad_ref[e] - cnt_ref[e]
            row = start_ref[e] + cnt_ref[e]
            piece = rb // 2
            while piece >= 1:
                take = (n_pad & piece) != 0

                @pl.when(take)
                def _(row=row, piece=piece):
                    zero_rows(row, piece)

                row = row + jnp.where(take, piece, 0)
                piece //= 2
            return carry

        lax.fori_loop(0, N_EXPERTS, pad_expert, 0)

        def pad_block(blk, carry):
            zero_rows(blk * rb, rb)
            return carry

        lax.fori_loop(n_used_ref[0], n_blk, pad_block, 0)
        n_zero = n_blk * rb - (n_p_steps * tmd + bs) * TOP_K
        done = xs_hbm.at[pl.ds(0, n_zero * n_sub), :]
        pltpu.make_async_copy(done, done, sem).wait()


def _dispatch_call(start, counts, padded, n_used, eid_p, rank_p, eid_s, rank_s, h2_p, h2_s,
                   n_blk, rb, tmd):
    n_p = eid_p.shape[1]
    bs = eid_s.shape[1]
    n_sub = h2_p.shape[0] // n_p
    n_p_steps = n_p // tmd
    kern = functools.partial(_dispatch_kernel, tmd=tmd, n_p_steps=n_p_steps, bs=bs, n_sub=n_sub,
                             rb=rb, n_blk=n_blk)
    last = n_p_steps - 1
    tile = pl.BlockSpec((TOP_K, tmd), lambda i, *_: (0, jnp.minimum(i, last)), memory_space=pltpu.SMEM)
    whole = pl.BlockSpec((TOP_K, bs), lambda i, *_: (0, 0), memory_space=pltpu.SMEM)
    grid_spec = pltpu.PrefetchScalarGridSpec(
        num_scalar_prefetch=4,
        grid=(n_p_steps + 1,),
        in_specs=[tile, tile, whole, whole,
                  pl.BlockSpec((tmd * n_sub, LANES), lambda i, *_: (jnp.minimum(i, last), 0)),
                  pl.BlockSpec((bs * n_sub, LANES), lambda i, *_: (0, 0))],
        out_specs=[pl.BlockSpec(memory_space=pl.ANY), tile, whole],
        scratch_shapes=[pltpu.VMEM((rb * n_sub, LANES), F32), pltpu.SemaphoreType.DMA],
    )
    return pl.pallas_call(
        kern,
        grid_spec=grid_spec,
        out_shape=[jax.ShapeDtypeStruct((n_blk * rb * n_sub, LANES), F32),
                   jax.ShapeDtypeStruct((TOP_K, n_p), I32),
                   jax.ShapeDtypeStruct((TOP_K, bs), I32)],
        compiler_params=_cparams(1),
        name="moe_dispatch",
    )(start, counts, padded, n_used, eid_p, rank_p, eid_s, rank_s, h2_p, h2_s)


def _experts_kernel(blk_e_ref, n_used_ref, x_ref, w1_ref, w3_ref, w2_ref, y_ref, *, rb, n_sub):
    i = pl.program_id(0)

    @pl.when(i < n_used_ref[0])
    def _():
        x = jnp.concatenate([x_ref[pl.ds(s, rb, stride=n_sub), :] for s in range(n_sub)], axis=1)
        hid = _silu(_mm(x, w1_ref[0])) * _mm(x, w3_ref[0])
        y = _mm(hid, w2_ref[0])
        for s in range(n_sub):
            y_ref[pl.ds(s, rb, stride=n_sub), :] = y[:, s * LANES:(s + 1) * LANES]

    @pl.when(i >= n_used_ref[0])
    def _():
        y_ref[...] = jnp.zeros(y_ref.shape, F32)


def _experts_call(x_rows, blk_e, n_used, w1, w3, w2, rb):
    _, d, d_exp = w1.shape
    n_sub = d // LANES
    n_blk = x_rows.shape[0] // (rb * n_sub)
    kern = functools.partial(_experts_kernel, rb=rb, n_sub=n_sub)
    grid_spec = pltpu.PrefetchScalarGridSpec(
        num_scalar_prefetch=2,
        grid=(n_blk,),
        in_specs=[pl.BlockSpec((rb * n_sub, LANES), lambda i, be, nu: (jnp.minimum(i, nu[0] - 1), 0)),
                  pl.BlockSpec((1, d, d_exp), lambda i, be, nu: (be[i], 0, 0)),
                  pl.BlockSpec((1, d, d_exp), lambda i, be, nu: (be[i], 0, 0)),
                  pl.BlockSpec((1, d_exp, d), lambda i, be, nu: (be[i], 0, 0))],
        out_specs=pl.BlockSpec((rb * n_sub, LANES), lambda i, be, nu: (i, 0)),
    )
    return pl.pallas_call(
        kern,
        grid_spec=grid_spec,
        out_shape=jax.ShapeDtypeStruct((n_blk * rb * n_sub, LANES), F32),
        compiler_params=_cparams(1),
        name="routed_experts",
    )(blk_e, n_used, x_rows, w1, w3, w2)


def _combine_kernel(rows_ref, rows_next_ref, y_hbm, ew_ref, x1_ref,
                    ysh_ref, gt2_ref, gf_ref, o_ref, buf, sem, *, tm, n_sub, nt, n_steps):
    step = pl.program_id(0) * nt + pl.program_id(1)
    slot = lax.rem(step, 2)

    def gather(r_ref, to_slot):
        def issue(t, carry):
            for k in range(TOP_K):
                _start_tile_copy(y_hbm, r_ref[k, t], buf, to_slot, (k * tm + t) * n_sub, sem, n_sub)
            return carry
        lax.fori_loop(0, tm, issue, 0)

    @pl.when(step == 0)
    def _():
        gather(rows_ref, 0)

    @pl.when(step + 1 < n_steps)
    def _():
        gather(rows_next_ref, 1 - slot)

    _wait_slot(buf, sem, slot)
    ew = ew_ref[...]
    cols = []
    for s in range(n_sub):
        acc = None
        for k in range(TOP_K):
            term = ew[:, k:k + 1] * buf[slot, pl.ds(k * tm * n_sub + s, tm, stride=n_sub), :]
            acc = term if acc is None else acc + term
        cols.append(acc)
    y_routed = jnp.concatenate(cols, axis=1)
    x2 = x1_ref[0] + gt2_ref[0] * (y_routed + ysh_ref[0])
    o_ref[0] = _rms(x2, gf_ref[...])


def _combine_call(dst_rows, y_rows, ew, x1, ysh, gt2, g_final, tm):
    bx, tx, d = x1.shape
    nt = tx // tm
    n_sub = d // LANES
    n_steps = bx * nt
    kern = functools.partial(_combine_kernel, tm=tm, n_sub=n_sub, nt=nt, n_steps=n_steps)
    row = pl.BlockSpec((1, tm, d), lambda b, i: (b, i, 0))
    cur = pl.BlockSpec((TOP_K, tm), lambda b, i: (0, b * nt + i), memory_space=pltpu.SMEM)
    nxt = pl.BlockSpec((TOP_K, tm), lambda b, i: (0, jnp.minimum(b * nt + i + 1, n_steps - 1)),
                       memory_space=pltpu.SMEM)
    return pl.pallas_call(
        kern,
        grid=(bx, nt),
        in_specs=[cur, nxt,
                  pl.BlockSpec(memory_space=pl.ANY),
                  pl.BlockSpec((tm, TOP_K), lambda b, i: (b * nt + i, 0)),
                  row, row, _mod_spec(gt2, tm, d),
                  pl.BlockSpec((1, d), lambda b, i: (0, 0))],
        out_specs=row,
        out_shape=jax.ShapeDtypeStruct((bx, tx, d), F32),
        scratch_shapes=[pltpu.VMEM((2, TOP_K * tm * n_sub, LANES), F32),
                        pltpu.SemaphoreType.DMA((2,))],
        compiler_params=_cparams(2),
        name="combine_final",
    )(dst_rows, dst_rows, y_rows, ew, x1, ysh, gt2, g_final.reshape(1, d))


def _compress_sample_kernel(pt_ref, cache_hbm, w1k_ref, w1v_ref, fs_ref, raw, kbuf, vbuf, sem,
                            *, pp, page_rows, n_steps):
    step = pl.program_id(0) * pl.num_programs(1) + pl.program_id(1)
    slot = lax.rem(step, 2)

    def fetch(of_step, to_slot):
        def issue(p, carry):
            page = pt_ref[of_step * pp + p]
            pltpu.make_async_copy(cache_hbm.at[page, pl.ds(0, 2)], raw.at[to_slot, p],
                                  sem.at[to_slot]).start()
            return carry
        lax.fori_loop(0, pp, issue, 0)

    @pl.when(step == 0)
    def _():
        fetch(0, 0)

    @pl.when(step + 1 < n_steps)
    def _():
        fetch(step + 1, 1 - slot)

    _wait_slot(raw, sem, slot)
    per_page = page_rows // CMP_STRIDE
    n_chunks = pp * per_page
    out_row = lax.broadcasted_iota(I32, (page_rows, page_rows), 0)
    in_row = lax.broadcasted_iota(I32, (page_rows, page_rows), 1)
    regroup = jnp.where(in_row == lax.rem(out_row, per_page) * CMP_STRIDE + lax.div(out_row, per_page),
                        1.0, 0.0).astype(MXU_DTYPE)

    def to_rows(p, carry):
        for which, buf in ((0, kbuf), (1, vbuf)):
            rows = _mm_nt(regroup, raw[slot, p, which].reshape(LANES, page_rows))
            for l in range(CMP_STRIDE):
                dst = pl.multiple_of(l * n_chunks + p * per_page, per_page)
                buf[pl.ds(dst, per_page), :] = rows[l * per_page:(l + 1) * per_page]
        return carry

    lax.fori_loop(0, pp, to_rows, 0, unroll=4)
    rows_of = lambda buf: (lambda l: buf[pl.ds(l * n_chunks, n_chunks), :])
    fs_ref[0] = _cmp_first_layer(rows_of(kbuf), rows_of(vbuf), w1k_ref, w1v_ref)


def _compress_sample_call(page_table, cache5, w1k, w1v, pp):
    bs, n_pages = page_table.shape
    _, _, n_g, hd, page_rows = cache5.shape
    assert n_g * hd == LANES
    n_chunks = pp * page_rows // CMP_STRIDE
    spb = n_pages // pp
    kern = functools.partial(_compress_sample_kernel, pp=pp, page_rows=page_rows, n_steps=bs * spb)
    buf = pltpu.VMEM((pp * page_rows, LANES), F32)
    grid_spec = pltpu.PrefetchScalarGridSpec(
        num_scalar_prefetch=1,
        grid=(bs, spb),
        in_specs=[pl.BlockSpec(memory_space=pl.ANY),
                  pl.BlockSpec(w1k.shape, lambda b, i, pt: (0, 0)),
                  pl.BlockSpec(w1v.shape, lambda b, i, pt: (0, 0))],
        out_specs=pl.BlockSpec((1, n_chunks, 8 * CMP_HID), lambda b, i, pt: (b, i, 0)),
        scratch_shapes=[pltpu.VMEM((2, pp, 2, n_g, hd, page_rows), F32), buf, buf,
                        pltpu.SemaphoreType.DMA((2,))],
    )
    return pl.pallas_call(
        kern,
        grid_spec=grid_spec,
        out_shape=jax.ShapeDtypeStruct((bs, spb * n_chunks, 8 * CMP_HID), F32),
        compiler_params=_cparams(2),
        name="compress_sample",
    )(page_table.reshape(-1), cache5, w1k, w1v)


def _head_slopes(n_rows):
    head = lax.broadcasted_iota(I32, (n_rows, 1), 0)
    slopes = jnp.zeros((n_rows, 1), F32)
    for h in range(N_HEADS):
        slopes = jnp.where(head == h, _slope(h), slopes)
    return slopes


def _attn_sample_a_kernel(q_ref, fs_ref, pe_ref, w1pe_ref, w2bd_ref, ocmp_ref, idx_ref,
                          *, t_pos, n_blk, n_blk_pad):
    hd = HEAD_DIM
    f = fs_ref[0]
    n_c = f.shape[0]
    half = 4 * CMP_HID
    second_next = pltpu.roll(f[:, half:], n_c - 1, 0)
    kcvc = _cmp_second_layer(f[:, :half], second_next, pe_ref, w1pe_ref, w2bd_ref)
    q = q_ref[0]
    c_idx = lax.broadcasted_iota(I32, (1, n_c), 1)
    dist_c = t_pos - (c_idx * CMP_STRIDE + (CMP_BLOCK - 1))
    mask_c = dist_c >= 0
    slopes = _head_slopes(N_HEADS)
    head_grp = lax.div(lax.broadcasted_iota(I32, (N_HEADS, 1), 0), Q_PER_KV)
    per = SEL_BLOCK // CMP_STRIDE
    c_row = lax.broadcasted_iota(I32, (n_c, n_blk_pad), 0)
    lo = lax.broadcasted_iota(I32, (n_c, n_blk_pad), 1) * per
    spread = jnp.where((c_row >= lo) & (c_row < lo + per - 1), 1.0, 0.0) \
        + jnp.where((c_row == lo - 1) | (c_row == lo + per - 1), 0.5, 0.0)
    blk_lane = lax.broadcasted_iota(I32, (1, n_blk_pad), 1)
    cur = t_pos // SEL_BLOCK
    forced = (blk_lane == 0) | (blk_lane == cur) | (blk_lane == cur - 1)
    in_range = (blk_lane * SEL_BLOCK <= t_pos) & (blk_lane < n_blk)
    n_sq = (n_blk_pad, n_blk_pad)
    sub_id = lax.broadcasted_iota(I32, n_sq, 0)
    lane_id = lax.broadcasted_iota(I32, n_sq, 1)
    top_n = min(TOP_N, n_blk)
    o_cmp = jnp.zeros((N_HEADS, hd), F32)
    idx_rows = []
    for g in range(KV_HEADS):
        k_cmp = kcvc[:, g * hd:(g + 1) * hd]
        v_cmp = kcvc[:, (2 + g) * hd:(3 + g) * hd]
        s = _mm_nt(q, k_cmp) * ATTN_SCALE - slopes * dist_c.astype(F32)
        m = jnp.max(jnp.where(mask_c, s, NEG_INF), axis=1, keepdims=True)
        p = jnp.where(mask_c, jnp.exp(s - m), 0.0)
        l = jnp.sum(p, axis=1, keepdims=True)
        p = p / jnp.where(l > 0.0, l, 1.0)
        o_cmp = jnp.where(head_grp == g, _mm(p, v_cmp), o_cmp)
        p_sum = jnp.sum(jnp.where(head_grp == g, p, 0.0), axis=0, keepdims=True)
        imp = jnp.dot(jnp.broadcast_to(p_sum, (SUBLANES, n_c)), spread,
                      precision=lax.Precision.HIGHEST, preferred_element_type=F32)[0:1]
        imp = jnp.where(forced, FORCE_SCORE, imp)
        imp = jnp.where(in_range, imp, -jnp.inf)
        imp_col = jnp.transpose(jnp.broadcast_to(imp, (LANES, n_blk_pad)))[:, 0:1]
        beats = (imp > imp_col) | ((imp == imp_col) & (lane_id < sub_id))
        rank_col = jnp.sum(jnp.where(beats, 1.0, 0.0), axis=1, keepdims=True)
        sel_col = jnp.where(rank_col < float(top_n), 1.0, 0.0)
        before = jnp.dot(jnp.where(lane_id < sub_id, 1.0, 0.0).astype(BF16),
                         jnp.broadcast_to(sel_col, (n_blk_pad, LANES)).astype(BF16),
                         preferred_element_type=F32)
        slot_lane = lax.broadcasted_iota(I32, (n_blk_pad, LANES), 1).astype(F32)
        blk_sub = lax.broadcasted_iota(I32, (n_blk_pad, LANES), 0).astype(F32)
        hit = (sel_col > 0.5) & (before == slot_lane)
        idx_rows.append(jnp.sum(jnp.where(hit, blk_sub, 0.0), axis=0, keepdims=True))
    ocmp_ref[0] = o_cmp
    pad = jnp.zeros((SUBLANES - KV_HEADS, LANES), F32)
    idx_ref[0] = jnp.concatenate(idx_rows + [pad], axis=0).astype(I32)


def _attn_sample_a_call(q_heads, fs, cw, t_pos, n_blk):
    bs = q_heads.shape[0]
    n_blk_pad = -(-n_blk // LANES) * LANES
    kern = functools.partial(_attn_sample_a_kernel, t_pos=t_pos, n_blk=n_blk, n_blk_pad=n_blk_pad)
    full = lambda a: pl.BlockSpec(a.shape, lambda b: (0,) * a.ndim)
    per_b = lambda a: pl.BlockSpec((1,) + a.shape[1:], lambda b: (b,) + (0,) * (a.ndim - 1))
    return pl.pallas_call(
        kern,
        grid=(bs,),
        in_specs=[per_b(q_heads), per_b(fs), full(cw["pe"]), full(cw["w1pe"]), full(cw["w2bd"])],
        out_specs=[pl.BlockSpec((1, N_HEADS, HEAD_DIM), lambda b: (b, 0, 0)),
                   pl.BlockSpec((1, SUBLANES, LANES), lambda b: (b, 0, 0))],
        out_shape=[jax.ShapeDtypeStruct((bs, N_HEADS, HEAD_DIM), F32),
                   jax.ShapeDtypeStruct((bs, SUBLANES, LANES), I32)],
        compiler_params=_cparams(1),
        name="attn_sample_select",
    )(q_heads, fs, cw["pe"], cw["w1pe"], cw["w2bd"])


def _attn_sample_b_kernel(sel_ref, pt_ref, q_ref, cache_hbm, kvn_ref, win_ref, gate_ref, ocmp_ref,
                          o_ref, kbuf, vbuf, sem,
                          *, t_pos, top_n, n_past_blk, n_pages, per_page, n_steps):
    b = pl.program_id(0)
    slot = lax.rem(b, 2)
    hd = HEAD_DIM
    page_rows = kbuf.shape[-1]

    def fetch(of_b, to_slot):
        for g in range(KV_HEADS):
            def issue(i, carry, g=g):
                blk = jnp.minimum(sel_ref[(of_b * KV_HEADS + g) * top_n + i], n_past_blk - 1)
                page = pt_ref[of_b * n_pages + blk // per_page]
                pltpu.make_async_copy(cache_hbm.at[page, 2, g], kbuf.at[to_slot, g, i],
                                      sem.at[to_slot]).start()
                pltpu.make_async_copy(cache_hbm.at[page, 3, g], vbuf.at[to_slot, g, i],
                                      sem.at[to_slot]).start()
                return carry
            lax.fori_loop(0, top_n, issue, 0)

    @pl.when(b == 0)
    def _():
        fetch(0, 0)

    @pl.when(b + 1 < n_steps)
    def _():
        fetch(b + 1, 1 - slot)

    _wait_slot(kbuf, sem, slot)
    _wait_slot(vbuf, sem, slot)
    q = q_ref[0]
    qf = q.astype(F32)
    slopes = _head_slopes(N_HEADS)
    kvn = kvn_ref[0]
    rnd = lambda a: a.astype(MXU_DTYPE).astype(F32)
    new_col = lambda c: rnd(kvn[:, c * hd:(c + 1) * hd])
    gates = gate_ref[0]
    head_grp = lax.div(lax.broadcasted_iota(I32, (N_HEADS, 1), 0), Q_PER_KV)
    lane = lax.broadcasted_iota(I32, (1, page_rows), 1)
    out = jnp.zeros((N_HEADS, hd), F32)
    for g in range(KV_HEADS):
        scores = []
        for i in range(top_n):
            blk = sel_ref[(b * KV_HEADS + g) * top_n + i]
            k_pos = (blk // per_page) * page_rows + lane
            picked = (lax.div(lane, SEL_BLOCK) == lax.rem(blk, per_page)) & (blk < n_past_blk)
            s = _mm(q, kbuf[slot, g, i]) * ATTN_SCALE - slopes * (t_pos - k_pos).astype(F32)
            scores.append(jnp.where(picked, s, NEG_INF))
        s_all = jnp.concatenate(scores, axis=1)
        s_new = jnp.sum(qf * new_col(4 + g), axis=1, keepdims=True) * ATTN_SCALE
        m = jnp.maximum(jnp.max(s_all, axis=1, keepdims=True), s_new)
        p_all = jnp.exp(s_all - m)
        p_new = jnp.exp(s_new - m)
        acc = p_new * new_col(6 + g)
        for i in range(top_n):
            acc = acc + _mm_nt(p_all[:, i * page_rows:(i + 1) * page_rows], vbuf[slot, g, i])
        o_sel = acc / (jnp.sum(p_all, axis=1, keepdims=True) + p_new)
        w_rows = win_ref.shape[-1]
        dist_w = (w_rows - lax.broadcasted_iota(I32, (1, w_rows), 1)).astype(F32)
        s_w = _mm(q, win_ref[0, 0, g]) * ATTN_SCALE - slopes * dist_w
        s_w = jnp.where(dist_w < float(WINDOW), s_w, NEG_INF)
        sw_new = jnp.sum(qf * new_col(8 + g), axis=1, keepdims=True) * ATTN_SCALE
        m_w = jnp.maximum(jnp.max(s_w, axis=1, keepdims=True), sw_new)
        p_w = jnp.exp(s_w - m_w)
        pw_new = jnp.exp(sw_new - m_w)
        o_win = (_mm_nt(p_w, win_ref[0, 1, g]) + pw_new * new_col(10 + g)) \
            / (jnp.sum(p_w, axis=1, keepdims=True) + pw_new)
        o = gates[:, 0:1] * ocmp_ref[0] + gates[:, 1:2] * o_sel + gates[:, 2:3] * o_win
        out = jnp.where(head_grp == g, o, out)
    o_ref[0] = out.astype(o_ref.dtype)


def _attn_sample_b_call(sel_idx, page_table, q_heads, cache5, kv_new, win5, gates3, o_cmp,
                        t_pos, top_n):
    bs, n_pages = page_table.shape
    _, _, n_g, hd, page_rows = cache5.shape
    per_page = page_rows // SEL_BLOCK
    n_past_blk = n_pages * per_page
    kern = functools.partial(_attn_sample_b_kernel, t_pos=t_pos, top_n=top_n, n_past_blk=n_past_blk,
                             n_pages=n_pages, per_page=per_page, n_steps=bs)
    per_b = lambda a: pl.BlockSpec((1,) + a.shape[1:], lambda b, sel, pt: (b,) + (0,) * (a.ndim - 1))
    tiles = pltpu.VMEM((2, n_g, top_n, hd, page_rows), F32)
    grid_spec = pltpu.PrefetchScalarGridSpec(
        num_scalar_prefetch=2,
        grid=(bs,),
        in_specs=[per_b(q_heads), pl.BlockSpec(memory_space=pl.ANY),
                  per_b(kv_new), per_b(win5), per_b(gates3), per_b(o_cmp)],
        out_specs=pl.BlockSpec((1, N_HEADS, HEAD_DIM), lambda b, sel, pt: (b, 0, 0)),
        scratch_shapes=[tiles, tiles, pltpu.SemaphoreType.DMA((2,))],
    )
    return pl.pallas_call(
        kern,
        grid_spec=grid_spec,
        out_shape=jax.ShapeDtypeStruct((bs, N_HEADS, HEAD_DIM), MXU_DTYPE),
        compiler_params=_cparams(1),
        name="attn_sample_gather",
    )(sel_idx, page_table.reshape(-1), q_heads, cache5, kv_new, win5, gates3, o_cmp)


def _tiles(sp, n_prompt_tokens, n_pages):
    return {
        "in_proj_rows": min(sp, 1024),
        "attn_queries": min(sp, 2 * LANES),
        "attn_keys": 512 if sp % 512 == 0 else 256,
        "conv_rows": 128,
        "merge_rows": min(sp, 512),
        "compress_pages": min(n_pages, 64),
        "expert_rows": 512,
        "dispatch_tokens": min(n_prompt_tokens, 1024),
        "combine_tokens": min(sp, LANES),
    }


def _prep_weights(w_in, w_a, w_b, w_out, router_w, router_b, sh_w1, sh_w3, sh_w2, g_norm1, g_norm2):
    d = w_in.shape[0]
    n_q = N_HEADS * HEAD_DIM
    n_kv = 6 * KV_HEADS * HEAD_DIM
    n_gate = 3 * N_HEADS
    o_gate = n_q + n_kv
    o_glu = o_gate + n_gate
    o_mrg = o_glu + d
    c = lambda a: a.astype(MXU_DTYPE)
    w_cat = jnp.concatenate(
        [w_in[:, :o_gate], w_in[:, o_glu:o_mrg],
         jnp.pad(w_in[:, o_gate:o_glu], ((0, 0), (0, LANES - n_gate)))], axis=1)
    mw = {"g1": g_norm1.reshape(1, d), "wm": c(w_in[:, o_mrg:]), "wa": c(w_a), "wb": c(w_b),
          "wo": c(w_out), "g2": g_norm2.reshape(1, d), "rwt": c(router_w.T),
          "rb": router_b.reshape(-1, 1), "s13": c(jnp.concatenate([sh_w1, sh_w3], axis=1)),
          "s2": c(sh_w2)}
    return c(w_cat), mw


def kernel(x_prompt, x_sample, cache_kv, cache_win, state_conv, page_table, c_prompt, c_sample, w_ada, b_ada, g_norm1, w_in, cmp_pe, cmp_w1, cmp_w2, conv_w, conv_b, conv_ln_g, conv_ln_b, w_a, w_b, w_out, g_norm2, router_w, router_b, exp_w1, exp_w3, exp_w2, sh_w1, sh_w3, sh_w2, g_final):
    l = 0
    assert w_ada.shape[0] == 1 and x_sample.shape[1] == 1 and cache_win.shape[2] == WINDOW
    bp, sp, d = x_prompt.shape
    bs = x_sample.shape[0]
    page_rows = cache_kv.shape[2]
    n_pages = page_table.shape[1]
    past_len = n_pages * page_rows
    hd = HEAD_DIM
    c_conv = d // 2
    n_sub = d // LANES
    n_kv_new = 4 * KV_HEADS * hd
    tiles = _tiles(sp, bp * sp, n_pages)
    rb = tiles["expert_rows"]

    mod = _ada_call(jnp.concatenate([c_prompt, c_sample], axis=0), w_ada[l], b_ada[l])
    mods_p = [mod[:bp, None, k * d:(k + 1) * d] for k in range(6)]
    mods_s = [mod[None, bp:, k * d:(k + 1) * d] for k in range(6)]
    w_cat, mw = _prep_weights(w_in[l], w_a[l], w_b[l], w_out[l], router_w[l], router_b[l],
                              sh_w1[l], sh_w3[l], sh_w2[l], g_norm1[l], g_norm2[l])
    cw = _compress_weights(cmp_pe[l], cmp_w1[l], cmp_w2[l])

    sh1, sc1, gt1, sh2, sc2, gt2 = mods_p
    q, kvw, glu, gates = _inproj_call(x_prompt, sc1, sh1, g_norm1[l], w_cat, tiles["in_proj_rows"])
    kcvc = _compress_prompt_call(kvw, cw)
    o_attn = _attn_prompt_call(q, kvw, kcvc, gates, tiles["attn_queries"], tiles["attn_keys"])
    z = _conv_prompt_call(glu, conv_w[l], conv_b[l], conv_ln_g[l], conv_ln_b[l], tiles["conv_rows"])
    x1_p, h2_p, ysh_p, eid_p, ew_p, rank_p, cnt_p = _merge_call(
        x_prompt, o_attn, z, (sc1, sh1, gt1, sc2, sh2), mw, tiles["merge_rows"])

    sh1s, sc1s, gt1s, sh2s, sc2s, gt2s = mods_s
    xs = x_sample.reshape(1, bs, d)
    q_s, kvw_s, glu_s, gates_s = _inproj_call(xs, sc1s, sh1s, g_norm1[l], w_cat, bs)
    cache5 = jnp.transpose(cache_kv[l], (0, 2, 3, 4, 1))
    win5 = jnp.transpose(cache_win[l], (0, 2, 3, 4, 1))
    fs = _compress_sample_call(page_table, cache5, cw["w1k"], cw["w1v"], tiles["compress_pages"])
    q_heads = q_s.reshape(bs, N_HEADS, hd)
    n_blk = -(-(past_len + 1) // SEL_BLOCK)
    top_n = min(TOP_N, n_blk)
    o_cmp_s, sel = _attn_sample_a_call(q_heads, fs, cw, past_len, n_blk)
    o_s = _attn_sample_b_call(
        sel[:, :KV_HEADS, :top_n].reshape(-1), page_table, q_heads, cache5,
        kvw_s.reshape(bs, 1, -1), win5,
        gates_s[0, :, :3 * N_HEADS].reshape(bs, N_HEADS, 3), o_cmp_s, past_len, top_n)
    glu_new = glu_s.reshape(bs, 1, c_conv)
    z_s = _conv_sample_call(jnp.transpose(state_conv[l], (1, 0, 2)), glu_s[0],
                            conv_w[l], conv_b[l], conv_ln_g[l], conv_ln_b[l])
    x1_s, h2_s, ysh_s, eid_s, ew_s, rank_s, cnt_s = _merge_call(
        xs, o_s.reshape(1, bs, N_HEADS * hd), z_s.reshape(1, bs, c_conv),
        (sc1s, sh1s, gt1s, sc2s, sh2s), mw, bs)

    n_p = bp * sp
    cnt_p = cnt_p[:, 0].astype(I32)
    cnt_s = cnt_s[:, 0].astype(I32)
    counts = cnt_p + cnt_s
    rank_s = rank_s + cnt_p[eid_s]
    start, padded, blk_e, n_used, n_blk = _moe_plan(counts, (n_p + bs) * TOP_K, rb)
    x_rows, dst_p, dst_s = _dispatch_call(start, counts, padded, n_used, eid_p, rank_p, eid_s,
                                          rank_s, h2_p, h2_s, n_blk, rb, tiles["dispatch_tokens"])
    y_rows = _experts_call(x_rows, blk_e, n_used, exp_w1[l], exp_w3[l], exp_w2[l], rb)
    y_prompt = _combine_call(dst_p, y_rows, ew_p.T, x1_p, ysh_p, gt2, g_final,
                             tiles["combine_tokens"])
    y_sample = _combine_call(dst_s, y_rows, ew_s.T, x1_s, ysh_s, gt2s, g_final, bs)

    w_keep = min(WINDOW, sp)
    kv_prompt = kvw[:, :, :n_kv_new].reshape(1, bp, sp, 4, KV_HEADS, hd)
    kv_sample = kvw_s[0, :, :n_kv_new].reshape(1, bs, 1, 4, KV_HEADS, hd)
    win_prompt = kvw[:, sp - w_keep:, n_kv_new:].reshape(1, bp, w_keep, 2, KV_HEADS, hd)
    win_new = kvw_s[0, :, n_kv_new:].reshape(bs, 1, 2, KV_HEADS, hd)
    win_sample = jnp.concatenate([cache_win[l][:, 1:], win_new], axis=1)[None]
    conv_prompt = glu[:, sp - (CONV_W - 1):][None]
    conv_sample = jnp.concatenate([state_conv[l][:, 1:], glu_new], axis=1)[None]
    return (y_prompt, y_sample.reshape(bs, 1, d), kv_prompt, kv_sample, win_prompt, win_sample,
            conv_prompt, conv_sample)
```

```python
import functools

import jax
import jax.numpy as jnp
from jax import lax
from jax.experimental import pallas as pl
from jax.experimental.pallas import tpu as pltpu

F32 = jnp.float32
BF16 = jnp.bfloat16
I32 = jnp.int32
MXU_DTYPE = jnp.bfloat16

N_HEADS = 8
KV_HEADS = 2
Q_PER_KV = N_HEADS // KV_HEADS
HEAD_DIM = 64
CMP_BLOCK = 32
CMP_STRIDE = 16
CMP_HID = 2 * HEAD_DIM
SEL_BLOCK = 64
TOP_N = 16
WINDOW = 512
CONV_W = 31
N_EXPERTS = 256
N_GROUPS = 8
TOPK_GROUPS = 4
TOP_K = 8
ROUTED_SCALE = 2.5
EPS = 1e-6
NEG_INF = -1e30
FORCE_SCORE = 1e4
ATTN_SCALE = HEAD_DIM ** -0.5

LANES = 128
SUBLANES = 8
VMEM_LIMIT = 56 * 1024 * 1024


def _cparams(n_axes):
    return pltpu.CompilerParams(
        dimension_semantics=("arbitrary",) * n_axes, vmem_limit_bytes=VMEM_LIMIT)


def _mm(a, b):
    return jnp.dot(a.astype(MXU_DTYPE), b.astype(MXU_DTYPE), preferred_element_type=F32)


def _mm_nt(a, b):
    return lax.dot_general(a.astype(MXU_DTYPE), b.astype(MXU_DTYPE),
                           (((1,), (1,)), ((), ())), preferred_element_type=F32)


def _mm_tn(a, b):
    return lax.dot_general(a.astype(MXU_DTYPE), b.astype(MXU_DTYPE),
                           (((0,), (0,)), ((), ())), preferred_element_type=F32)


def _sigmoid(x):
    return 1.0 / (1.0 + jnp.exp(-x))


def _silu(x):
    return x * _sigmoid(x)


def _rms(x, g):
    return x * lax.rsqrt(jnp.mean(x * x, axis=-1, keepdims=True) + EPS) * g


def _slope(head):
    return 2.0 ** (-8.0 * (head + 1) / N_HEADS)


def _ada_kernel(c_ref, w_ref, b_ref, o_ref):
    o_ref[...] = _mm(_silu(c_ref[...]), w_ref[...]) + b_ref[...]


def _ada_call(c, w_ada, b_ada):
    n, d = c.shape
    n_out = w_ada.shape[1]
    tn = n_out // 6
    return pl.pallas_call(
        _ada_kernel,
        grid=(n_out // tn,),
        in_specs=[pl.BlockSpec((n, d), lambda j: (0, 0)),
                  pl.BlockSpec((d, tn), lambda j: (0, j)),
                  pl.BlockSpec((1, tn), lambda j: (0, j))],
        out_specs=pl.BlockSpec((n, tn), lambda j: (0, j)),
        out_shape=jax.ShapeDtypeStruct((n, n_out), F32),
        compiler_params=_cparams(1),
        name="ada_mod",
    )(c, w_ada, b_ada.reshape(1, n_out))


def _inproj_kernel(x_ref, sc_ref, sh_ref, g_ref, w_ref, q_ref, kv_ref, glu_ref, gate_ref,
                   *, n_q, n_kv, c_conv):
    x = x_ref[0]
    h = _rms(x, g_ref[...]) * (1.0 + sc_ref[0]) + sh_ref[0]
    y = _mm(h, w_ref[...])
    o = 0
    q_ref[0] = y[:, o:o + n_q].astype(q_ref.dtype)
    o += n_q
    kv_ref[0] = y[:, o:o + n_kv]
    o += n_kv
    u_a = y[:, o:o + c_conv]
    u_g = y[:, o + c_conv:o + 2 * c_conv]
    glu_ref[0] = u_a * _sigmoid(u_g)
    o += 2 * c_conv
    gate_ref[0] = _sigmoid(y[:, o:o + LANES])


def _mod_spec(mod, tm, d):
    if mod.shape[1] == 1:
        return pl.BlockSpec((1, 1, d), lambda b, i: (b, 0, 0))
    return pl.BlockSpec((1, tm, d), lambda b, i: (b, i, 0))


def _inproj_call(x, sc1, sh1, g1, w_cat, tm):
    bx, tx, d = x.shape
    n_q = N_HEADS * HEAD_DIM
    n_kv = 6 * KV_HEADS * HEAD_DIM
    c_conv = d // 2
    n_cat = w_cat.shape[1]
    kern = functools.partial(_inproj_kernel, n_q=n_q, n_kv=n_kv, c_conv=c_conv)
    row = lambda w: pl.BlockSpec((1, tm, w), lambda b, i: (b, i, 0))
    return pl.pallas_call(
        kern,
        grid=(bx, tx // tm),
        in_specs=[row(d), _mod_spec(sc1, tm, d), _mod_spec(sh1, tm, d),
                  pl.BlockSpec((1, d), lambda b, i: (0, 0)),
                  pl.BlockSpec((d, n_cat), lambda b, i: (0, 0))],
        out_specs=[row(n_q), row(n_kv), row(c_conv), row(LANES)],
        out_shape=[jax.ShapeDtypeStruct((bx, tx, n_q), MXU_DTYPE),
                   jax.ShapeDtypeStruct((bx, tx, n_kv), F32),
                   jax.ShapeDtypeStruct((bx, tx, c_conv), F32),
                   jax.ShapeDtypeStruct((bx, tx, LANES), F32)],
        compiler_params=_cparams(2),
        name="in_proj",
    )(x, sc1, sh1, g1.reshape(1, d), w_cat)


def _cmp_first_layer(k_of_l, v_of_l, w1k_ref, w1v_ref):
    half = KV_HEADS * CMP_HID
    f_k = _mm(jnp.concatenate([k_of_l(l) for l in range(CMP_STRIDE)], axis=1), w1k_ref[...])
    f_v = _mm(jnp.concatenate([v_of_l(l) for l in range(CMP_STRIDE)], axis=1), w1v_ref[...])
    return jnp.concatenate([f_k[:, :half], f_v[:, :half], f_k[:, half:], f_v[:, half:]], axis=1)


def _cmp_second_layer(first, second_next, pe_ref, w1pe_ref, w2bd_ref):
    pe_term = _mm(pe_ref[...], w1pe_ref[...])[0:1]
    return _mm(_silu(first + second_next + pe_term), w2bd_ref[...])


def _compress_prompt_kernel(k_ref, v_ref, w1k_ref, w1v_ref, pe_ref, w1pe_ref, w2bd_ref, o_ref, fs_ref,
                            *, n_chunks):
    half = 4 * CMP_HID
    rows_of = lambda ref: (lambda l: ref.at[0][pl.ds(l, n_chunks, stride=CMP_STRIDE), :])
    fs_ref[pl.ds(0, n_chunks), :] = _cmp_first_layer(rows_of(k_ref), rows_of(v_ref), w1k_ref, w1v_ref)
    fs_ref[pl.ds(n_chunks, SUBLANES), :] = jnp.zeros((SUBLANES, 2 * half), F32)
    first = fs_ref[pl.ds(0, n_chunks), pl.ds(0, half)]
    second_next = fs_ref[pl.ds(1, n_chunks), pl.ds(half, half)]
    o_ref[0] = _cmp_second_layer(first, second_next, pe_ref, w1pe_ref, w2bd_ref)


def _compress_prompt_call(kvw, cw):
    b, t, _ = kvw.shape
    n_chunks = t // CMP_STRIDE
    wid = 4 * HEAD_DIM
    kern = functools.partial(_compress_prompt_kernel, n_chunks=n_chunks)
    full = lambda a: pl.BlockSpec(a.shape, lambda i: (0,) * a.ndim)
    return pl.pallas_call(
        kern,
        grid=(b,),
        in_specs=[pl.BlockSpec((1, t, LANES), lambda i: (i, 0, 0)),
                  pl.BlockSpec((1, t, LANES), lambda i: (i, 0, 1)),
                  full(cw["w1k"]), full(cw["w1v"]), full(cw["pe"]), full(cw["w1pe"]), full(cw["w2bd"])],
        out_specs=pl.BlockSpec((1, n_chunks, wid), lambda i: (i, 0, 0)),
        out_shape=jax.ShapeDtypeStruct((b, n_chunks, wid), F32),
        scratch_shapes=[pltpu.VMEM((n_chunks + SUBLANES, 8 * CMP_HID), F32)],
        compiler_params=_cparams(1),
        name="compress_prompt",
    )(kvw, kvw, cw["w1k"], cw["w1v"], cw["pe"], cw["w1pe"], cw["w2bd"])


def _compress_weights(cmp_pe, cmp_w1, cmp_w2):
    hd, hid = HEAD_DIM, CMP_HID
    slab_kv = (0, 0, 1, 1)

    def first_layer(kv):
        w = jnp.zeros((CMP_STRIDE, KV_HEADS, hd, 2, KV_HEADS, hid), F32)
        for g in range(KV_HEADS):
            w = w.at[:, g, :, 0, g, :].set(cmp_w1[kv, :CMP_STRIDE])
            w = w.at[:, g, :, 1, g, :].set(cmp_w1[kv, CMP_STRIDE:])
        return w.reshape(CMP_STRIDE * KV_HEADS * hd, 2 * KV_HEADS * hid).astype(MXU_DTYPE)

    w2bd = jnp.zeros((4 * hid, 4 * hd), F32)
    for s, kv in enumerate(slab_kv):
        w2bd = w2bd.at[s * hid:(s + 1) * hid, s * hd:(s + 1) * hd].set(cmp_w2[kv])
    pe = jnp.broadcast_to(cmp_pe.reshape(1, -1), (SUBLANES, 2 * CMP_BLOCK * hd))
    w1pe = jnp.zeros((2 * CMP_BLOCK * hd, 4 * hid), F32)
    n_flat = CMP_BLOCK * hd
    for s, kv in enumerate(slab_kv):
        w1pe = w1pe.at[kv * n_flat:(kv + 1) * n_flat, s * hid:(s + 1) * hid].set(
            cmp_w1[kv].reshape(n_flat, hid))
    return {"w1k": first_layer(0), "w1v": first_layer(1), "w2bd": w2bd.astype(MXU_DTYPE),
            "pe": pe, "w1pe": w1pe.astype(MXU_DTYPE)}


def _softmax_cols(s, mask):
    m = jnp.max(jnp.where(mask, s, NEG_INF), axis=0, keepdims=True)
    p = jnp.where(mask, jnp.exp(s - m), 0.0)
    l = jnp.sum(p, axis=0, keepdims=True)
    return p / jnp.where(l > 0.0, l, 1.0)


def _top_n_rows(imp, top_n):
    n = imp.shape[0]
    row_id = lax.broadcasted_iota(I32, imp.shape, 0)
    beaten = jnp.zeros(imp.shape, F32)
    for m in range(n):
        other = imp[m:m + 1, :]
        wins = (other > imp) | ((other == imp) & (row_id > m))
        beaten = beaten + jnp.where(wins, 1.0, 0.0)
    return beaten < float(top_n)


SEL_FEAT0 = HEAD_DIM + SUBLANES
MASK_BIG = -2.0 ** 100


def _attn_prompt_kernel(q_ref, ksv_ref, kwv_ref, kc_ref, gate_ref, o_ref,
                        kaug, vsel, kwaug, vwin, psum_ref, *, tq, kc_len, t_len):
    qt = pl.program_id(1)
    q0 = qt * tq
    hd = HEAD_DIM
    n_cmp_rows = kc_ref.shape[1]
    n_blk = t_len // SEL_BLOCK
    top_n = min(TOP_N, n_blk)
    wk = min(WINDOW + tq, t_len)

    @pl.when(qt == 0)
    def _():
        lane = lax.broadcasted_iota(I32, (t_len, LANES), 1)
        pos = lax.broadcasted_iota(I32, (t_len, LANES), 0)
        pos_hi = lax.div(pos, SEL_BLOCK)
        alibi = jnp.where(lane < hd + 2, 1.0,
                          jnp.where(lane == hd + 2, pos_hi.astype(F32),
                                    jnp.where(lane == hd + 3, lax.rem(pos, SEL_BLOCK).astype(F32), 0.0)))
        in_blk = jnp.where(lane - SEL_FEAT0 == pos_hi, 1.0, 0.0)
        for g in range(KV_HEADS):
            k_s = ksv_ref[0, :, 0:LANES]
            k_w = kwv_ref[0, :, 0:LANES]
            if g:
                k_s = pltpu.roll(k_s, LANES - g * hd, 1)
                k_w = pltpu.roll(k_w, LANES - g * hd, 1)
            kaug[g] = jnp.where(lane < hd, k_s, alibi + in_blk).astype(kaug.dtype)
            kwaug[g] = jnp.where(lane < hd, k_w, alibi).astype(kwaug.dtype)
            v_s = ksv_ref[0, :, LANES:2 * LANES]
            v_w = kwv_ref[0, :, LANES:2 * LANES]
            if g:
                v_s = pltpu.roll(v_s, LANES - g * hd, 1)
                v_w = pltpu.roll(v_w, LANES - g * hd, 1)
            ones_lane = jnp.where(lane == hd, 1.0, 0.0)
            vsel[g] = jnp.where(lane < hd, v_s, ones_lane).astype(vsel.dtype)
            vwin[g] = jnp.where(lane < hd, v_w, ones_lane).astype(vwin.dtype)
        psum_ref[...] = jnp.zeros(psum_ref.shape, F32)

    q_blk = q_ref[0]
    gates = gate_ref[0]
    rows = Q_PER_KV * tq
    row_t = q0 + lax.rem(lax.broadcasted_iota(I32, (rows, 1), 0), tq)
    lane_q = lax.broadcasted_iota(I32, (tq, LANES), 1)
    t_q = q0 + lax.broadcasted_iota(I32, (tq, LANES), 0)
    t_hi = (lax.div(t_q, SEL_BLOCK) * SEL_BLOCK).astype(F32)
    t_lo = lax.rem(t_q, SEL_BLOCK).astype(F32)
    w0 = pl.multiple_of(jnp.maximum(q0 + tq - wk, 0), tq)
    dist_w = row_t - (w0 + lax.broadcasted_iota(I32, (rows, wk), 1))
    band = jnp.where((dist_w >= 0) & (dist_w < WINDOW), 0.0, NEG_INF)
    out_heads = []
    for g in range(KV_HEADS):
        q_heads = [q_blk[:, (g * Q_PER_KV + r) * hd:(g * Q_PER_KV + r + 1) * hd]
                   for r in range(Q_PER_KV)]
        k_cmp = kc_ref[0, :, g * hd:(g + 1) * hd]
        v_cmp = kc_ref[0, :, (2 + g) * hd:(3 + g) * hd]
        t_lane = q0 + lax.broadcasted_iota(I32, (n_cmp_rows, tq), 1)
        c_pos = lax.broadcasted_iota(I32, (n_cmp_rows, tq), 0) * CMP_STRIDE + (CMP_BLOCK - 1)
        dist_c = t_lane - c_pos
        mask_c = dist_c >= 0
        dist_cf = dist_c.astype(F32)
        o_cmp = []
        p_sum = None
        for r in range(Q_PER_KV):
            s = _mm_nt(k_cmp, q_heads[r]) * ATTN_SCALE - _slope(g * Q_PER_KV + r) * dist_cf
            p = _softmax_cols(s, mask_c)
            o_cmp.append(_mm_tn(p, v_cmp))
            p_sum = p if p_sum is None else p_sum + p
        per = SEL_BLOCK // CMP_STRIDE
        for h in range(tq // LANES):
            psum_ref[h, pl.ds(SUBLANES, n_cmp_rows), :] = p_sum[:, h * LANES:(h + 1) * LANES]
        taps = [jnp.concatenate(
            [psum_ref.at[h][pl.ds(SUBLANES - 1 + k, n_blk, stride=per), :] for h in range(tq // LANES)],
            axis=1) for k in range(per + 1)]
        imp = 0.5 * taps[0] + 0.5 * taps[per]
        for k in range(1, per):
            imp = imp + taps[k]
        blk = lax.broadcasted_iota(I32, (n_blk, tq), 0)
        t_blk = q0 + lax.broadcasted_iota(I32, (n_blk, tq), 1)
        cur = lax.div(t_blk, SEL_BLOCK)
        forced = (blk == 0) | (blk == cur) | (blk == cur - 1)
        imp = jnp.where(forced, FORCE_SCORE, imp)
        imp = jnp.where(blk * SEL_BLOCK <= t_blk, imp, -jnp.inf)
        sel_neg = jnp.where(_top_n_rows(imp, top_n), 0.0, MASK_BIG)
        sel_lanes = jnp.transpose(jnp.concatenate(
            [jnp.zeros((SEL_FEAT0, tq), F32), sel_neg,
             jnp.zeros((LANES - SEL_FEAT0 - n_blk, tq), F32)], axis=0))
        q_aug = []
        for r in range(Q_PER_KV):
            head = g * Q_PER_KV + r
            slope = _slope(head)
            pair = q_blk[:, (head // 2) * LANES:(head // 2 + 1) * LANES].astype(F32)
            if head % 2:
                pair = pltpu.roll(pair, LANES - hd, 1)
            feats = jnp.where(lane_q == hd, -slope * t_hi,
                              jnp.where(lane_q == hd + 1, -slope * t_lo,
                                        jnp.where(lane_q == hd + 2, slope * SEL_BLOCK,
                                                  jnp.where(lane_q == hd + 3, slope, sel_lanes))))
            q_aug.append(jnp.where(lane_q < hd, pair * ATTN_SCALE, feats).astype(MXU_DTYPE))
        q_aug = jnp.concatenate(q_aug, axis=0)

        def sel_chunk(j, carry, causal):
            m_run, acc = carry
            k0 = pl.multiple_of(j * kc_len, kc_len)
            s = _mm_nt(q_aug, kaug[g, pl.ds(k0, kc_len), :])
            if causal:
                k_pos = k0 + lax.broadcasted_iota(I32, (rows, kc_len), 1)
                s = jnp.where(k_pos <= row_t, s, NEG_INF)
            m_new = jnp.maximum(m_run, jnp.max(s, axis=1, keepdims=True))
            p = jnp.exp((s - m_new).astype(MXU_DTYPE))
            acc_new = jnp.exp(m_run - m_new) * acc + _mm(p, vsel[g, pl.ds(k0, kc_len), :])
            return m_new, acc_new

        n_full = lax.div(q0, kc_len)
        init = (jnp.full((rows, 1), NEG_INF, F32), jnp.zeros((rows, LANES), F32))
        carry = lax.fori_loop(0, n_full, functools.partial(sel_chunk, causal=False), init)
        _, acc_sel = sel_chunk(n_full, carry, causal=True)
        o_sel = acc_sel[:, :hd] / acc_sel[:, hd:hd + 1]
        s = _mm_nt(q_aug, kwaug[g, pl.ds(w0, wk), :]) + band
        p = jnp.exp((s - jnp.max(s, axis=1, keepdims=True)).astype(MXU_DTYPE))
        acc_win = _mm(p, vwin[g, pl.ds(w0, wk), :])
        o_win = acc_win[:, :hd] / acc_win[:, hd:hd + 1]
        for r in range(Q_PER_KV):
            c = (g * Q_PER_KV + r) * 3
            out_heads.append(gates[:, c:c + 1] * o_cmp[r]
                             + gates[:, c + 1:c + 2] * o_sel[r * tq:(r + 1) * tq]
                             + gates[:, c + 2:c + 3] * o_win[r * tq:(r + 1) * tq])
    o_ref[0] = jnp.concatenate(out_heads, axis=1).astype(o_ref.dtype)


def _attn_prompt_call(q, kvw, kcvc, gates, tq, kc_len):
    b, t, n_q = q.shape
    wid = 4 * HEAD_DIM
    kern = functools.partial(_attn_prompt_kernel, tq=tq, kc_len=kc_len, t_len=t)
    assert SEL_FEAT0 + t // SEL_BLOCK <= LANES and kc_len % tq == 0
    k_scratch = pltpu.VMEM((KV_HEADS, t, LANES), MXU_DTYPE)
    v_scratch = pltpu.VMEM((KV_HEADS, t, LANES), MXU_DTYPE)
    n_cmp_rows = kcvc.shape[1]
    return pl.pallas_call(
        kern,
        grid=(b, t // tq),
        in_specs=[pl.BlockSpec((1, tq, n_q), lambda i, j: (i, j, 0)),
                  pl.BlockSpec((1, t, wid), lambda i, j: (i, 0, 1)),
                  pl.BlockSpec((1, t, wid), lambda i, j: (i, 0, 2)),
                  pl.BlockSpec((1, n_cmp_rows, wid), lambda i, j: (i, 0, 0)),
                  pl.BlockSpec((1, tq, LANES), lambda i, j: (i, j, 0))],
        out_specs=pl.BlockSpec((1, tq, n_q), lambda i, j: (i, j, 0)),
        out_shape=jax.ShapeDtypeStruct((b, t, n_q), MXU_DTYPE),
        scratch_shapes=[k_scratch, v_scratch, k_scratch, v_scratch,
                        pltpu.VMEM((tq // LANES, n_cmp_rows + 2 * SUBLANES, LANES), F32)],
        compiler_params=_cparams(2),
        name="attn_prompt",
    )(q, kvw, kvw, kcvc, gates)


CONV_PAD = 32


def _ln_silu(z, lg, lb):
    mu = jnp.mean(z, axis=-1, keepdims=True)
    zc = z - mu
    var = jnp.mean(zc * zc, axis=-1, keepdims=True)
    return _silu(zc * lax.rsqrt(var + EPS) * lg + lb)


def _conv_prompt_kernel(glu_ref, cw_ref, cb_ref, lg_ref, lb_ref, z_ref, full_ref, *, t_len, tt):
    full_ref[pl.ds(0, CONV_PAD), :] = jnp.zeros((CONV_PAD, full_ref.shape[1]), F32)
    full_ref[pl.ds(CONV_PAD, t_len), :] = glu_ref[0]
    first = CONV_PAD - (CONV_W - 1)

    def tile(i, carry):
        r0 = pl.multiple_of(i * tt, tt)
        acc = jnp.zeros((tt, full_ref.shape[1]), F32) + cb_ref[...]
        win = full_ref[pl.ds(r0, tt + CONV_PAD), :]
        for phase in range(SUBLANES):
            offs = [o for o in range(first, first + CONV_W) if o % SUBLANES == phase]
            if not offs:
                continue
            shifted = win[phase:max(offs) + tt]
            for o in offs:
                acc = acc + shifted[o - phase:o - phase + tt] * cw_ref[o - first:o - first + 1, :]
        z_ref[0, pl.ds(r0, tt), :] = _ln_silu(acc, lg_ref[...], lb_ref[...]).astype(z_ref.dtype)
        return carry

    lax.fori_loop(0, t_len // tt, tile, 0)


def _conv_prompt_call(glu, conv_w, conv_b, ln_g, ln_b, tt):
    b, t, c = glu.shape
    kern = functools.partial(_conv_prompt_kernel, t_len=t, tt=tt)
    vec = pl.BlockSpec((1, c), lambda i: (0, 0))
    return pl.pallas_call(
        kern,
        grid=(b,),
        in_specs=[pl.BlockSpec((1, t, c), lambda i: (i, 0, 0)),
                  pl.BlockSpec((CONV_PAD, c), lambda i: (0, 0)), vec, vec, vec],
        out_specs=pl.BlockSpec((1, t, c), lambda i: (i, 0, 0)),
        out_shape=jax.ShapeDtypeStruct((b, t, c), MXU_DTYPE),
        scratch_shapes=[pltpu.VMEM((CONV_PAD + t, c), F32)],
        compiler_params=_cparams(1),
        name="conv_prompt",
    )(glu, jnp.pad(conv_w, ((0, CONV_PAD - CONV_W), (0, 0))), conv_b.reshape(1, c),
      ln_g.reshape(1, c), ln_b.reshape(1, c))


def _conv_sample_kernel(hist_ref, new_ref, cw_ref, cb_ref, lg_ref, lb_ref, z_ref):
    z = new_ref[...] * cw_ref[CONV_W - 1:CONV_W, :] + cb_ref[...]
    for w in range(CONV_W - 1):
        z = z + hist_ref[w] * cw_ref[w:w + 1, :]
    z_ref[...] = _ln_silu(z, lg_ref[...], lb_ref[...]).astype(z_ref.dtype)


def _conv_sample_call(hist, glu_new, conv_w, conv_b, ln_g, ln_b):
    bs, c = glu_new.shape
    return pl.pallas_call(
        _conv_sample_kernel,
        out_shape=jax.ShapeDtypeStruct((bs, c), MXU_DTYPE),
        name="conv_sample",
    )(hist, glu_new, jnp.pad(conv_w, ((0, CONV_PAD - CONV_W), (0, 0))), conv_b.reshape(1, c),
      ln_g.reshape(1, c), ln_b.reshape(1, c))


def _route_tile(h2, rwt_ref, rb_ref, cnt_ref):
    n_e = N_EXPERTS
    per_grp = n_e // N_GROUPS
    tm = h2.shape[0]
    aff = _sigmoid(_mm_nt(rwt_ref[...], h2))
    biased = aff + rb_ref[...]
    neg = -jnp.inf
    g_rows = []
    for g in range(N_GROUPS):
        v = biased[g * per_grp:(g + 1) * per_grp]
        m1 = jnp.max(v, axis=0, keepdims=True)
        is_m1 = v == m1
        n_m1 = jnp.sum(jnp.where(is_m1, 1.0, 0.0), axis=0, keepdims=True)
        m2 = jnp.max(jnp.where(is_m1, neg, v), axis=0, keepdims=True)
        g_rows.append(m1 + jnp.where(n_m1 >= 2.0, m1, m2))
    g_keep = _top_n_rows(jnp.concatenate(g_rows, axis=0), TOPK_GROUPS)
    cur = jnp.concatenate(
        [jnp.where(g_keep[g:g + 1], biased[g * per_grp:(g + 1) * per_grp], neg)
         for g in range(N_GROUPS)], axis=0)
    row_id = lax.broadcasted_iota(I32, (n_e, tm), 0).astype(F32)
    ids, wts, hots = [], [], []
    for _ in range(TOP_K):
        m = jnp.max(cur, axis=0, keepdims=True)
        idx = jnp.min(jnp.where(cur == m, row_id, float(n_e)), axis=0, keepdims=True)
        hot = row_id == idx
        ids.append(idx)
        wts.append(jnp.sum(jnp.where(hot, aff, 0.0), axis=0, keepdims=True))
        hots.append(hot)
        cur = jnp.where(hot, neg, cur)
    w_sum = wts[0]
    for w in wts[1:]:
        w_sum = w_sum + w
    wts = [w / w_sum * ROUTED_SCALE for w in wts]
    hot_all = jnp.where(hots[0], 1.0, 0.0)
    for hot in hots[1:]:
        hot_all = hot_all + jnp.where(hot, 1.0, 0.0)
    earlier = jnp.where(lax.broadcasted_iota(I32, (tm, tm), 0) < lax.broadcasted_iota(I32, (tm, tm), 1),
                        1.0, 0.0)
    before = cnt_ref[:, 0:1] + jnp.dot(hot_all.astype(BF16), earlier.astype(BF16),
                                       preferred_element_type=F32)
    ranks = [jnp.sum(jnp.where(hot, before, 0.0), axis=0, keepdims=True) for hot in hots]
    cnt_ref[...] = cnt_ref[...] + jnp.sum(hot_all, axis=1, keepdims=True)
    cat = lambda rows: jnp.concatenate(rows, axis=0)
    return cat(ids).astype(I32), cat(wts), cat(ranks).astype(I32)


def _merge_kernel(x_ref, oa_ref, z_ref, sc1_ref, sh1_ref, gt1_ref, sc2_ref, sh2_ref,
                  g1_ref, wm_ref, wa_ref, wb_ref, wo_ref, g2_ref, rwt_ref, rb_ref,
                  s13_ref, s2_ref,
                  x1_ref, h2_ref, ysh_ref, eid_ref, ew_ref, rk_ref, cnt_out_ref, cnt_ref, *, d_exp):
    first_step = (pl.program_id(0) == 0) & (pl.program_id(1) == 0)

    @pl.when(first_step)
    def _():
        cnt_ref[...] = jnp.zeros(cnt_ref.shape, F32)

    x = x_ref[0]
    d = x.shape[1]
    tm = x.shape[0]
    h = _rms(x, g1_ref[...]) * (1.0 + sc1_ref[0]) + sh1_ref[0]
    g_mrg = _sigmoid(_mm(h, wm_ref[...]))
    mixed = g_mrg[:, :d] * _mm(oa_ref[0], wa_ref[...]) + g_mrg[:, d:] * _mm(z_ref[0], wb_ref[...])
    x1 = x + gt1_ref[0] * _mm(mixed, wo_ref[...])
    x1_ref[0] = x1
    h2 = _rms(x1, g2_ref[...]) * (1.0 + sc2_ref[0]) + sh2_ref[0]
    for s in range(d // LANES):
        h2_ref[pl.ds(s, tm, stride=d // LANES), :] = h2[:, s * LANES:(s + 1) * LANES]
    hs = _mm(h2, s13_ref[...])
    ysh_ref[0] = _mm(_silu(hs[:, :d_exp]) * hs[:, d_exp:], s2_ref[...])
    ids, wts, ranks = _route_tile(h2, rwt_ref, rb_ref, cnt_ref)
    eid_ref[...] = ids
    ew_ref[...] = wts
    rk_ref[...] = ranks
    cnt_out_ref[...] = cnt_ref[...]


def _merge_call(x, o_attn, z, mods, mw, tm):
    bx, tx, d = x.shape
    n = bx * tx
    nt = tx // tm
    d_exp = mw["s2"].shape[0]
    kern = functools.partial(_merge_kernel, d_exp=d_exp)
    row = lambda w: pl.BlockSpec((1, tm, w), lambda b, i: (b, i, 0))
    full = lambda a: pl.BlockSpec(a.shape, lambda b, i: (0,) * a.ndim)
    tok = pl.BlockSpec((TOP_K, tm), lambda b, i: (0, b * nt + i))
    wnames = ("g1", "wm", "wa", "wb", "wo", "g2", "rwt", "rb", "s13", "s2")
    return pl.pallas_call(
        kern,
        grid=(bx, nt),
        in_specs=[row(d), row(o_attn.shape[2]), row(z.shape[2])]
                 + [_mod_spec(m, tm, d) for m in mods]
                 + [full(mw[k]) for k in wnames],
        out_specs=[row(d),
                   pl.BlockSpec((tm * (d // LANES), LANES), lambda b, i: (b * nt + i, 0)),
                   row(d), tok, tok, tok,
                   pl.BlockSpec((N_EXPERTS, LANES), lambda b, i: (0, 0))],
        out_shape=[jax.ShapeDtypeStruct((bx, tx, d), F32),
                   jax.ShapeDtypeStruct((n * (d // LANES), LANES), F32),
                   jax.ShapeDtypeStruct((bx, tx, d), F32),
                   jax.ShapeDtypeStruct((TOP_K, n), I32),
                   jax.ShapeDtypeStruct((TOP_K, n), F32),
                   jax.ShapeDtypeStruct((TOP_K, n), I32),
                   jax.ShapeDtypeStruct((N_EXPERTS, LANES), F32)],
        scratch_shapes=[pltpu.VMEM((N_EXPERTS, LANES), F32)],
        compiler_params=_cparams(2),
        name="merge_route",
    )(x, o_attn, z, *mods, *[mw[k] for k in wnames])


def _start_tile_copy(src_hbm, src_row, buf, slot, dst_row, sem, n_sub, priority):
    pltpu.make_async_copy(src_hbm.at[pl.ds(pl.multiple_of(src_row, n_sub), n_sub), :],
                          buf.at[slot, pl.ds(pl.multiple_of(dst_row, n_sub), n_sub), :],
                          sem.at[slot]).start(priority=priority)


def _wait_slot(buf, sem, slot):
    pltpu.make_async_copy(buf.at[slot], buf.at[slot], sem.at[slot]).wait()


def _moe_plan(counts, n_asg, rb):
    n_blk = -(-(n_asg + N_EXPERTS * (rb - 1)) // rb)
    padded = (counts + rb - 1) // rb * rb
    pad_end = jnp.cumsum(padded)
    start = pad_end - padded
    blk_row = jnp.arange(n_blk, dtype=I32)[:, None] * rb
    blk_e = jnp.minimum(jnp.sum((pad_end[None, :] <= blk_row).astype(I32), axis=1), N_EXPERTS - 1)
    n_used = (pad_end[-1] // rb).astype(I32).reshape(1)
    return start.astype(I32), padded.astype(I32), blk_e.astype(I32), n_used, n_blk


def _dispatch_kernel(start_ref, cnt_ref, pad_ref, n_used_ref,
                     eid_ref, rk_ref, eid_s_ref, rk_s_ref, h2p_ref, h2s_ref,
                     xs_hbm, dst_ref, dst_s_ref, zeros, sem,
                     *, tmd, n_p_steps, bs, n_sub, rb, n_blk):
    i = pl.program_id(0)

    def scatter_tokens(src_ref, e_ref, r_ref, d_ref, n_tok):
        def issue(t, carry):
            src = src_ref.at[pl.ds(pl.multiple_of(t * n_sub, n_sub), n_sub), :]
            for k in range(TOP_K):
                dst_row = (start_ref[e_ref[k, t]] + r_ref[k, t]) * n_sub
                d_ref[k, t] = dst_row
                pltpu.make_async_copy(
                    src, xs_hbm.at[pl.ds(pl.multiple_of(dst_row, n_sub), n_sub), :],
                    sem).start(priority=k % 2)
            return carry
        lax.fori_loop(0, n_tok, issue, 0)
        done = xs_hbm.at[pl.ds(0, n_tok * TOP_K * n_sub), :]
        pltpu.make_async_copy(done, done, sem).wait()

    @pl.when(i < n_p_steps)
    def _():
        scatter_tokens(h2p_ref, eid_ref, rk_ref, dst_ref, tmd)

    @pl.when(i == n_p_steps)
    def _():
        scatter_tokens(h2s_ref, eid_s_ref, rk_s_ref, dst_s_ref, bs)
        zeros[...] = jnp.zeros(zeros.shape, F32)

        def zero_rows(first_row, n_rows):
            pltpu.make_async_copy(
                zeros.at[pl.ds(0, n_rows * n_sub), :],
                xs_hbm.at[pl.ds(pl.multiple_of(first_row * n_sub, n_sub), n_rows * n_sub), :],
                sem).start()

        def pad_expert(e, carry):
            n_pad = pad_ref[e] - cnt_ref[e]
            row = start_ref[e] + cnt_ref[e]
            piece = rb // 2
            while piece >= 1:
                take = (n_pad & piece) != 0

                @pl.when(take)
                def _(row=row, piece=piece):
                    zero_rows(row, piece)

                row = row + jnp.where(take, piece, 0)
                piece //= 2
            return carry

        lax.fori_loop(0, N_EXPERTS, pad_expert, 0)

        def pad_block(blk, carry):
            zero_rows(blk * rb, rb)
            return carry

        lax.fori_loop(n_used_ref[0], n_blk, pad_block, 0)
        n_zero = n_blk * rb - (n_p_steps * tmd + bs) * TOP_K
        done = xs_hbm.at[pl.ds(0, n_zero * n_sub), :]
        pltpu.make_async_copy(done, done, sem).wait()


def _dispatch_call(start, counts, padded, n_used, eid_p, rank_p, eid_s, rank_s, h2_p, h2_s,
                   n_blk, rb, tmd):
    n_p = eid_p.shape[1]
    bs = eid_s.shape[1]
    n_sub = h2_p.shape[0] // n_p
    n_p_steps = n_p // tmd
    kern = functools.partial(_dispatch_kernel, tmd=tmd, n_p_steps=n_p_steps, bs=bs, n_sub=n_sub,
                             rb=rb, n_blk=n_blk)
    last = n_p_steps - 1
    tile = pl.BlockSpec((TOP_K, tmd), lambda i, *_: (0, jnp.minimum(i, last)), memory_space=pltpu.SMEM)
    whole = pl.BlockSpec((TOP_K, bs), lambda i, *_: (0, 0), memory_space=pltpu.SMEM)
    grid_spec = pltpu.PrefetchScalarGridSpec(
        num_scalar_prefetch=4,
        grid=(n_p_steps + 1,),
        in_specs=[tile, tile, whole, whole,
                  pl.BlockSpec((tmd * n_sub, LANES), lambda i, *_: (jnp.minimum(i, last), 0)),
                  pl.BlockSpec((bs * n_sub, LANES), lambda i, *_: (0, 0))],
        out_specs=[pl.BlockSpec(memory_space=pl.ANY), tile, whole],
        scratch_shapes=[pltpu.VMEM((rb * n_sub, LANES), F32), pltpu.SemaphoreType.DMA],
    )
    return pl.pallas_call(
        kern,
        grid_spec=grid_spec,
        out_shape=[jax.ShapeDtypeStruct((n_blk * rb * n_sub, LANES), F32),
                   jax.ShapeDtypeStruct((TOP_K, n_p), I32),
                   jax.ShapeDtypeStruct((TOP_K, bs), I32)],
        compiler_params=_cparams(1),
        name="moe_dispatch",
    )(start, counts, padded, n_used, eid_p, rank_p, eid_s, rank_s, h2_p, h2_s)


def _experts_kernel(blk_e_ref, n_used_ref, x_ref, w1_ref, w3_ref, w2_ref, y_ref, *, rb, n_sub):
    i = pl.program_id(0)

    @pl.when(i < n_used_ref[0])
    def _():
        x = jnp.concatenate([x_ref[pl.ds(s, rb, stride=n_sub), :] for s in range(n_sub)], axis=1)
        hid = _silu(_mm(x, w1_ref[0])) * _mm(x, w3_ref[0])
        y = _mm(hid, w2_ref[0])
        for s in range(n_sub):
            y_ref[pl.ds(s, rb, stride=n_sub), :] = y[:, s * LANES:(s + 1) * LANES]

    @pl.when(i >= n_used_ref[0])
    def _():
        y_ref[...] = jnp.zeros(y_ref.shape, F32)


def _experts_call(x_rows, blk_e, n_used, w1, w3, w2, rb):
    _, d, d_exp = w1.shape
    n_sub = d // LANES
    n_blk = x_rows.shape[0] // (rb * n_sub)
    kern = functools.partial(_experts_kernel, rb=rb, n_sub=n_sub)
    grid_spec = pltpu.PrefetchScalarGridSpec(
        num_scalar_prefetch=2,
        grid=(n_blk,),
        in_specs=[pl.BlockSpec((rb * n_sub, LANES), lambda i, be, nu: (jnp.minimum(i, nu[0] - 1), 0)),
                  pl.BlockSpec((1, d, d_exp), lambda i, be, nu: (be[i], 0, 0)),
                  pl.BlockSpec((1, d, d_exp), lambda i, be, nu: (be[i], 0, 0)),
                  pl.BlockSpec((1, d_exp, d), lambda i, be, nu: (be[i], 0, 0))],
        out_specs=pl.BlockSpec((rb * n_sub, LANES), lambda i, be, nu: (i, 0)),
    )
    return pl.pallas_call(
        kern,
        grid_spec=grid_spec,
        out_shape=jax.ShapeDtypeStruct((n_blk * rb * n_sub, LANES), F32),
        compiler_params=_cparams(1),
        name="routed_experts",
    )(blk_e, n_used, x_rows, w1, w3, w2)


def _combine_kernel(rows_ref, rows_next_ref, y_hbm, ew_ref, x1_ref,
                    ysh_ref, gt2_ref, gf_ref, o_ref, buf, sem, *, tm, n_sub, nt, n_steps):
    step = pl.program_id(0) * nt + pl.program_id(1)
    slot = lax.rem(step, 2)

    def gather(r_ref, to_slot):
        def issue(t, carry):
            for k in range(TOP_K):
                _start_tile_copy(y_hbm, r_ref[k, t], buf, to_slot, (k * tm + t) * n_sub, sem, n_sub,
                                 k % 2)
            return carry
        lax.fori_loop(0, tm, issue, 0)

    @pl.when(step == 0)
    def _():
        gather(rows_ref, 0)

    @pl.when(step + 1 < n_steps)
    def _():
        gather(rows_next_ref, 1 - slot)

    _wait_slot(buf, sem, slot)
    ew = ew_ref[...]
    cols = []
    for s in range(n_sub):
        acc = None
        for k in range(TOP_K):
            term = ew[:, k:k + 1] * buf[slot, pl.ds(k * tm * n_sub + s, tm, stride=n_sub), :]
            acc = term if acc is None else acc + term
        cols.append(acc)
    y_routed = jnp.concatenate(cols, axis=1)
    x2 = x1_ref[0] + gt2_ref[0] * (y_routed + ysh_ref[0])
    o_ref[0] = _rms(x2, gf_ref[...])


def _combine_call(dst_rows, y_rows, ew, x1, ysh, gt2, g_final, tm):
    bx, tx, d = x1.shape
    nt = tx // tm
    n_sub = d // LANES
    n_steps = bx * nt
    kern = functools.partial(_combine_kernel, tm=tm, n_sub=n_sub, nt=nt, n_steps=n_steps)
    row = pl.BlockSpec((1, tm, d), lambda b, i: (b, i, 0))
    cur = pl.BlockSpec((TOP_K, tm), lambda b, i: (0, b * nt + i), memory_space=pltpu.SMEM)
    nxt = pl.BlockSpec((TOP_K, tm), lambda b, i: (0, jnp.minimum(b * nt + i + 1, n_steps - 1)),
                       memory_space=pltpu.SMEM)
    return pl.pallas_call(
        kern,
        grid=(bx, nt),
        in_specs=[cur, nxt,
                  pl.BlockSpec(memory_space=pl.ANY),
                  pl.BlockSpec((tm, TOP_K), lambda b, i: (b * nt + i, 0)),
                  row, row, _mod_spec(gt2, tm, d),
                  pl.BlockSpec((1, d), lambda b, i: (0, 0))],
        out_specs=row,
        out_shape=jax.ShapeDtypeStruct((bx, tx, d), F32),
        scratch_shapes=[pltpu.VMEM((2, TOP_K * tm * n_sub, LANES), F32),
                        pltpu.SemaphoreType.DMA((2,))],
        compiler_params=_cparams(2),
        name="combine_final",
    )(dst_rows, dst_rows, y_rows, ew, x1, ysh, gt2, g_final.reshape(1, d))


def _compress_sample_kernel(pt_ref, cache_hbm, w1k_ref, w1v_ref, fs_ref, raw, kbuf, vbuf, sem,
                            *, pp, page_rows, n_steps):
    step = pl.program_id(0) * pl.num_programs(1) + pl.program_id(1)
    slot = lax.rem(step, 2)

    def fetch(of_step, to_slot):
        def issue(p, carry):
            page = pt_ref[of_step * pp + p]
            pltpu.make_async_copy(cache_hbm.at[page, pl.ds(0, 2)], raw.at[to_slot, p],
                                  sem.at[to_slot]).start()
            return carry
        lax.fori_loop(0, pp, issue, 0)

    @pl.when(step == 0)
    def _():
        fetch(0, 0)

    @pl.when(step + 1 < n_steps)
    def _():
        fetch(step + 1, 1 - slot)

    _wait_slot(raw, sem, slot)
    per_page = page_rows // CMP_STRIDE
    n_chunks = pp * per_page
    out_row = lax.broadcasted_iota(I32, (page_rows, page_rows), 0)
    in_row = lax.broadcasted_iota(I32, (page_rows, page_rows), 1)
    regroup = jnp.where(in_row == lax.rem(out_row, per_page) * CMP_STRIDE + lax.div(out_row, per_page),
                        1.0, 0.0).astype(MXU_DTYPE)

    def to_rows(p, carry):
        for which, buf in ((0, kbuf), (1, vbuf)):
            rows = _mm_nt(regroup, raw[slot, p, which].reshape(LANES, page_rows))
            for l in range(CMP_STRIDE):
                dst = pl.multiple_of(l * n_chunks + p * per_page, per_page)
                buf[pl.ds(dst, per_page), :] = rows[l * per_page:(l + 1) * per_page]
        return carry

    lax.fori_loop(0, pp, to_rows, 0, unroll=4)
    rows_of = lambda buf: (lambda l: buf[pl.ds(l * n_chunks, n_chunks), :])
    fs_ref[0] = _cmp_first_layer(rows_of(kbuf), rows_of(vbuf), w1k_ref, w1v_ref)


def _compress_sample_call(page_table, cache5, w1k, w1v, pp):
    bs, n_pages = page_table.shape
    _, _, n_g, hd, page_rows = cache5.shape
    assert n_g * hd == LANES
    n_chunks = pp * page_rows // CMP_STRIDE
    spb = n_pages // pp
    kern = functools.partial(_compress_sample_kernel, pp=pp, page_rows=page_rows, n_steps=bs * spb)
    buf = pltpu.VMEM((pp * page_rows, LANES), F32)
    grid_spec = pltpu.PrefetchScalarGridSpec(
        num_scalar_prefetch=1,
        grid=(bs, spb),
        in_specs=[pl.BlockSpec(memory_space=pl.ANY),
                  pl.BlockSpec(w1k.shape, lambda b, i, pt: (0, 0)),
                  pl.BlockSpec(w1v.shape, lambda b, i, pt: (0, 0))],
        out_specs=pl.BlockSpec((1, n_chunks, 8 * CMP_HID), lambda b, i, pt: (b, i, 0)),
        scratch_shapes=[pltpu.VMEM((2, pp, 2, n_g, hd, page_rows), F32), buf, buf,
                        pltpu.SemaphoreType.DMA((2,))],
    )
    return pl.pallas_call(
        kern,
        grid_spec=grid_spec,
        out_shape=jax.ShapeDtypeStruct((bs, spb * n_chunks, 8 * CMP_HID), F32),
        compiler_params=_cparams(2),
        name="compress_sample",
    )(page_table.reshape(-1), cache5, w1k, w1v)


def _head_slopes(n_rows):
    head = lax.broadcasted_iota(I32, (n_rows, 1), 0)
    slopes = jnp.zeros((n_rows, 1), F32)
    for h in range(N_HEADS):
        slopes = jnp.where(head == h, _slope(h), slopes)
    return slopes


def _attn_sample_a_kernel(q_ref, fs_ref, pe_ref, w1pe_ref, w2bd_ref, ocmp_ref, idx_ref,
                          *, t_pos, n_blk, n_blk_pad):
    hd = HEAD_DIM
    f = fs_ref[0]
    n_c = f.shape[0]
    half = 4 * CMP_HID
    second_next = pltpu.roll(f[:, half:], n_c - 1, 0)
    kcvc = _cmp_second_layer(f[:, :half], second_next, pe_ref, w1pe_ref, w2bd_ref)
    q = q_ref[0]
    c_idx = lax.broadcasted_iota(I32, (1, n_c), 1)
    dist_c = t_pos - (c_idx * CMP_STRIDE + (CMP_BLOCK - 1))
    mask_c = dist_c >= 0
    slopes = _head_slopes(N_HEADS)
    head_grp = lax.div(lax.broadcasted_iota(I32, (N_HEADS, 1), 0), Q_PER_KV)
    per = SEL_BLOCK // CMP_STRIDE
    c_row = lax.broadcasted_iota(I32, (n_c, n_blk_pad), 0)
    lo = lax.broadcasted_iota(I32, (n_c, n_blk_pad), 1) * per
    spread = jnp.where((c_row >= lo) & (c_row < lo + per - 1), 1.0, 0.0) \
        + jnp.where((c_row == lo - 1) | (c_row == lo + per - 1), 0.5, 0.0)
    blk_lane = lax.broadcasted_iota(I32, (1, n_blk_pad), 1)
    cur = t_pos // SEL_BLOCK
    forced = (blk_lane == 0) | (blk_lane == cur) | (blk_lane == cur - 1)
    in_range = (blk_lane * SEL_BLOCK <= t_pos) & (blk_lane < n_blk)
    n_sq = (n_blk_pad, n_blk_pad)
    sub_id = lax.broadcasted_iota(I32, n_sq, 0)
    lane_id = lax.broadcasted_iota(I32, n_sq, 1)
    top_n = min(TOP_N, n_blk)
    o_cmp = jnp.zeros((N_HEADS, hd), F32)
    idx_rows = []
    for g in range(KV_HEADS):
        k_cmp = kcvc[:, g * hd:(g + 1) * hd]
        v_cmp = kcvc[:, (2 + g) * hd:(3 + g) * hd]
        s = _mm_nt(q, k_cmp) * ATTN_SCALE - slopes * dist_c.astype(F32)
        m = jnp.max(jnp.where(mask_c, s, NEG_INF), axis=1, keepdims=True)
        p = jnp.where(mask_c, jnp.exp(s - m), 0.0)
        l = jnp.sum(p, axis=1, keepdims=True)
        p = p / jnp.where(l > 0.0, l, 1.0)
        o_cmp = jnp.where(head_grp == g, _mm(p, v_cmp), o_cmp)
        p_sum = jnp.sum(jnp.where(head_grp == g, p, 0.0), axis=0, keepdims=True)
        imp = jnp.dot(jnp.broadcast_to(p_sum, (SUBLANES, n_c)), spread,
                      precision=lax.Precision.HIGHEST, preferred_element_type=F32)[0:1]
        imp = jnp.where(forced, FORCE_SCORE, imp)
        imp = jnp.where(in_range, imp, -jnp.inf)
        imp_col = jnp.transpose(jnp.broadcast_to(imp, (LANES, n_blk_pad)))[:, 0:1]
        beats = (imp > imp_col) | ((imp == imp_col) & (lane_id < sub_id))
        rank_col = jnp.sum(jnp.where(beats, 1.0, 0.0), axis=1, keepdims=True)
        sel_col = jnp.where(rank_col < float(top_n), 1.0, 0.0)
        before = jnp.dot(jnp.where(lane_id < sub_id, 1.0, 0.0).astype(BF16),
                         jnp.broadcast_to(sel_col, (n_blk_pad, LANES)).astype(BF16),
                         preferred_element_type=F32)
        slot_lane = lax.broadcasted_iota(I32, (n_blk_pad, LANES), 1).astype(F32)
        blk_sub = lax.broadcasted_iota(I32, (n_blk_pad, LANES), 0).astype(F32)
        hit = (sel_col > 0.5) & (before == slot_lane)
        idx_rows.append(jnp.sum(jnp.where(hit, blk_sub, 0.0), axis=0, keepdims=True))
    ocmp_ref[0] = o_cmp
    pad = jnp.zeros((SUBLANES - KV_HEADS, LANES), F32)
    idx_ref[0] = jnp.concatenate(idx_rows + [pad], axis=0).astype(I32)


def _attn_sample_a_call(q_heads, fs, cw, t_pos, n_blk):
    bs = q_heads.shape[0]
    n_blk_pad = -(-n_blk // LANES) * LANES
    kern = functools.partial(_attn_sample_a_kernel, t_pos=t_pos, n_blk=n_blk, n_blk_pad=n_blk_pad)
    full = lambda a: pl.BlockSpec(a.shape, lambda b: (0,) * a.ndim)
    per_b = lambda a: pl.BlockSpec((1,) + a.shape[1:], lambda b: (b,) + (0,) * (a.ndim - 1))
    return pl.pallas_call(
        kern,
        grid=(bs,),
        in_specs=[per_b(q_heads), per_b(fs), full(cw["pe"]), full(cw["w1pe"]), full(cw["w2bd"])],
        out_specs=[pl.BlockSpec((1, N_HEADS, HEAD_DIM), lambda b: (b, 0, 0)),
                   pl.BlockSpec((1, SUBLANES, LANES), lambda b: (b, 0, 0))],
        out_shape=[jax.ShapeDtypeStruct((bs, N_HEADS, HEAD_DIM), F32),
                   jax.ShapeDtypeStruct((bs, SUBLANES, LANES), I32)],
        compiler_params=_cparams(1),
        name="attn_sample_select",
    )(q_heads, fs, cw["pe"], cw["w1pe"], cw["w2bd"])


def _attn_sample_b_kernel(sel_ref, pt_ref, q_ref, cache_hbm, kvn_ref, win_ref, gate_ref, ocmp_ref,
                          o_ref, kbuf, vbuf, sem,
                          *, t_pos, top_n, n_past_blk, n_pages, per_page, n_steps):
    b = pl.program_id(0)
    slot = lax.rem(b, 2)
    hd = HEAD_DIM
    page_rows = kbuf.shape[-1]

    def fetch(of_b, to_slot):
        for g in range(KV_HEADS):
            def issue(i, carry, g=g):
                blk = jnp.minimum(sel_ref[(of_b * KV_HEADS + g) * top_n + i], n_past_blk - 1)
                page = pt_ref[of_b * n_pages + blk // per_page]
                pltpu.make_async_copy(cache_hbm.at[page, 2, g], kbuf.at[to_slot, g, i],
                                      sem.at[to_slot]).start()
                pltpu.make_async_copy(cache_hbm.at[page, 3, g], vbuf.at[to_slot, g, i],
                                      sem.at[to_slot]).start()
                return carry
            lax.fori_loop(0, top_n, issue, 0)

    @pl.when(b == 0)
    def _():
        fetch(0, 0)

    @pl.when(b + 1 < n_steps)
    def _():
        fetch(b + 1, 1 - slot)

    _wait_slot(kbuf, sem, slot)
    _wait_slot(vbuf, sem, slot)
    q = q_ref[0]
    qf = q.astype(F32)
    slopes = _head_slopes(N_HEADS)
    kvn = kvn_ref[0]
    rnd = lambda a: a.astype(MXU_DTYPE).astype(F32)
    new_col = lambda c: rnd(kvn[:, c * hd:(c + 1) * hd])
    gates = gate_ref[0]
    head_grp = lax.div(lax.broadcasted_iota(I32, (N_HEADS, 1), 0), Q_PER_KV)
    lane = lax.broadcasted_iota(I32, (1, page_rows), 1)
    out = jnp.zeros((N_HEADS, hd), F32)
    for g in range(KV_HEADS):
        scores = []
        for i in range(top_n):
            blk = sel_ref[(b * KV_HEADS + g) * top_n + i]
            k_pos = (blk // per_page) * page_rows + lane
            picked = (lax.div(lane, SEL_BLOCK) == lax.rem(blk, per_page)) & (blk < n_past_blk)
            s = _mm(q, kbuf[slot, g, i]) * ATTN_SCALE - slopes * (t_pos - k_pos).astype(F32)
            scores.append(jnp.where(picked, s, NEG_INF))
        s_all = jnp.concatenate(scores, axis=1)
        s_new = jnp.sum(qf * new_col(4 + g), axis=1, keepdims=True) * ATTN_SCALE
        m = jnp.maximum(jnp.max(s_all, axis=1, keepdims=True), s_new)
        p_all = jnp.exp(s_all - m)
        p_new = jnp.exp(s_new - m)
        acc = p_new * new_col(6 + g)
        for i in range(top_n):
            acc = acc + _mm_nt(p_all[:, i * page_rows:(i + 1) * page_rows], vbuf[slot, g, i])
        o_sel = acc / (jnp.sum(p_all, axis=1, keepdims=True) + p_new)
        w_rows = win_ref.shape[-1]
        dist_w = (w_rows - lax.broadcasted_iota(I32, (1, w_rows), 1)).astype(F32)
        s_w = _mm(q, win_ref[0, 0, g]) * ATTN_SCALE - slopes * dist_w
        s_w = jnp.where(dist_w < float(WINDOW), s_w, NEG_INF)
        sw_new = jnp.sum(qf * new_col(8 + g), axis=1, keepdims=True) * ATTN_SCALE
        m_w = jnp.maximum(jnp.max(s_w, axis=1, keepdims=True), sw_new)
        p_w = jnp.exp(s_w - m_w)
        pw_new = jnp.exp(sw_new - m_w)
        o_win = (_mm_nt(p_w, win_ref[0, 1, g]) + pw_new * new_col(10 + g)) \
            / (jnp.sum(p_w, axis=1, keepdims=True) + pw_new)
        o = gates[:, 0:1] * ocmp_ref[0] + gates[:, 1:2] * o_sel + gates[:, 2:3] * o_win
        out = jnp.where(head_grp == g, o, out)
    o_ref[0] = out.astype(o_ref.dtype)


def _attn_sample_b_call(sel_idx, page_table, q_heads, cache5, kv_new, win5, gates3, o_cmp,
                        t_pos, top_n):
    bs, n_pages = page_table.shape
    _, _, n_g, hd, page_rows = cache5.shape
    per_page = page_rows // SEL_BLOCK
    n_past_blk = n_pages * per_page
    kern = functools.partial(_attn_sample_b_kernel, t_pos=t_pos, top_n=top_n, n_past_blk=n_past_blk,
                             n_pages=n_pages, per_page=per_page, n_steps=bs)
    per_b = lambda a: pl.BlockSpec((1,) + a.shape[1:], lambda b, sel, pt: (b,) + (0,) * (a.ndim - 1))
    tiles = pltpu.VMEM((2, n_g, top_n, hd, page_rows), F32)
    grid_spec = pltpu.PrefetchScalarGridSpec(
        num_scalar_prefetch=2,
        grid=(bs,),
        in_specs=[per_b(q_heads), pl.BlockSpec(memory_space=pl.ANY),
                  per_b(kv_new), per_b(win5), per_b(gates3), per_b(o_cmp)],
        out_specs=pl.BlockSpec((1, N_HEADS, HEAD_DIM), lambda b, sel, pt: (b, 0, 0)),
        scratch_shapes=[tiles, tiles, pltpu.SemaphoreType.DMA((2,))],
    )
    return pl.pallas_call(
        kern,
        grid_spec=grid_spec,
        out_shape=jax.ShapeDtypeStruct((bs, N_HEADS, HEAD_DIM), MXU_DTYPE),
        compiler_params=_cparams(1),
        name="attn_sample_gather",
    )(sel_idx, page_table.reshape(-1), q_heads, cache5, kv_new, win5, gates3, o_cmp)


def _tiles(sp, n_prompt_tokens, n_pages):
    return {
        "in_proj_rows": min(sp, 1024),
        "attn_queries": min(sp, 2 * LANES),
        "attn_keys": 512 if sp % 512 == 0 else 256,
        "conv_rows": 128,
        "merge_rows": min(sp, 512),
        "compress_pages": min(n_pages, 64),
        "expert_rows": 512,
        "dispatch_tokens": min(n_prompt_tokens, 1024),
        "combine_tokens": min(sp, LANES),
    }


def _prep_weights(w_in, w_a, w_b, w_out, router_w, router_b, sh_w1, sh_w3, sh_w2, g_norm1, g_norm2):
    d = w_in.shape[0]
    n_q = N_HEADS * HEAD_DIM
    n_kv = 6 * KV_HEADS * HEAD_DIM
    n_gate = 3 * N_HEADS
    o_gate = n_q + n_kv
    o_glu = o_gate + n_gate
    o_mrg = o_glu + d
    c = lambda a: a.astype(MXU_DTYPE)
    w_cat = jnp.concatenate(
        [w_in[:, :o_gate], w_in[:, o_glu:o_mrg],
         jnp.pad(w_in[:, o_gate:o_glu], ((0, 0), (0, LANES - n_gate)))], axis=1)
    mw = {"g1": g_norm1.reshape(1, d), "wm": c(w_in[:, o_mrg:]), "wa": c(w_a), "wb": c(w_b),
          "wo": c(w_out), "g2": g_norm2.reshape(1, d), "rwt": c(router_w.T),
          "rb": router_b.reshape(-1, 1), "s13": c(jnp.concatenate([sh_w1, sh_w3], axis=1)),
          "s2": c(sh_w2)}
    return c(w_cat), mw


def kernel(x_prompt, x_sample, cache_kv, cache_win, state_conv, page_table, c_prompt, c_sample, w_ada, b_ada, g_norm1, w_in, cmp_pe, cmp_w1, cmp_w2, conv_w, conv_b, conv_ln_g, conv_ln_b, w_a, w_b, w_out, g_norm2, router_w, router_b, exp_w1, exp_w3, exp_w2, sh_w1, sh_w3, sh_w2, g_final):
    l = 0
    assert w_ada.shape[0] == 1 and x_sample.shape[1] == 1 and cache_win.shape[2] == WINDOW
    bp, sp, d = x_prompt.shape
    bs = x_sample.shape[0]
    page_rows = cache_kv.shape[2]
    n_pages = page_table.shape[1]
    past_len = n_pages * page_rows
    hd = HEAD_DIM
    c_conv = d // 2
    n_sub = d // LANES
    n_kv_new = 4 * KV_HEADS * hd
    tiles = _tiles(sp, bp * sp, n_pages)
    rb = tiles["expert_rows"]

    mod = _ada_call(jnp.concatenate([c_prompt, c_sample], axis=0), w_ada[l], b_ada[l])
    mods_p = [mod[:bp, None, k * d:(k + 1) * d] for k in range(6)]
    mods_s = [mod[None, bp:, k * d:(k + 1) * d] for k in range(6)]
    w_cat, mw = _prep_weights(w_in[l], w_a[l], w_b[l], w_out[l], router_w[l], router_b[l],
                              sh_w1[l], sh_w3[l], sh_w2[l], g_norm1[l], g_norm2[l])
    cw = _compress_weights(cmp_pe[l], cmp_w1[l], cmp_w2[l])

    sh1, sc1, gt1, sh2, sc2, gt2 = mods_p
    q, kvw, glu, gates = _inproj_call(x_prompt, sc1, sh1, g_norm1[l], w_cat, tiles["in_proj_rows"])
    kcvc = _compress_prompt_call(kvw, cw)
    o_attn = _attn_prompt_call(q, kvw, kcvc, gates, tiles["attn_queries"], tiles["attn_keys"])
    z = _conv_prompt_call(glu, conv_w[l], conv_b[l], conv_ln_g[l], conv_ln_b[l], tiles["conv_rows"])
    x1_p, h2_p, ysh_p, eid_p, ew_p, rank_p, cnt_p = _merge_call(
        x_prompt, o_attn, z, (sc1, sh1, gt1, sc2, sh2), mw, tiles["merge_rows"])

    sh1s, sc1s, gt1s, sh2s, sc2s, gt2s = mods_s
    xs = x_sample.reshape(1, bs, d)
    q_s, kvw_s, glu_s, gates_s = _inproj_call(xs, sc1s, sh1s, g_norm1[l], w_cat, bs)
    cache5 = jnp.transpose(cache_kv[l], (0, 2, 3, 4, 1))
    win5 = jnp.transpose(cache_win[l], (0, 2, 3, 4, 1))
    fs = _compress_sample_call(page_table, cache5, cw["w1k"], cw["w1v"], tiles["compress_pages"])
    q_heads = q_s.reshape(bs, N_HEADS, hd)
    n_blk = -(-(past_len + 1) // SEL_BLOCK)
    top_n = min(TOP_N, n_blk)
    o_cmp_s, sel = _attn_sample_a_call(q_heads, fs, cw, past_len, n_blk)
    o_s = _attn_sample_b_call(
        sel[:, :KV_HEADS, :top_n].reshape(-1), page_table, q_heads, cache5,
        kvw_s.reshape(bs, 1, -1), win5,
        gates_s[0, :, :3 * N_HEADS].reshape(bs, N_HEADS, 3), o_cmp_s, past_len, top_n)
    glu_new = glu_s.reshape(bs, 1, c_conv)
    z_s = _conv_sample_call(jnp.transpose(state_conv[l], (1, 0, 2)), glu_s[0],
                            conv_w[l], conv_b[l], conv_ln_g[l], conv_ln_b[l])
    x1_s, h2_s, ysh_s, eid_s, ew_s, rank_s, cnt_s = _merge_call(
        xs, o_s.reshape(1, bs, N_HEADS * hd), z_s.reshape(1, bs, c_conv),
        (sc1s, sh1s, gt1s, sc2s, sh2s), mw, bs)

    n_p = bp * sp
    cnt_p = cnt_p[:, 0].astype(I32)
    cnt_s = cnt_s[:, 0].astype(I32)
    counts = cnt_p + cnt_s
    rank_s = rank_s + cnt_p[eid_s]
    start, padded, blk_e, n_used, n_blk = _moe_plan(counts, (n_p + bs) * TOP_K, rb)
    x_rows, dst_p, dst_s = _dispatch_call(start, counts, padded, n_used, eid_p, rank_p, eid_s,
                                          rank_s, h2_p, h2_s, n_blk, rb, tiles["dispatch_tokens"])
    y_rows = _experts_call(x_rows, blk_e, n_used, exp_w1[l], exp_w3[l], exp_w2[l], rb)
    y_prompt = _combine_call(dst_p, y_rows, ew_p.T, x1_p, ysh_p, gt2, g_final,
                             tiles["combine_tokens"])
    y_sample = _combine_call(dst_s, y_rows, ew_s.T, x1_s, ysh_s, gt2s, g_final, bs)

    w_keep = min(WINDOW, sp)
    kv_prompt = kvw[:, :, :n_kv_new].reshape(1, bp, sp, 4, KV_HEADS, hd)
    kv_sample = kvw_s[0, :, :n_kv_new].reshape(1, bs, 1, 4, KV_HEADS, hd)
    win_prompt = kvw[:, sp - w_keep:, n_kv_new:].reshape(1, bp, w_keep, 2, KV_HEADS, hd)
    win_new = kvw_s[0, :, n_kv_new:].reshape(bs, 1, 2, KV_HEADS, hd)
    win_sample = jnp.concatenate([cache_win[l][:, 1:], win_new], axis=1)[None]
    conv_prompt = glu[:, sp - (CONV_W - 1):][None]
    conv_sample = jnp.concatenate([state_conv[l][:, 1:], glu_new], axis=1)[None]
    return (y_prompt, y_sample.reshape(bs, 1, d), kv_prompt, kv_sample, win_prompt, win_sample,
            conv_prompt, conv_sample)
```
